```python
import jax
import jax.numpy as jnp
from jax import lax
import numpy as np

D_MODEL = 1024
BATCH = 32
SEQ = 2048
DEPTH = 1
DEC_BATCH = 128
DEC_SEQ = 8
PAST_LEN = 8192
PAGE_SIZE = 128

HEAD_DIM = 64
MOBA_HEADS = 8
MOBA_BLOCK = 256
MOBA_TOPK = 3
NSA_HEADS = 8
NSA_GROUPS = 2
NSA_HPG = NSA_HEADS // NSA_GROUPS
NSA_CMP_LEN = 32
NSA_CMP_STRIDE = 16
NSA_CMP_HIDDEN = 128
NSA_SEL_BLOCK = 64
NSA_SEL_TOPN = 8
NSA_WINDOW = 512
D_FF = 2816
CONV_W = 3
Q_BLOCK = 128
RMS_EPS = 1e-6
NEG_INF = -1e30
F32 = jnp.float32
MOBA_W = MOBA_HEADS * HEAD_DIM
NSA_W = NSA_HEADS * HEAD_DIM
NSA_KV_W = NSA_GROUPS * HEAD_DIM
IN_SIZES = (MOBA_W, 2 * MOBA_W, NSA_W, 2 * NSA_KV_W, 2 * NSA_KV_W, 2 * NSA_KV_W, 3 * NSA_HEADS, 2 * D_MODEL)
IN_W = 3 * MOBA_W + NSA_W + 6 * NSA_KV_W + 3 * NSA_HEADS + 2 * D_MODEL

kernel_name = 'moba_nsa_gated_convffn_step'


def _rms(x, g):
    xf = x.astype(F32)
    y = xf * lax.rsqrt(jnp.mean(xf * xf, axis=-1, keepdims=True) + RMS_EPS)
    return (y * g.astype(F32)).astype(x.dtype)


def _alibi_slopes(n):
    return jnp.exp2(-8.0 * jnp.arange(1, n + 1, dtype=F32) / n)


def _masked_softmax(s, mask):
    p = jax.nn.softmax(jnp.where(mask, s, NEG_INF), axis=-1)
    return p * mask.astype(p.dtype)


def _n_qblocks(lq):
    return lq // Q_BLOCK if lq % Q_BLOCK == 0 else 1


def _mixer_inputs(x, p):
    n, L, _ = x.shape
    xn = _rms(x, p['attn_norm_g'])
    cuts = np.cumsum(IN_SIZES)[:-1].tolist()
    mq, mkv, nq, ckv, skv, wkv, ng, mg = jnp.split(xn @ p['w_in'], cuts, axis=-1)
    g = p['qk_norm_g']
    mq = _rms(mq.reshape(n, L, MOBA_HEADS, HEAD_DIM), g[0])
    mkv = mkv.reshape(n, L, 2, MOBA_HEADS, HEAD_DIM)
    mkv = jnp.stack([_rms(mkv[:, :, 0], g[1]), mkv[:, :, 1]], axis=2)
    nq = _rms(nq.reshape(n, L, NSA_HEADS, HEAD_DIM), g[2])
    ckv = ckv.reshape(n, L, 2, NSA_GROUPS, HEAD_DIM)
    skv = skv.reshape(n, L, 2, NSA_GROUPS, HEAD_DIM)
    skv = jnp.stack([_rms(skv[:, :, 0], g[4]), skv[:, :, 1]], axis=2)
    wkv = wkv.reshape(n, L, 2, NSA_GROUPS, HEAD_DIM)
    wkv = jnp.stack([_rms(wkv[:, :, 0], g[5]), wkv[:, :, 1]], axis=2)
    ng = jax.nn.sigmoid(ng.astype(F32)).reshape(n, L, 3, NSA_HEADS)
    mg = jax.nn.sigmoid(mg.astype(F32)).reshape(n, L, 2, D_MODEL)
    return mq, mkv, nq, ckv, skv, wkv, ng, mg


def _moba_select(q, means, qpos):
    nb = means.shape[0]
    own = qpos // MOBA_BLOCK
    s = jnp.einsum('qhd,bhd->qhb', q.astype(F32), means.astype(F32))
    cand = (jnp.arange(nb)[None, :] < own[:, None])[:, None, :]
    _, idx = lax.top_k(jnp.where(cand, s, -jnp.inf), min(MOBA_TOPK, nb))
    return idx, idx < own[:, None, None]


def _moba_attend(q, qpos, kv_sel, kpos_sel, valid_sel, kv_own, kpos_own, slopes):
    scale = HEAD_DIM ** -0.5
    qf = q.astype(F32)
    sl = slopes[None, :, None]
    s_sel = jnp.einsum('qhd,qhnd->qhn', qf, kv_sel[..., 0, :].astype(F32)) * scale
    s_sel = s_sel - sl * (qpos[:, None, None] - kpos_sel).astype(F32)
    s_own = jnp.einsum('qhd,khd->qhk', qf, kv_own[:, 0].astype(F32)) * scale
    s_own = s_own - sl * (qpos[:, None, None] - kpos_own[None, None, :]).astype(F32)
    m_own = jnp.broadcast_to((kpos_own[None, :] <= qpos[:, None])[:, None, :], s_own.shape)
    pr = _masked_softmax(jnp.concatenate([s_sel, s_own], -1), jnp.concatenate([valid_sel, m_own], -1))
    ns = s_sel.shape[-1]
    o = jnp.einsum('qhn,qhnd->qhd', pr[..., :ns], kv_sel[..., 1, :].astype(F32))
    o = o + jnp.einsum('qhk,khd->qhd', pr[..., ns:], kv_own[:, 1].astype(F32))
    return o.astype(q.dtype)


def _moba_prompt(q, kv, slopes):
    b, s = q.shape[:2]
    nb = -(-s // MOBA_BLOCK)
    kvp = jnp.pad(kv, ((0, 0), (0, nb * MOBA_BLOCK - s), (0, 0), (0, 0), (0, 0)))
    kvb = kvp.reshape(b, nb, MOBA_BLOCK, 2, MOBA_HEADS, HEAD_DIM)
    means = jnp.mean(kvb[:, :, :, 0].astype(F32), axis=2)
    kvb_h = kvb.transpose(0, 1, 4, 2, 3, 5)
    hidx = jnp.arange(MOBA_HEADS)[None, :, None]
    nqb = s // Q_BLOCK

    def seq_fn(args):
        q_s, kvb_s, kvbh_s, means_s = args

        def blk_fn(i):
            q0 = i * Q_BLOCK
            qpos = q0 + jnp.arange(Q_BLOCK)
            qb = lax.dynamic_slice_in_dim(q_s, q0, Q_BLOCK, 0)
            idx, valid = _moba_select(qb, means_s, qpos)
            k = idx.shape[-1]
            kv_sel = kvbh_s[idx, hidx].reshape(Q_BLOCK, MOBA_HEADS, k * MOBA_BLOCK, 2, HEAD_DIM)
            kpos_sel = (idx[..., None] * MOBA_BLOCK + jnp.arange(MOBA_BLOCK)).reshape(Q_BLOCK, MOBA_HEADS, k * MOBA_BLOCK)
            valid_sel = jnp.repeat(valid, MOBA_BLOCK, axis=-1)
            ob = q0 // MOBA_BLOCK
            kv_own = lax.dynamic_index_in_dim(kvb_s, ob, 0, keepdims=False)
            kpos_own = ob * MOBA_BLOCK + jnp.arange(MOBA_BLOCK)
            return _moba_attend(qb, qpos, kv_sel, kpos_sel, valid_sel, kv_own, kpos_own, slopes)

        return lax.map(blk_fn, jnp.arange(nqb)).reshape(s, MOBA_HEADS, HEAD_DIM)

    return lax.map(seq_fn, (q, kvb, kvb_h, means))


def _moba_sample(q, kv_new, pool, l, page_sum_k, page_table, slopes):
    db, ds = q.shape[:2]
    n_pages = page_table.shape[1]
    past = n_pages * PAGE_SIZE
    ppb = MOBA_BLOCK // PAGE_SIZE
    nbs = -(-n_pages // ppb)
    sums = jnp.pad(page_sum_k[page_table], ((0, 0), (0, nbs * ppb - n_pages), (0, 0), (0, 0)))
    means = sums.reshape(db, nbs, ppb, MOBA_HEADS, HEAD_DIM).sum(axis=2) / MOBA_BLOCK
    own_start = (past // MOBA_BLOCK) * MOBA_BLOCK
    own_past = pool[l, page_table[:, own_start // PAGE_SIZE:]]
    own_past = own_past.reshape(db, -1, 2, MOBA_HEADS, HEAD_DIM)
    kv_own = jnp.concatenate([own_past, kv_new], axis=1)
    kpos_own = own_start + jnp.arange(kv_own.shape[1])
    qpos = past + jnp.arange(ds)
    hidx = jnp.arange(MOBA_HEADS)[None, :, None, None]

    def seq_fn(args):
        q_s, kv_own_s, pt_s, means_s = args
        idx, valid = _moba_select(q_s, means_s, qpos)
        k = idx.shape[-1]
        pages = pt_s[jnp.minimum(idx[..., None] * ppb + jnp.arange(ppb), n_pages - 1)]
        kv_sel = pool[l, pages, :, :, hidx]
        kv_sel = kv_sel.reshape(ds, MOBA_HEADS, k * MOBA_BLOCK, 2, HEAD_DIM)
        kpos_sel = (idx[..., None] * MOBA_BLOCK + jnp.arange(MOBA_BLOCK)).reshape(ds, MOBA_HEADS, k * MOBA_BLOCK)
        valid_sel = jnp.repeat(valid, MOBA_BLOCK, axis=-1)
        return _moba_attend(q_s, qpos, kv_sel, kpos_sel, valid_sel, kv_own_s, kpos_own, slopes)

    return lax.map(seq_fn, (q, kv_own, page_table, means))


def _nsa_compress(kv, pe, w1, b1, w2, b2):
    n, t = kv.shape[:2]
    r = NSA_CMP_LEN // NSA_CMP_STRIDE
    nch = t // NSA_CMP_STRIDE
    nc = nch - r + 1
    ch = kv[:, :nch * NSA_CMP_STRIDE].reshape(n, nch, NSA_CMP_STRIDE, 2, NSA_GROUPS, HEAD_DIM).astype(F32)
    w1r = w1.reshape(2, r, NSA_CMP_STRIDE, HEAD_DIM, NSA_CMP_HIDDEN).astype(F32)
    per = pe.reshape(2, r, NSA_CMP_STRIDE, HEAD_DIM).astype(F32)
    hid = b1.astype(F32)[:, None, :]
    for j in range(r):
        proj = jnp.einsum('nctkgd,ktdh->nckgh', ch, w1r[:, j])
        hid = hid + proj[:, j:j + nc] + jnp.einsum('ktd,ktdh->kh', per[:, j], w1r[:, j])[:, None, :]
    hid = jax.nn.gelu(hid)
    out = jnp.einsum('nckgh,khd->nckgd', hid, w2.astype(F32)) + b2.astype(F32)[:, None, :]
    return out.astype(kv.dtype)


def _nsa_cmp_branch(q, qpos, ckv_raw, p, slopes):
    n, lq = q.shape[:2]
    ckv = _nsa_compress(ckv_raw, p['cmp_pe'], p['cmp_w1'], p['cmp_b1'], p['cmp_w2'], p['cmp_b2'])
    nc = ckv.shape[1]
    ck = _rms(ckv[:, :, 0], p['qk_norm_g'][3]).astype(F32)
    cv = ckv[:, :, 1].astype(F32)
    cend = jnp.arange(nc) * NSA_CMP_STRIDE + (NSA_CMP_LEN - 1)
    dist = qpos[:, None] - cend[None, :]
    qg = q.reshape(n, lq, NSA_GROUPS, NSA_HPG, HEAD_DIM).astype(F32)
    s = jnp.einsum('nqgpd,ncgd->nqgpc', qg, ck) * HEAD_DIM ** -0.5
    s = s - slopes.reshape(NSA_GROUPS, NSA_HPG)[:, :, None] * dist[:, None, None, :].astype(F32)
    pr = _masked_softmax(s, (dist >= 0)[:, None, None, :])
    o = jnp.einsum('nqgpc,ncgd->nqgpd', pr, cv).reshape(n, lq, NSA_HEADS, HEAD_DIM)
    return o.astype(q.dtype), pr


def _nsa_select(p_cmp, qpos, ns):
    nc = p_cmp.shape[-1]
    c0 = jnp.arange(nc) * NSA_CMP_STRIDE
    j0 = jnp.arange(ns) * NSA_SEL_BLOCK
    overlap = ((c0[:, None] < j0[None, :] + NSA_SEL_BLOCK) & (c0[:, None] + NSA_CMP_LEN > j0[None, :])).astype(F32)
    imp = jnp.einsum('nqgpc,cj->nqgj', p_cmp, overlap)
    cur = qpos // NSA_SEL_BLOCK
    jj = jnp.arange(ns)[None, :]
    forced = (jj == 0) | (jj == cur[:, None]) | (jj == cur[:, None] - 1)
    cand = jj <= cur[:, None]
    imp = jnp.where(forced[:, None, :], jnp.inf, imp)
    imp = jnp.where(cand[:, None, :], imp, -jnp.inf)
    _, idx = lax.top_k(imp, min(NSA_SEL_TOPN, ns))
    return idx, idx <= cur[:, None, None]


def _nsa_sel_branch(q, qpos, skv, idx, valid, slopes):
    n, lq = q.shape[:2]
    t = skv.shape[1]
    ns = -(-t // NSA_SEL_BLOCK)
    kvb = jnp.pad(skv, ((0, 0), (0, ns * NSA_SEL_BLOCK - t), (0, 0), (0, 0), (0, 0)))
    kvb = kvb.reshape(n, ns, NSA_SEL_BLOCK, 2, NSA_GROUPS, HEAD_DIM)
    nqb = _n_qblocks(lq)
    bq = lq // nqb
    gidx = jnp.arange(NSA_GROUPS)[None, :, None]
    sl = slopes.reshape(NSA_GROUPS, NSA_HPG)[None, :, :, None]

    def seq_fn(args):
        q_s, kvb_s, idx_s, valid_s = args

        def blk_fn(i):
            q0 = i * bq
            qp = lax.dynamic_slice_in_dim(qpos, q0, bq)
            qb = lax.dynamic_slice_in_dim(q_s, q0, bq).reshape(bq, NSA_GROUPS, NSA_HPG, HEAD_DIM)
            ib = lax.dynamic_slice_in_dim(idx_s, q0, bq)
            vb = lax.dynamic_slice_in_dim(valid_s, q0, bq)
            nsel = ib.shape[-1]
            kv = kvb_s[ib, :, :, gidx].reshape(bq, NSA_GROUPS, nsel * NSA_SEL_BLOCK, 2, HEAD_DIM)
            kpos = (ib[..., None] * NSA_SEL_BLOCK + jnp.arange(NSA_SEL_BLOCK)).reshape(bq, NSA_GROUPS, nsel * NSA_SEL_BLOCK)
            mask = jnp.repeat(vb, NSA_SEL_BLOCK, axis=-1) & (kpos <= qp[:, None, None])
            s = jnp.einsum('qgpd,qgkd->qgpk', qb.astype(F32), kv[..., 0, :].astype(F32)) * HEAD_DIM ** -0.5
            s = s - sl * (qp[:, None, None, None] - kpos[:, :, None, :]).astype(F32)
            pr = _masked_softmax(s, mask[:, :, None, :])
            o = jnp.einsum('qgpk,qgkd->qgpd', pr, kv[..., 1, :].astype(F32))
            return o.reshape(bq, NSA_HEADS, HEAD_DIM).astype(q.dtype)

        return lax.map(blk_fn, jnp.arange(nqb)).reshape(lq, NSA_HEADS, HEAD_DIM)

    return lax.map(seq_fn, (q, kvb, idx, valid))


def _win_attend(q, qpos, kv, kpos, slopes):
    n, lq = q.shape[:2]
    qg = q.reshape(n, lq, NSA_GROUPS, NSA_HPG, HEAD_DIM).astype(F32)
    s = jnp.einsum('nqgpd,nkgd->nqgpk', qg, kv[:, :, 0].astype(F32)) * HEAD_DIM ** -0.5
    dist = qpos[:, None] - kpos[None, :]
    s = s - slopes.reshape(NSA_GROUPS, NSA_HPG)[:, :, None] * dist[:, None, None, :].astype(F32)
    mask = ((dist >= 0) & (dist <= NSA_WINDOW) & (kpos[None, :] >= 0))[:, None, None, :]
    pr = _masked_softmax(s, mask)
    o = jnp.einsum('nqgpk,nkgd->nqgpd', pr, kv[:, :, 1].astype(F32))
    return o.reshape(n, lq, NSA_HEADS, HEAD_DIM).astype(q.dtype)


def _nsa_win_prompt(q, wkv, slopes):
    b, s = q.shape[:2]
    kvp = jnp.pad(wkv, ((0, 0), (NSA_WINDOW, 0), (0, 0), (0, 0), (0, 0)))
    nk = NSA_WINDOW + Q_BLOCK

    def blk_fn(i):
        q0 = i * Q_BLOCK
        qb = lax.dynamic_slice_in_dim(q, q0, Q_BLOCK, axis=1)
        kv = lax.dynamic_slice_in_dim(kvp, q0, nk, axis=1)
        return _win_attend(qb, q0 + jnp.arange(Q_BLOCK), kv, q0 - NSA_WINDOW + jnp.arange(nk), slopes)

    o = lax.map(blk_fn, jnp.arange(s // Q_BLOCK))
    return jnp.moveaxis(o, 0, 1).reshape(b, s, NSA_HEADS, HEAD_DIM)


def _nsa_attention(q, qpos, ckv_raw, skv, o_w, ng, p, slopes):
    o_c, p_c = _nsa_cmp_branch(q, qpos, ckv_raw, p, slopes)
    ns = -(-skv.shape[1] // NSA_SEL_BLOCK)
    idx, valid = _nsa_select(p_c, qpos, ns)
    o_s = _nsa_sel_branch(q, qpos, skv, idx, valid, slopes)
    o = (ng[:, :, 0, :, None] * o_c.astype(F32) + ng[:, :, 1, :, None] * o_s.astype(F32)
         + ng[:, :, 2, :, None] * o_w.astype(F32))
    return o.astype(q.dtype)


def _conv_ffn(h, conv_state, p):
    L = h.shape[1]
    g, u = jnp.split(_rms(h, p['ffn_norm_g']) @ p['w_ffn_in'], 2, axis=-1)
    gp = jnp.concatenate([conv_state.astype(g.dtype), g], axis=1)
    cw = p['ffn_conv_w']
    gc = p['ffn_conv_b'] + gp[:, 0:L] * cw[0]
    for j in range(1, CONV_W):
        gc = gc + gp[:, j:j + L] * cw[j]
    y = (jax.nn.silu(gc) * u) @ p['w_ffn_down']
    return h + y, gp[:, L:]


def _merge_and_ffn(x, o_moba, o_nsa, mg, conv_state, p):
    n, L, _ = x.shape
    a = (o_moba.reshape(n, L, MOBA_W) @ p['w_branch_moba']).astype(F32)
    b = (o_nsa.reshape(n, L, NSA_W) @ p['w_branch_nsa']).astype(F32)
    merged = (mg[:, :, 0] * a + mg[:, :, 1] * b).astype(x.dtype)
    h = x + merged @ p['w_out']
    return _conv_ffn(h, conv_state, p)


def setup_inputs(seed: int = 0) -> dict:
    key = jax.random.key(seed)
    ks = jax.random.split(key, 24)
    n_pages = PAST_LEN // PAGE_SIZE
    n_used = DEC_BATCH * n_pages
    n_pool = n_used + n_used // 4
    page_table = jax.random.permutation(ks[0], n_pool)[:n_used].reshape(DEC_BATCH, n_pages).astype(jnp.int32)
    win_rows = min(NSA_WINDOW, PAST_LEN)

    def nrm(k, shape, scale=1.0):
        return jax.random.normal(k, shape, F32) * scale

    def gain(k, shape):
        return 1.0 + 0.05 * jax.random.normal(k, shape, F32)

    return {
        'x_prompt': nrm(ks[1], (BATCH, SEQ, D_MODEL)),
        'x_sample': nrm(ks[2], (DEC_BATCH, DEC_SEQ, D_MODEL)),
        'cache_moba_kv': nrm(ks[3], (DEPTH, n_pool, PAGE_SIZE, 2, MOBA_HEADS, HEAD_DIM)),
        'cache_nsa_cmp_kv': nrm(ks[4], (DEPTH, n_pool, PAGE_SIZE, 2, NSA_GROUPS, HEAD_DIM)),
        'cache_nsa_sel_kv': nrm(ks[5], (DEPTH, n_pool, PAGE_SIZE, 2, NSA_GROUPS, HEAD_DIM)),
        'state_nsa_win_kv': nrm(ks[6], (DEPTH, DEC_BATCH, win_rows, 2, NSA_GROUPS, HEAD_DIM)),
        'state_ffn_conv': nrm(ks[7], (DEPTH, DEC_BATCH, CONV_W - 1, D_FF)),
        'page_table': page_table,
        'attn_norm_g': gain(ks[8], (DEPTH, D_MODEL)),
        'w_in': nrm(ks[9], (DEPTH, D_MODEL, IN_W), D_MODEL ** -0.5),
        'qk_norm_g': gain(ks[10], (DEPTH, 6, HEAD_DIM)),
        'nsa_cmp_pe': nrm(ks[11], (DEPTH, 2, NSA_CMP_LEN, HEAD_DIM), 0.1),
        'nsa_cmp_w1': nrm(ks[12], (DEPTH, 2, NSA_CMP_LEN * HEAD_DIM, NSA_CMP_HIDDEN), (NSA_CMP_LEN * HEAD_DIM) ** -0.5),
        'nsa_cmp_b1': nrm(ks[13], (DEPTH, 2, NSA_CMP_HIDDEN), 0.01),
        'nsa_cmp_w2': nrm(ks[14], (DEPTH, 2, NSA_CMP_HIDDEN, HEAD_DIM), NSA_CMP_HIDDEN ** -0.5),
        'nsa_cmp_b2': nrm(ks[15], (DEPTH, 2, HEAD_DIM), 0.01),
        'w_branch_moba': nrm(ks[16], (DEPTH, MOBA_W, D_MODEL), MOBA_W ** -0.5),
        'w_branch_nsa': nrm(ks[17], (DEPTH, NSA_W, D_MODEL), NSA_W ** -0.5),
        'w_out': nrm(ks[18], (DEPTH, D_MODEL, D_MODEL), D_MODEL ** -0.5),
        'ffn_norm_g': gain(ks[19], (DEPTH, D_MODEL)),
        'w_ffn_in': nrm(ks[20], (DEPTH, D_MODEL, 2 * D_FF), D_MODEL ** -0.5),
        'ffn_conv_w': nrm(ks[21], (DEPTH, CONV_W, D_FF), CONV_W ** -0.5),
        'ffn_conv_b': nrm(ks[22], (DEPTH, D_FF), 0.01),
        'w_ffn_down': nrm(ks[23], (DEPTH, D_FF, D_MODEL), D_FF ** -0.5),
    }


def reference(x_prompt, x_sample, cache_moba_kv, cache_nsa_cmp_kv, cache_nsa_sel_kv, state_nsa_win_kv,
              state_ffn_conv, page_table, attn_norm_g, w_in, qk_norm_g, nsa_cmp_pe, nsa_cmp_w1, nsa_cmp_b1,
              nsa_cmp_w2, nsa_cmp_b2, w_branch_moba, w_branch_nsa, w_out, ffn_norm_g, w_ffn_in, ffn_conv_w,
              ffn_conv_b, w_ffn_down):
    b, s = x_prompt.shape[:2]
    db, ds = x_sample.shape[:2]
    n_pages = page_table.shape[1]
    past = n_pages * PAGE_SIZE
    qpos_p = jnp.arange(s)
    qpos_s = past + jnp.arange(ds)
    sl_m = _alibi_slopes(MOBA_HEADS)
    sl_n = _alibi_slopes(NSA_HEADS)
    moba_page_sums = jnp.sum(cache_moba_kv, axis=2, dtype=F32)
    hp, hs = x_prompt, x_sample
    new = [[] for _ in range(10)]
    for l in range(DEPTH):
        p = {'attn_norm_g': attn_norm_g[l], 'w_in': w_in[l], 'qk_norm_g': qk_norm_g[l],
             'cmp_pe': nsa_cmp_pe[l], 'cmp_w1': nsa_cmp_w1[l], 'cmp_b1': nsa_cmp_b1[l],
             'cmp_w2': nsa_cmp_w2[l], 'cmp_b2': nsa_cmp_b2[l], 'w_branch_moba': w_branch_moba[l],
             'w_branch_nsa': w_branch_nsa[l], 'w_out': w_out[l], 'ffn_norm_g': ffn_norm_g[l],
             'w_ffn_in': w_ffn_in[l], 'ffn_conv_w': ffn_conv_w[l], 'ffn_conv_b': ffn_conv_b[l],
             'w_ffn_down': w_ffn_down[l]}
        mq, mkv, nq, ckv, skv, wkv, ng, mg = _mixer_inputs(hp, p)
        o_m = _moba_prompt(mq, mkv, sl_m)
        o_w = _nsa_win_prompt(nq, wkv, sl_n)
        o_n = _nsa_attention(nq, qpos_p, ckv, skv, o_w, ng, p, sl_n)
        hp, conv_p = _merge_and_ffn(hp, o_m, o_n, mg, jnp.zeros((b, CONV_W - 1, D_FF), hp.dtype), p)
        win_p = wkv[:, s - min(NSA_WINDOW, s):]
        for lst, arr in zip(new[:5], (mkv, ckv, skv, win_p, conv_p)):
            lst.append(arr)
        mq, mkv, nq, ckv, skv, wkv, ng, mg = _mixer_inputs(hs, p)
        o_m = _moba_sample(mq, mkv, cache_moba_kv, l, moba_page_sums[l, :, 0], page_table, sl_m)
        ckv_full = jnp.concatenate(
            [cache_nsa_cmp_kv[l, page_table].reshape(db, past, 2, NSA_GROUPS, HEAD_DIM), ckv], axis=1)
        skv_full = jnp.concatenate(
            [cache_nsa_sel_kv[l, page_table].reshape(db, past, 2, NSA_GROUPS, HEAD_DIM), skv], axis=1)
        wcat = jnp.concatenate([state_nsa_win_kv[l], wkv], axis=1)
        wb = state_nsa_win_kv.shape[2]
        o_w = _win_attend(nq, qpos_s, wcat, past - wb + jnp.arange(wcat.shape[1]), sl_n)
        o_n = _nsa_attention(nq, qpos_s, ckv_full, skv_full, o_w, ng, p, sl_n)
        hs, conv_s = _merge_and_ffn(hs, o_m, o_n, mg, state_ffn_conv[l], p)
        win_s = wcat[:, wcat.shape[1] - min(NSA_WINDOW, wcat.shape[1]):]
        for lst, arr in zip(new[5:], (mkv, ckv, skv, win_s, conv_s)):
            lst.append(arr)
    moba_kv_p = jnp.stack(new[0])
    cmp_kv_p = jnp.stack(new[1])
    sel_kv_p = jnp.stack(new[2])
    win_kv_p = jnp.stack(new[3])
    conv_st_p = jnp.stack(new[4])
    moba_kv_s = jnp.stack(new[5])
    cmp_kv_s = jnp.stack(new[6])
    sel_kv_s = jnp.stack(new[7])
    win_kv_s = jnp.stack(new[8])
    conv_st_s = jnp.stack(new[9])
    return (hp, hs, moba_kv_p, moba_kv_s, cmp_kv_p, cmp_kv_s, sel_kv_p, sel_kv_s, win_kv_p, win_kv_s, conv_st_p, conv_st_s)
```

```python
import functools

import numpy as np
import jax
import jax.numpy as jnp
from jax import lax
from jax.experimental import pallas as pl
from jax.experimental.pallas import tpu as pltpu

F32 = jnp.float32
BF16 = jnp.bfloat16

HEAD_DIM = 64
MOBA_HEADS = 8
MOBA_BLOCK = 256
MOBA_TOPK = 3
NSA_HEADS = 8
NSA_GROUPS = 2
NSA_HPG = NSA_HEADS // NSA_GROUPS
CMP_LEN = 32
CMP_STRIDE = 16
CMP_HIDDEN = 128
SEL_BLOCK = 64
SEL_TOPN = 8
WINDOW = 512
CONV_W = 3
PAGE = 128
RMS_EPS = 1e-6
NEG = -1e30
QK_SCALE = HEAD_DIM ** -0.5

LANES = 128
TQ = 256
TM = 256
PAGES_PER_STEP = 8
VMEM_LIMIT = 56 * 1024 * 1024

MOBA_SLOPES = tuple(float(2.0 ** (-8.0 * (h + 1) / MOBA_HEADS)) for h in range(MOBA_HEADS))
NSA_SLOPES = tuple(float(2.0 ** (-8.0 * (h + 1) / NSA_HEADS)) for h in range(NSA_HEADS))


def _dot(a, b):
    return jnp.dot(a, b, preferred_element_type=F32)


def _dot_nt(a, b):
    return lax.dot_general(a, b, (((1,), (1,)), ((), ())), preferred_element_type=F32)


def _div_pow2(x, n):
    assert n & (n - 1) == 0
    return lax.shift_right_logical(x, jnp.int32(n.bit_length() - 1))


def _mod_pow2(x, n):
    assert n & (n - 1) == 0
    return x & (n - 1)


def _split_bf16(x):
    hi = x.astype(BF16)
    lo = (x - hi.astype(F32)).astype(BF16)
    return hi, lo


def _cparams(sem):
    return pltpu.CompilerParams(dimension_semantics=sem, vmem_limit_bytes=VMEM_LIMIT)


def _const_spec(shape):
    nd = len(shape)
    return pl.BlockSpec(shape, lambda *_: (0,) * nd)


def _seg_rms(y, bd, gain):
    outs = []
    for c in range(y.shape[1] // 256):
        s = y[:, c * 256:(c + 1) * 256]
        ss = _dot((s * s).astype(BF16), bd)
        outs.append(s * lax.rsqrt(ss * (1.0 / HEAD_DIM) + RMS_EPS))
    r = outs[0] if len(outs) == 1 else jnp.concatenate(outs, axis=1)
    return r * gain


def _top_rank(val, n, lane):
    rank = jnp.zeros(val.shape, F32)
    for j2 in range(n):
        c = val[:, j2:j2 + 1]
        before = (c > val) | ((c == val) & (lane > j2))
        rank = rank + jnp.where(before, 1.0, 0.0)
    return rank


def _inproj_kernel(x_ref, g_ref, w1_ref, wng_ref, wmg_ref, qkg_ref, bd_ref,
                   mq_ref, mkv_ref, nq_ref, ckv_ref, skv_ref, wkv_ref, ng_ref, mg_ref, ksum_ref):
    x = x_ref[...]
    xn = (x * lax.rsqrt(jnp.mean(x * x, axis=-1, keepdims=True) + RMS_EPS) * g_ref[...]).astype(BF16)
    y = _dot(xn, w1_ref[...])
    bd = bd_ref[...]
    mq_ref[...] = (_seg_rms(y[:, 0:512], bd, qkg_ref[0:1, :]) * QK_SCALE).astype(BF16)
    mk = _seg_rms(y[:, 512:1024], bd, qkg_ref[1:2, :])
    mkv_ref[:, 0:512] = mk
    mkv_ref[:, 512:1024] = y[:, 1024:1536]
    ksum_ref[0] = jnp.sum(mk, axis=0, keepdims=True)
    nq_ref[...] = (_seg_rms(y[:, 1536:2048], bd, qkg_ref[2:3, :]) * QK_SCALE).astype(BF16)
    ckv_ref[...] = y[:, 2048:2304]
    skwk = _seg_rms(y[:, 2304:2560], bd, qkg_ref[3:4, 0:256])
    skv_ref[:, 0:128] = skwk[:, 0:128]
    skv_ref[:, 128:256] = y[:, 2560:2688]
    wkv_ref[:, 0:128] = skwk[:, 128:256]
    wkv_ref[:, 128:256] = y[:, 2688:2816]
    ng_ref[...] = jax.nn.sigmoid(_dot(xn, wng_ref[...]))
    mg_ref[...] = jax.nn.sigmoid(_dot(xn, wmg_ref[...]))


def _inproj(x2d, g_attn, w1, wng, wmg, qkg, bd):
    t, d = x2d.shape
    nt = t // TM
    row = lambda w: pl.BlockSpec((TM, w), lambda i: (i, 0))
    out_shape = (
        jax.ShapeDtypeStruct((t, 512), BF16), jax.ShapeDtypeStruct((t, 1024), F32),
        jax.ShapeDtypeStruct((t, 512), BF16), jax.ShapeDtypeStruct((t, 256), F32),
        jax.ShapeDtypeStruct((t, 256), F32), jax.ShapeDtypeStruct((t, 256), F32),
        jax.ShapeDtypeStruct((t, 128), F32), jax.ShapeDtypeStruct((t, 2 * d), F32),
        jax.ShapeDtypeStruct((nt, 1, 512), F32))
    return pl.pallas_call(
        _inproj_kernel, grid=(nt,),
        in_specs=[row(d), _const_spec(g_attn.shape), _const_spec(w1.shape), _const_spec(wng.shape),
                  _const_spec(wmg.shape), _const_spec(qkg.shape), _const_spec(bd.shape)],
        out_specs=(row(512), row(1024), row(512), row(256), row(256), row(256), row(128), row(2 * d),
                   pl.BlockSpec((1, 1, 512), lambda i: (i, 0, 0))),
        out_shape=out_shape, compiler_params=_cparams(("arbitrary",)), name="inproj",
    )(x2d, g_attn, w1, wng, wmg, qkg, bd)


def _masked_q(q_ref, h, half_id):
    pair, half = divmod(h, 2)
    qp = q_ref[0, :, pair * LANES:(pair + 1) * LANES].astype(F32)
    return jnp.where(half_id == half, qp, 0.0).astype(BF16)


def _flash_update(s, mask, v_blk, h, m_sc, l_sc, acc_sc, lane, half_id):
    pair, half = divmod(h, 2)
    s = jnp.where(mask, s, NEG)
    m_all = m_sc[...]
    l_all = l_sc[...]
    m_old = m_all[:, h:h + 1]
    l_old = l_all[:, h:h + 1]
    m_new = jnp.maximum(m_old, jnp.max(s, axis=1, keepdims=True))
    alpha = jnp.exp(m_old - m_new)
    p = jnp.exp(s - m_new)
    l_new = alpha * l_old + jnp.sum(p, axis=1, keepdims=True)
    pv = _dot(p.astype(BF16), v_blk)
    acc_old = acc_sc[:, pair * LANES:(pair + 1) * LANES]
    acc_sc[:, pair * LANES:(pair + 1) * LANES] = jnp.where(half_id == half, alpha * acc_old + pv, acc_old)
    m_sc[...] = jnp.where(lane == h, m_new, m_all)
    l_sc[...] = jnp.where(lane == h, l_new, l_all)


def _pair_cols(a, pair, half_id):
    return jnp.where(half_id == 0, a[:, 2 * pair:2 * pair + 1], a[:, 2 * pair + 1:2 * pair + 2])


def _moba_prompt_kernel(q_ref, k_ref, v_ref, mean_ref, o_ref, sel_sc, m_sc, l_sc, acc_sc, *, nb):
    i = pl.program_id(1)
    lane = lax.broadcasted_iota(jnp.int32, (TQ, LANES), 1)
    half_id = _div_pow2(lane, HEAD_DIM)
    rel = (lax.broadcasted_iota(jnp.int32, (TQ, TQ), 1) - lax.broadcasted_iota(jnp.int32, (TQ, TQ), 0)).astype(F32)
    m_sc[...] = jnp.full((TQ, LANES), NEG, F32)
    l_sc[...] = jnp.zeros((TQ, LANES), F32)
    acc_sc[...] = jnp.zeros(acc_sc.shape, F32)

    cand = lane < i
    for h in range(MOBA_HEADS):
        pair = h // 2
        qm = _masked_q(q_ref, h, half_id)
        mh, ml = _split_bf16(mean_ref[0, :, pair * LANES:(pair + 1) * LANES])
        ss = _dot_nt(qm, mh) + _dot_nt(qm, ml)
        rank = _top_rank(jnp.where(cand, ss, -jnp.inf), nb, lane)
        sel_sc[h] = jnp.where(cand & (rank < MOBA_TOPK), 1.0, 0.0)

    def kv_block(kb, own):
        start = pl.multiple_of(kb * TQ, TQ)
        k_blk = k_ref[0, pl.ds(start, TQ), :]
        v_blk = v_ref[0, pl.ds(start, TQ), :]
        dist = rel + ((kb - i) * TQ).astype(F32)
        for h in range(MOBA_HEADS):
            pair = h // 2
            qm = _masked_q(q_ref, h, half_id)
            s = _dot_nt(qm, k_blk[:, pair * LANES:(pair + 1) * LANES]) + MOBA_SLOPES[h] * dist
            if own:
                mask = rel <= 0.0
            else:
                rowsel = jnp.max(jnp.where(lane == kb, sel_sc[h], 0.0), axis=1, keepdims=True)
                mask = jnp.broadcast_to(rowsel > 0.5, s.shape)
            _flash_update(s, mask, v_blk[:, pair * LANES:(pair + 1) * LANES], h, m_sc, l_sc, acc_sc, lane, half_id)

    def body(kb, carry):
        kv_block(kb, False)
        return carry

    lax.fori_loop(0, i, body, 0)
    kv_block(i, True)

    l_all = l_sc[...]
    for pair in range(MOBA_HEADS // 2):
        acc = acc_sc[:, pair * LANES:(pair + 1) * LANES]
        o_ref[0, :, pair * LANES:(pair + 1) * LANES] = (acc / _pair_cols(l_all, pair, half_id)).astype(BF16)


def _moba_prompt(q, k, v, means):
    b, s, w = q.shape
    nb = s // TQ
    qspec = pl.BlockSpec((1, TQ, w), lambda bi, i: (bi, i, 0))
    full = pl.BlockSpec((1, s, w), lambda bi, i: (bi, 0, 0))
    return pl.pallas_call(
        functools.partial(_moba_prompt_kernel, nb=nb), grid=(b, nb),
        in_specs=[qspec, full, full, pl.BlockSpec((1, LANES, w), lambda bi, i: (bi, 0, 0))],
        out_specs=qspec, out_shape=jax.ShapeDtypeStruct((b, s, w), BF16),
        scratch_shapes=[pltpu.VMEM((MOBA_HEADS, TQ, LANES), F32), pltpu.VMEM((TQ, LANES), F32),
                        pltpu.VMEM((TQ, LANES), F32), pltpu.VMEM((TQ, w), F32)],
        compiler_params=_cparams(("arbitrary", "arbitrary")), name="moba_prompt",
    )(q, k, v, means)


def _compress_kernel(*refs, n_x, n_prefetch):
    refs = refs[n_prefetch:]
    x_refs = refs[:n_x]
    wfull_ref, pe_ref, w1_ref, b1_ref, w2_ref, b2_ref, gk_ref, bd_ref, ck_ref, cv_ref = refs[n_x:]
    if n_x == 1:
        x = x_refs[0][0]
    else:
        x = jnp.concatenate([r[0] for r in x_refs], axis=0)
    ncp = x.shape[0]
    proj = _dot(x.astype(BF16), wfull_ref[...])
    hids = []
    for kvg in range(2 * NSA_GROUPS):
        kv = kvg // NSA_GROUPS
        pe_term = _dot(pe_ref[kv], w1_ref[kv])[0:1, :]
        c0 = kvg * 2 * CMP_HIDDEN
        p0 = proj[:, c0:c0 + CMP_HIDDEN]
        p1 = pltpu.roll(proj[:, c0 + CMP_HIDDEN:c0 + 2 * CMP_HIDDEN], ncp - 1, 0)
        hids.append(jax.nn.gelu(b1_ref[kv:kv + 1, :] + p0 + p1 + pe_term))
    hid = jnp.concatenate(hids, axis=1).astype(BF16)
    out = _dot(hid, w2_ref[...]) + b2_ref[...]
    ck = _seg_rms(out[:, 0:256], bd_ref[...], gk_ref[...])
    ck_ref[0] = ck.astype(BF16)
    cv_ref[0] = out[:, 256:512].astype(BF16)


def _compress_call(x_args, x_specs, n, ncp, grid, cw, num_prefetch=0, prefetch_args=()):
    consts = (cw["wfull"], cw["pe"], cw["w1"], cw["b1"], cw["w2bd"], cw["b2row"], cw["gk"], cw["bd"])
    if num_prefetch:
        cspec = lambda a: pl.BlockSpec(a.shape, lambda *_, nd=a.ndim: (0,) * nd)
        ospec = pl.BlockSpec((1, ncp, 256), lambda bi, pt: (bi, 0, 0))
    else:
        cspec = lambda a: _const_spec(a.shape)
        ospec = pl.BlockSpec((1, ncp, 256), lambda bi: (bi, 0, 0))
    gs = pltpu.PrefetchScalarGridSpec(
        num_scalar_prefetch=num_prefetch, grid=grid,
        in_specs=list(x_specs) + [cspec(a) for a in consts], out_specs=(ospec, ospec))
    return pl.pallas_call(
        functools.partial(_compress_kernel, n_x=len(x_args), n_prefetch=num_prefetch), grid_spec=gs,
        out_shape=(jax.ShapeDtypeStruct((n, ncp, 256), BF16), jax.ShapeDtypeStruct((n, ncp, 256), BF16)),
        compiler_params=_cparams(("arbitrary",)), name="nsa_compress",
    )(*prefetch_args, *x_args, *consts)


def _nsa_prompt_kernel(q_ref, ck_ref, cv_ref, sk_ref, sv_ref, wk_ref, wv_ref, ng_ref, ovl_ref, exp_ref, o_ref,
                       sel_sc, oc_sc, ms_sc, ls_sc, as_sc, mw_sc, lw_sc, aw_sc, *, ns, ncp):
    i = pl.program_id(1)
    lane = lax.broadcasted_iota(jnp.int32, (TQ, LANES), 1)
    half_id = _div_pow2(lane, HEAD_DIM)
    rel = (lax.broadcasted_iota(jnp.int32, (TQ, TQ), 1) - lax.broadcasted_iota(jnp.int32, (TQ, TQ), 0)).astype(F32)
    qpos = i * TQ + lax.broadcasted_iota(jnp.int32, (TQ, 1), 0)
    for m_ref, l_ref, a_ref in ((ms_sc, ls_sc, as_sc), (mw_sc, lw_sc, aw_sc)):
        m_ref[...] = jnp.full((TQ, LANES), NEG, F32)
        l_ref[...] = jnp.zeros((TQ, LANES), F32)
        a_ref[...] = jnp.zeros(a_ref.shape, F32)

    cend = lax.broadcasted_iota(jnp.int32, (TQ, ncp), 1) * CMP_STRIDE + (CMP_LEN - 1)
    cdist = (cend - qpos).astype(F32)
    cmask = cend <= qpos
    cur = _div_pow2(qpos, SEL_BLOCK)
    forced = (lane == 0) | (lane == cur) | (lane == cur - 1)
    cand = lane <= cur
    for g in range(NSA_GROUPS):
        ck = ck_ref[0, :, g * LANES:(g + 1) * LANES]
        cv = cv_ref[0, :, g * LANES:(g + 1) * LANES]
        psum = jnp.zeros((TQ, ncp), F32)
        for p_ in range(NSA_HPG):
            h = g * NSA_HPG + p_
            pair, half = divmod(h, 2)
            qm = _masked_q(q_ref, h, half_id)
            s = jnp.where(cmask, _dot_nt(qm, ck) + NSA_SLOPES[h] * cdist, NEG)
            e = jnp.where(cmask, jnp.exp(s - jnp.max(s, axis=1, keepdims=True)), 0.0)
            den = jnp.sum(e, axis=1, keepdims=True)
            pn = e / jnp.where(den > 0.0, den, 1.0)
            psum = psum + pn
            oc = _dot(pn.astype(BF16), cv)
            old = oc_sc[:, pair * LANES:(pair + 1) * LANES]
            oc_sc[:, pair * LANES:(pair + 1) * LANES] = jnp.where(half_id == half, oc, old)
        ph, plo = _split_bf16(psum)
        imp = _dot(ph, ovl_ref[...]) + _dot(plo, ovl_ref[...])
        val = jnp.where(cand, jnp.where(forced, jnp.inf, imp), -jnp.inf)
        rank = _top_rank(val, ns, lane)
        sel_sc[g] = jnp.where(cand & (rank < SEL_TOPN), 1.0, 0.0)

    def sel_block(kb, carry):
        start = pl.multiple_of(kb * TQ, TQ)
        k_blk = sk_ref[0, pl.ds(start, TQ), :]
        v_blk = sv_ref[0, pl.ds(start, TQ), :]
        dist = rel + ((kb - i) * TQ).astype(F32)
        causal = dist <= 0.0
        for g in range(NSA_GROUPS):
            picked = _dot(sel_sc[g].astype(BF16), exp_ref[kb])
            mask = causal & (picked > 0.5)
            for p_ in range(NSA_HPG):
                h = g * NSA_HPG + p_
                qm = _masked_q(q_ref, h, half_id)
                s = _dot_nt(qm, k_blk[:, g * LANES:(g + 1) * LANES]) + NSA_SLOPES[h] * dist
                _flash_update(s, mask, v_blk[:, g * LANES:(g + 1) * LANES], h, ms_sc, ls_sc, as_sc, lane, half_id)
        return carry

    def win_block(kb, carry):
        start = pl.multiple_of(kb * TQ, TQ)
        k_blk = wk_ref[0, pl.ds(start, TQ), :]
        v_blk = wv_ref[0, pl.ds(start, TQ), :]
        dist = rel + ((kb - i) * TQ).astype(F32)
        mask = (dist <= 0.0) & (dist >= -float(WINDOW))
        for h in range(NSA_HEADS):
            g = h // NSA_HPG
            qm = _masked_q(q_ref, h, half_id)
            s = _dot_nt(qm, k_blk[:, g * LANES:(g + 1) * LANES]) + NSA_SLOPES[h] * dist
            _flash_update(s, mask, v_blk[:, g * LANES:(g + 1) * LANES], h, mw_sc, lw_sc, aw_sc, lane, half_id)
        return carry

    lax.fori_loop(0, i + 1, sel_block, 0)
    lax.fori_loop(jnp.maximum(i - WINDOW // TQ, 0), i + 1, win_block, 0)

    ng = ng_ref[0]
    ls = ls_sc[...]
    lw = lw_sc[...]
    for pair in range(NSA_HEADS // 2):
        sl = slice(pair * LANES, (pair + 1) * LANES)
        o_c = oc_sc[:, sl]
        o_s = as_sc[:, sl] / _pair_cols(ls, pair, half_id)
        o_w = aw_sc[:, sl] / _pair_cols(lw, pair, half_id)
        g_c = _pair_cols(ng[:, 0:NSA_HEADS], pair, half_id)
        g_s = _pair_cols(ng[:, NSA_HEADS:2 * NSA_HEADS], pair, half_id)
        g_w = _pair_cols(ng[:, 2 * NSA_HEADS:3 * NSA_HEADS], pair, half_id)
        o_ref[0, :, sl] = (g_c * o_c + g_s * o_s + g_w * o_w).astype(BF16)


def _nsa_prompt(q, ck, cv, sk, sv, wk, wv, ng, ovl, expand):
    b, s, w = q.shape
    ncp = ck.shape[1]
    ns = s // SEL_BLOCK
    qspec = pl.BlockSpec((1, TQ, w), lambda bi, i: (bi, i, 0))
    full = lambda a: pl.BlockSpec((1,) + a.shape[1:], lambda bi, i: (bi, 0, 0))
    return pl.pallas_call(
        functools.partial(_nsa_prompt_kernel, ns=ns, ncp=ncp), grid=(b, s // TQ),
        in_specs=[qspec, full(ck), full(cv), full(sk), full(sv), full(wk), full(wv),
                  pl.BlockSpec((1, TQ, LANES), lambda bi, i: (bi, i, 0)),
                  _const_spec(ovl.shape), _const_spec(expand.shape)],
        out_specs=qspec, out_shape=jax.ShapeDtypeStruct((b, s, w), BF16),
        scratch_shapes=[pltpu.VMEM((NSA_GROUPS, TQ, LANES), F32), pltpu.VMEM((TQ, w), F32),
                        pltpu.VMEM((TQ, LANES), F32), pltpu.VMEM((TQ, LANES), F32), pltpu.VMEM((TQ, w), F32),
                        pltpu.VMEM((TQ, LANES), F32), pltpu.VMEM((TQ, LANES), F32), pltpu.VMEM((TQ, w), F32)],
        compiler_params=_cparams(("arbitrary", "arbitrary")), name="nsa_prompt",
    )(q, ck, cv, sk, sv, wk, wv, ng, ovl, expand)


def _moba_sample_kernel(pt_ref, q_ref, kn_ref, vn_ref, *rest, past, ds, nbs):
    pg_refs = rest[:PAGES_PER_STEP]
    o_ref, m_sc, l_sc, o_sc, ksum_sc = rest[PAGES_PER_STEP:]
    c = pl.program_id(1)
    nch = pl.num_programs(1)
    rows = MOBA_HEADS * ds
    w = MOBA_HEADS * HEAD_DIM
    ppb = MOBA_BLOCK // PAGE
    row_w = lax.broadcasted_iota(jnp.int32, (rows, w), 0)
    diag = _div_pow2(lax.broadcasted_iota(jnp.int32, (rows, w), 1), HEAD_DIM) == _div_pow2(row_w, ds)
    lane = lax.broadcasted_iota(jnp.int32, (rows, LANES), 1)
    row1 = lax.broadcasted_iota(jnp.int32, (rows, 1), 0)
    qi = _mod_pow2(row1, ds)
    slope = jnp.zeros((rows, 1), F32)
    for h in range(MOBA_HEADS):
        slope = jnp.where(_div_pow2(row1, ds) == h, MOBA_SLOPES[h], slope)
    q8 = q_ref[0].astype(F32)
    qbd = jnp.where(diag, jnp.concatenate([q8] * MOBA_HEADS, axis=0), 0.0).astype(BF16)

    @pl.when(c == 0)
    def _():
        m_sc[...] = jnp.full(m_sc.shape, NEG, F32)
        l_sc[...] = jnp.zeros(l_sc.shape, F32)
        ksum_sc[...] = jnp.zeros(ksum_sc.shape, F32)

    col = lax.broadcasted_iota(jnp.int32, (rows, MOBA_BLOCK), 1)
    for bb in range(PAGES_PER_STEP // ppb):
        blk = c * (PAGES_PER_STEP // ppb) + bb
        pages = [pg_refs[bb * ppb + j][0] for j in range(ppb)]
        k_f = jnp.concatenate([p_[:, 0:w] for p_ in pages], axis=0)
        v_b = jnp.concatenate([p_[:, w:2 * w] for p_ in pages], axis=0).astype(BF16)
        ksum_sc[pl.ds(blk, 1), :] = jnp.sum(k_f, axis=0, keepdims=True)
        kdist = (col - qi + (blk * MOBA_BLOCK - past)).astype(F32)
        s = _dot_nt(qbd, k_f.astype(BF16)) + slope * kdist
        m_b = jnp.max(s, axis=1, keepdims=True)
        p = jnp.exp(s - m_b)
        l_b = jnp.sum(p, axis=1, keepdims=True)
        o_sc[blk] = jnp.where(diag, _dot(p.astype(BF16), v_b), 0.0)
        m_sc[...] = jnp.where(lane == blk, m_b, m_sc[...])
        l_sc[...] = jnp.where(lane == blk, l_b, l_sc[...])

    @pl.when(c == nch - 1)
    def _():
        mh, ml = _split_bf16(ksum_sc[...] * (1.0 / MOBA_BLOCK))
        ss = _dot_nt(qbd, mh) + _dot_nt(qbd, ml)
        own = past // MOBA_BLOCK
        cand = lane < own
        rank = _top_rank(jnp.where(cand, ss, -jnp.inf), nbs, lane)
        selb = cand & (rank < MOBA_TOPK)
        m_all = m_sc[...]
        m_past = jnp.max(jnp.where(selb, m_all, NEG), axis=1, keepdims=True)
        zpad = jnp.zeros((LANES - ds, w), F32)
        k_new = jnp.concatenate([kn_ref[0], zpad], axis=0).astype(BF16)
        v_new = jnp.concatenate([vn_ref[0], zpad], axis=0).astype(BF16)
        s_own = _dot_nt(qbd, k_new) + slope * (lane - qi).astype(F32)
        s_own = jnp.where(lane <= qi, s_own, NEG)
        m_tot = jnp.maximum(m_past, jnp.max(s_own, axis=1, keepdims=True))
        wgt = jnp.where(selb, jnp.exp(m_all - m_tot), 0.0)
        p_own = jnp.exp(s_own - m_tot)
        den = jnp.sum(wgt * l_sc[...], axis=1, keepdims=True) + jnp.sum(p_own, axis=1, keepdims=True)
        out = jnp.where(diag, _dot(p_own.astype(BF16), v_new), 0.0)
        for b2 in range(nbs):
            out = out + wgt[:, b2:b2 + 1] * o_sc[b2]
        out = out / den
        o8 = out[0:ds, :]
        for h in range(1, MOBA_HEADS):
            o8 = o8 + out[h * ds:(h + 1) * ds, :]
        o_ref[0] = o8.astype(BF16)


def _moba_sample(page_table, q, k_new, v_new, pool):
    db, ds, w = q.shape
    n_pages = page_table.shape[1]
    past = n_pages * PAGE
    nbs = past // MOBA_BLOCK
    assert n_pages % PAGES_PER_STEP == 0 and past % MOBA_BLOCK == 0 and nbs <= LANES and ds == 8
    tok = pl.BlockSpec((1, ds, w), lambda bi, c, pt: (bi, 0, 0))
    pg_specs = [pl.BlockSpec((1, PAGE, 2 * w), lambda bi, c, pt, j=j: (pt[bi, c * PAGES_PER_STEP + j], 0, 0))
                for j in range(PAGES_PER_STEP)]
    gs = pltpu.PrefetchScalarGridSpec(
        num_scalar_prefetch=1, grid=(db, n_pages // PAGES_PER_STEP),
        in_specs=[tok, tok, tok] + pg_specs, out_specs=tok,
        scratch_shapes=[pltpu.VMEM((MOBA_HEADS * ds, LANES), F32), pltpu.VMEM((MOBA_HEADS * ds, LANES), F32),
                        pltpu.VMEM((nbs, MOBA_HEADS * ds, w), F32), pltpu.VMEM((LANES, w), F32)])
    return pl.pallas_call(
        functools.partial(_moba_sample_kernel, past=past, ds=ds, nbs=nbs), grid_spec=gs,
        out_shape=jax.ShapeDtypeStruct((db, ds, w), BF16),
        compiler_params=_cparams(("arbitrary", "arbitrary")), name="moba_sample",
    )(page_table, q, k_new, v_new, *([pool] * PAGES_PER_STEP))


def _nsa_sample_kernel(pt_ref, qp_ref, qg_ref, ck_ref, cv_ref, skn_ref, wkn_ref, wst_ref, ng_ref, ovl_ref, exp_ref,
                       *rest, past, ds, n_pages, ns):
    pg_refs = rest[:n_pages]
    o_ref = rest[n_pages]
    rows = NSA_HEADS * ds
    grows = NSA_HPG * ds
    ncp = ck_ref.shape[1]
    nsp = ovl_ref.shape[1]
    wb = wst_ref.shape[1]
    row1 = lax.broadcasted_iota(jnp.int32, (rows, 1), 0)
    qi = _mod_pow2(row1, ds)
    qpos = past + qi
    slope = jnp.zeros((rows, 1), F32)
    for h in range(NSA_HEADS):
        slope = jnp.where(_div_pow2(row1, ds) == h, NSA_SLOPES[h], slope)
    lane = lax.broadcasted_iota(jnp.int32, (rows, LANES), 1)
    grp_half = _div_pow2(lane, HEAD_DIM) == _div_pow2(row1, grows)
    q_pair = qp_ref[0]
    q_grp = qg_ref[0]

    def softmax_parts(s_list, m_list):
        m = None
        for s, mk in zip(s_list, m_list):
            mm = jnp.max(jnp.where(mk, s, NEG), axis=1, keepdims=True)
            m = mm if m is None else jnp.maximum(m, mm)
        ps = [jnp.where(mk, jnp.exp(jnp.where(mk, s, NEG) - m), 0.0) for s, mk in zip(s_list, m_list)]
        den = sum(jnp.sum(p, axis=1, keepdims=True) for p in ps)
        return ps, jnp.where(den > 0.0, den, 1.0)

    def own_group(o):
        return jnp.where(grp_half, o, pltpu.roll(o, HEAD_DIM, 1))

    cend = lax.broadcasted_iota(jnp.int32, (rows, ncp), 1) * CMP_STRIDE + (CMP_LEN - 1)
    cmask = cend <= qpos
    cbias = slope * (cend - qpos).astype(F32)
    o_c_parts, sel_parts = [], []
    lane_s = lax.broadcasted_iota(jnp.int32, (ds, nsp), 1)
    cur = _div_pow2(past + lax.broadcasted_iota(jnp.int32, (ds, 1), 0), SEL_BLOCK)
    forced = (lane_s == 0) | (lane_s == cur) | (lane_s == cur - 1)
    cand = lane_s <= cur
    for g in range(NSA_GROUPS):
        rs = slice(g * grows, (g + 1) * grows)
        s = _dot_nt(q_pair[rs], ck_ref[0, :, g * LANES:(g + 1) * LANES]) + cbias[rs]
        (e,), den = softmax_parts([s], [cmask[rs]])
        pn = e / den
        o_c_parts.append(_dot(pn.astype(BF16), cv_ref[0, :, g * LANES:(g + 1) * LANES]))
        psum = pn[0:ds]
        for p_ in range(1, NSA_HPG):
            psum = psum + pn[p_ * ds:(p_ + 1) * ds]
        ph, plo = _split_bf16(psum)
        imp = _dot(ph, ovl_ref[...]) + _dot(plo, ovl_ref[...])
        val = jnp.where(cand, jnp.where(forced, jnp.inf, imp), -jnp.inf)
        rank = _top_rank(val, ns, lane_s)
        sel_g = jnp.where(cand & (rank < SEL_TOPN), 1.0, 0.0)
        sel_parts.append(jnp.concatenate([sel_g] * NSA_HPG, axis=0))
    o_c = jnp.concatenate(o_c_parts, axis=0)
    sel_f32 = jnp.concatenate(sel_parts, axis=0)
    sel_rows = sel_f32.astype(BF16)

    pages = [r[0] for r in pg_refs]
    k_all = jnp.concatenate([p_[:, 0:LANES] for p_ in pages], axis=0).astype(BF16)
    v_all = jnp.concatenate([p_[:, LANES:2 * LANES] for p_ in pages], axis=0).astype(BF16)
    kpos = lax.broadcasted_iota(jnp.int32, (rows, past), 1)
    s_past = _dot_nt(q_grp, k_all) + slope * (kpos - qpos).astype(F32)
    m_past = _dot(sel_rows, exp_ref[...]) > 0.5
    zpad = jnp.zeros((LANES - ds, 2 * LANES), F32)
    skn = jnp.concatenate([skn_ref[0], zpad], axis=0)
    s_new = _dot_nt(q_grp, skn[:, 0:LANES].astype(BF16)) + slope * (lane - qi).astype(F32)
    new_blk = past // SEL_BLOCK
    m_new = (lane <= qi) & (sel_f32[:, new_blk:new_blk + 1] > 0.5)
    (p_past, p_new), den = softmax_parts([s_past, s_new], [m_past, m_new])
    o_s = _dot(p_past.astype(BF16), v_all) + _dot(p_new.astype(BF16), skn[:, LANES:2 * LANES].astype(BF16))
    o_s = own_group(o_s / den)

    wst = wst_ref[0]
    jst = lax.broadcasted_iota(jnp.int32, (rows, wb), 1)
    dist_st = qpos - (past - wb + jst)
    s_st = _dot_nt(q_grp, wst[:, 0:LANES].astype(BF16)) - slope * dist_st.astype(F32)
    m_st = (dist_st >= 0) & (dist_st <= WINDOW)
    wkn = jnp.concatenate([wkn_ref[0], zpad], axis=0)
    s_wn = _dot_nt(q_grp, wkn[:, 0:LANES].astype(BF16)) + slope * (lane - qi).astype(F32)
    m_wn = lane <= qi
    (p_st, p_wn), den = softmax_parts([s_st, s_wn], [m_st, m_wn])
    o_w = _dot(p_st.astype(BF16), wst[:, LANES:2 * LANES].astype(BF16)) \
        + _dot(p_wn.astype(BF16), wkn[:, LANES:2 * LANES].astype(BF16))
    o_w = own_group(o_w / den)

    ng = ng_ref[0]
    gate = lambda br: jnp.concatenate([ng[:, br * NSA_HEADS + h:br * NSA_HEADS + h + 1] for h in range(NSA_HEADS)],
                                      axis=0)
    o = gate(0) * o_c + gate(1) * o_s + gate(2) * o_w
    lane8 = lax.broadcasted_iota(jnp.int32, (ds, LANES), 1)
    for pair in range(NSA_HEADS // 2):
        o_even = o[(2 * pair) * ds:(2 * pair + 1) * ds]
        o_odd = o[(2 * pair + 1) * ds:(2 * pair + 2) * ds]
        o_ref[0, :, pair * LANES:(pair + 1) * LANES] = jnp.where(lane8 < HEAD_DIM, o_even, o_odd).astype(BF16)


def _nsa_sample(page_table, q_pair, q_grp, ck, cv, sk_new, wk_new, wstate, ng, ovl, expand, pool):
    db, rows, _ = q_pair.shape
    ds = rows // NSA_HEADS
    n_pages = page_table.shape[1]
    past = n_pages * PAGE
    ns = -(-(past + ds) // SEL_BLOCK)
    per_b = lambda a: pl.BlockSpec((1,) + a.shape[1:], lambda bi, pt: (bi, 0, 0))
    cst = lambda a: pl.BlockSpec(a.shape, lambda bi, pt, nd=a.ndim: (0,) * nd)
    pg_specs = [pl.BlockSpec((1, PAGE, 2 * LANES), lambda bi, pt, j=j: (pt[bi, j], 0, 0)) for j in range(n_pages)]
    gs = pltpu.PrefetchScalarGridSpec(
        num_scalar_prefetch=1, grid=(db,),
        in_specs=[per_b(q_pair), per_b(q_grp), per_b(ck), per_b(cv), per_b(sk_new), per_b(wk_new), per_b(wstate),
                  per_b(ng), cst(ovl), cst(expand)] + pg_specs,
        out_specs=pl.BlockSpec((1, ds, NSA_HEADS * HEAD_DIM), lambda bi, pt: (bi, 0, 0)))
    return pl.pallas_call(
        functools.partial(_nsa_sample_kernel, past=past, ds=ds, n_pages=n_pages, ns=ns), grid_spec=gs,
        out_shape=jax.ShapeDtypeStruct((db, ds, NSA_HEADS * HEAD_DIM), BF16),
        compiler_params=_cparams(("arbitrary",)), name="nsa_sample",
    )(page_table, q_pair, q_grp, ck, cv, sk_new, wk_new, wstate, ng, ovl, expand, *([pool] * n_pages))


def _merge_kernel(x_ref, om_ref, on_ref, mg_ref, wa_ref, wb_ref, wo_ref, h_ref):
    d = x_ref.shape[1]
    a = _dot(om_ref[...], wa_ref[...])
    b = _dot(on_ref[...], wb_ref[...])
    merged = mg_ref[:, 0:d] * a + mg_ref[:, d:2 * d] * b
    h_ref[...] = x_ref[...] + _dot(merged.astype(BF16), wo_ref[...])


def _merge(x2d, o_m, o_n, mg, wa, wb, wo):
    t, d = x2d.shape
    row = lambda w: pl.BlockSpec((TM, w), lambda i: (i, 0))
    return pl.pallas_call(
        _merge_kernel, grid=(t // TM,),
        in_specs=[row(d), row(o_m.shape[1]), row(o_n.shape[1]), row(2 * d),
                  _const_spec(wa.shape), _const_spec(wb.shape), _const_spec(wo.shape)],
        out_specs=row(d), out_shape=jax.ShapeDtypeStruct((t, d), F32),
        compiler_params=_cparams(("arbitrary",)), name="merge",
    )(x2d, o_m, o_n, mg, wa, wb, wo)


def _ffn_kernel(h_ref, *rest, seq_len, per_seq_state):
    if per_seq_state:
        st_ref, gf_ref, win_ref, cw_ref, cb_ref, wd_ref, y_ref, cs_ref, carry_sc = rest
    else:
        st0_ref, st1_ref, gf_ref, win_ref, cw_ref, cb_ref, wd_ref, y_ref, g_ref = rest
    f = cw_ref.shape[1]
    h = h_ref[...]
    hn = (h * lax.rsqrt(jnp.mean(h * h, axis=-1, keepdims=True) + RMS_EPS) * gf_ref[...]).astype(BF16)
    gu = _dot(hn, win_ref[...])
    g = gu[:, 0:f]
    u = gu[:, f:2 * f]
    idx = _mod_pow2(lax.broadcasted_iota(jnp.int32, (TM, 1), 0), seq_len)
    if per_seq_state:
        @pl.when(pl.program_id(1) == 0)
        def _():
            carry_sc[0:CONV_W - 1, :] = st_ref[0]
        st0 = carry_sc[0:1, :]
        st1 = carry_sc[1:2, :]
    else:
        st0 = st0_ref[...]
        st1 = st1_ref[...]
    g1 = pltpu.roll(g, 1, 0)
    g2 = pltpu.roll(g, 2, 0)
    prev1 = jnp.where(idx == 0, st1, g1)
    prev2 = jnp.where(idx == 0, st0, jnp.where(idx == 1, st1, g2))
    gc = cb_ref[...] + prev2 * cw_ref[0:1, :] + prev1 * cw_ref[1:2, :] + g * cw_ref[2:3, :]
    act = (jax.nn.silu(gc) * u).astype(BF16)
    y_ref[...] = h + _dot(act, wd_ref[...])
    if per_seq_state:
        carry_sc[0:CONV_W - 1, :] = g[TM - (CONV_W - 1):TM, :]
        cs_ref[0] = g[TM - (CONV_W - 1):TM, :]
    else:
        g_ref[...] = g


def _ffn_prompt(h2d, n, conv_state, gf, win, cw, cb, wd):
    t, d = h2d.shape
    f = cw.shape[1]
    nt = (t // n) // TM
    row = pl.BlockSpec((TM, d), lambda bi, i: (bi * nt + i, 0))
    st = pl.BlockSpec((1, CONV_W - 1, f), lambda bi, i: (bi, 0, 0))
    cst = lambda a: pl.BlockSpec(a.shape, lambda bi, i, nd=a.ndim: (0,) * nd)
    return pl.pallas_call(
        functools.partial(_ffn_kernel, seq_len=TM, per_seq_state=True), grid=(n, nt),
        in_specs=[row, st, cst(gf), cst(win), cst(cw), cst(cb), cst(wd)],
        out_specs=(row, st),
        out_shape=(jax.ShapeDtypeStruct((t, d), F32), jax.ShapeDtypeStruct((n, CONV_W - 1, f), F32)),
        scratch_shapes=[pltpu.VMEM((8, f), F32)],
        compiler_params=_cparams(("arbitrary", "arbitrary")), name="ffn_prompt",
    )(h2d, conv_state, gf, win, cw, cb, wd)


def _ffn_sample(h2d, seq_len, st0, st1, gf, win, cw, cb, wd):
    t, d = h2d.shape
    f = cw.shape[1]
    row = lambda w: pl.BlockSpec((TM, w), lambda i: (i, 0))
    return pl.pallas_call(
        functools.partial(_ffn_kernel, seq_len=seq_len, per_seq_state=False), grid=(t // TM,),
        in_specs=[row(d), row(f), row(f), _const_spec(gf.shape), _const_spec(win.shape), _const_spec(cw.shape),
                  _const_spec(cb.shape), _const_spec(wd.shape)],
        out_specs=(row(d), row(f)),
        out_shape=(jax.ShapeDtypeStruct((t, d), F32), jax.ShapeDtypeStruct((t, f), F32)),
        compiler_params=_cparams(("arbitrary",)), name="ffn_sample",
    )(h2d, st0, st1, gf, win, cw, cb, wd)


def _block_diag_ones(n, blk):
    i = np.arange(n)
    return jnp.asarray((i[:, None] // blk == i[None, :] // blk).astype(np.float32), BF16)


def _overlap_matrix(ncp, nc, ns, nsp):
    c0 = np.arange(ncp) * CMP_STRIDE
    j0 = np.arange(nsp) * SEL_BLOCK
    m = (c0[:, None] < j0[None, :] + SEL_BLOCK) & (c0[:, None] + CMP_LEN > j0[None, :])
    m &= (np.arange(ncp)[:, None] < nc) & (np.arange(nsp)[None, :] < ns)
    return jnp.asarray(m.astype(np.float32), BF16)


def _dup_groups(a):
    parts = []
    for g in range(NSA_GROUPS):
        seg = a[..., g * HEAD_DIM:(g + 1) * HEAD_DIM]
        parts += [seg, seg]
    return jnp.concatenate(parts, axis=-1)


def _layer_weights(l, attn_norm_g, w_in, qk_norm_g, nsa_cmp_pe, nsa_cmp_w1, nsa_cmp_b1, nsa_cmp_w2, nsa_cmp_b2,
                   w_branch_moba, w_branch_nsa, w_out, ffn_norm_g, w_ffn_in, ffn_conv_w, ffn_conv_b, w_ffn_down):
    d = w_in.shape[1]
    wi = w_in[l]
    mw, nw, kvw = MOBA_HEADS * HEAD_DIM, NSA_HEADS * HEAD_DIM, NSA_GROUPS * HEAD_DIM
    o_nq = 3 * mw
    o_c = o_nq + nw
    o_s = o_c + 2 * kvw
    o_w = o_s + 2 * kvw
    o_ng = o_w + 2 * kvw
    o_mg = o_ng + 3 * NSA_HEADS
    w1 = jnp.concatenate([wi[:, :o_s], wi[:, o_s:o_s + kvw], wi[:, o_w:o_w + kvw],
                          wi[:, o_s + kvw:o_w], wi[:, o_w + kvw:o_ng]], axis=1).astype(BF16)
    wng = jnp.pad(wi[:, o_ng:o_mg], ((0, 0), (0, LANES - 3 * NSA_HEADS))).astype(BF16)
    wmg = wi[:, o_mg:o_mg + 2 * d].astype(BF16)
    g = qk_norm_g[l]
    qkg = jnp.stack([jnp.tile(g[0], MOBA_HEADS), jnp.tile(g[1], MOBA_HEADS), jnp.tile(g[2], NSA_HEADS),
                     jnp.concatenate([jnp.tile(g[4], NSA_GROUPS), jnp.tile(g[5], NSA_GROUPS),
                                      jnp.zeros((nw - 2 * kvw,), F32)])])
    r = CMP_LEN // CMP_STRIDE
    w1c = nsa_cmp_w1[l]
    w1r = w1c.reshape(2, r, CMP_STRIDE, HEAD_DIM, CMP_HIDDEN)
    nkvg = 2 * NSA_GROUPS
    wfull = jnp.zeros((CMP_STRIDE, nkvg, HEAD_DIM, nkvg, r, CMP_HIDDEN), F32)
    for kvg in range(nkvg):
        wfull = wfull.at[:, kvg, :, kvg].set(jnp.transpose(w1r[kvg // NSA_GROUPS], (1, 2, 0, 3)))
    wfull = wfull.reshape(CMP_STRIDE * nkvg * HEAD_DIM, nkvg * r * CMP_HIDDEN).astype(BF16)
    w2bd = jnp.zeros((nkvg, CMP_HIDDEN, nkvg, 2, HEAD_DIM), F32)
    for kvg in range(nkvg):
        w2bd = w2bd.at[kvg, :, kvg].set(jnp.broadcast_to(nsa_cmp_w2[l][kvg // NSA_GROUPS][:, None, :],
                                                         (CMP_HIDDEN, 2, HEAD_DIM)))
    w2bd = w2bd.reshape(nkvg * CMP_HIDDEN, nkvg * 2 * HEAD_DIM).astype(BF16)
    b2 = nsa_cmp_b2[l]
    cw = dict(
        wfull=wfull,
        pe=jnp.broadcast_to(nsa_cmp_pe[l].reshape(2, 1, CMP_LEN * HEAD_DIM), (2, 8, CMP_LEN * HEAD_DIM)).astype(BF16),
        w1=w1c.astype(BF16), b1=nsa_cmp_b1[l], w2bd=w2bd,
        b2row=jnp.concatenate([jnp.tile(b2[0], 2 * NSA_GROUPS), jnp.tile(b2[1], 2 * NSA_GROUPS)])[None, :],
        gk=jnp.tile(g[3], 2 * NSA_GROUPS)[None, :], bd=_block_diag_ones(256, HEAD_DIM))
    return dict(
        g_attn=attn_norm_g[l][None, :], w1=w1, wng=wng, wmg=wmg, qkg=qkg, bd=_block_diag_ones(256, HEAD_DIM), cw=cw,
        wa=w_branch_moba[l].astype(BF16), wb=w_branch_nsa[l].astype(BF16), wo=w_out[l].astype(BF16),
        gf=ffn_norm_g[l][None, :], win=w_ffn_in[l].astype(BF16), cwt=ffn_conv_w[l], cb=ffn_conv_b[l][None, :],
        wd=w_ffn_down[l].astype(BF16))


def _bf16_kv(kv2d, n, length):
    k = _dup_groups(kv2d[:, 0:NSA_GROUPS * HEAD_DIM]).astype(BF16).reshape(n, length, 2 * LANES)
    v = _dup_groups(kv2d[:, NSA_GROUPS * HEAD_DIM:]).astype(BF16).reshape(n, length, 2 * LANES)
    return k, v


def _prompt_layer(x, wts):
    b, s, d = x.shape
    assert s % (CMP_STRIDE * LANES) == 0 and s % TQ == 0
    t = b * s
    x2d = x.reshape(t, d)
    mq, mkv, nq, ckv, skv, wkv, ng, mg, ksum = _inproj(x2d, wts["g_attn"], wts["w1"], wts["wng"], wts["wmg"],
                                                      wts["qkg"], wts["bd"])
    nb = s // MOBA_BLOCK
    mw = MOBA_HEADS * HEAD_DIM
    means = jnp.pad(ksum.reshape(b, nb, mw) * (1.0 / MOBA_BLOCK), ((0, 0), (0, LANES - nb), (0, 0)))
    mk16 = mkv[:, 0:mw].astype(BF16).reshape(b, s, mw)
    mv16 = mkv[:, mw:2 * mw].astype(BF16).reshape(b, s, mw)
    o_m = _moba_prompt(mq.reshape(b, s, mw), mk16, mv16, means)

    ncp = s // CMP_STRIDE
    x_cmp = ckv.reshape(b, ncp, CMP_STRIDE * ckv.shape[1])
    ck, cv = _compress_call((x_cmp,), (pl.BlockSpec((1, ncp, x_cmp.shape[2]), lambda bi: (bi, 0, 0)),),
                            b, ncp, (b,), wts["cw"])
    sk, sv = _bf16_kv(skv, b, s)
    wk, wv = _bf16_kv(wkv, b, s)
    ns = s // SEL_BLOCK
    ovl = _overlap_matrix(ncp, ncp - CMP_LEN // CMP_STRIDE + 1, ns, LANES)
    kk = np.arange(TQ)[None, None, :] // SEL_BLOCK + (TQ // SEL_BLOCK) * np.arange(s // TQ)[:, None, None]
    expand = jnp.asarray((np.arange(LANES)[None, :, None] == kk).astype(np.float32), BF16)
    o_n = _nsa_prompt(nq.reshape(b, s, -1), ck, cv, sk, sv, wk, wv, ng.reshape(b, s, LANES), ovl, expand)

    h2d = _merge(x2d, o_m.reshape(t, -1), o_n.reshape(t, -1), mg, wts["wa"], wts["wb"], wts["wo"])
    f = wts["cwt"].shape[1]
    y2d, conv = _ffn_prompt(h2d, b, jnp.zeros((b, CONV_W - 1, f), F32), wts["gf"], wts["win"], wts["cwt"],
                            wts["cb"], wts["wd"])
    wrows = min(WINDOW, s)
    outs = (mkv.reshape(b, s, 2, MOBA_HEADS, HEAD_DIM), ckv.reshape(b, s, 2, NSA_GROUPS, HEAD_DIM),
            skv.reshape(b, s, 2, NSA_GROUPS, HEAD_DIM),
            wkv.reshape(b, s, 2, NSA_GROUPS, HEAD_DIM)[:, s - wrows:], conv)
    return y2d.reshape(b, s, d), outs


def _sample_layer(x, wts, pool_moba, pool_cmp, pool_sel, win_state, conv_state, page_table):
    db, ds, d = x.shape
    t = db * ds
    assert t % TM == 0 and ds == 8
    n_pages = page_table.shape[1]
    past = n_pages * PAGE
    x2d = x.reshape(t, d)
    mq, mkv, nq, ckv, skv, wkv, ng, mg, _ = _inproj(x2d, wts["g_attn"], wts["w1"], wts["wng"], wts["wmg"],
                                                   wts["qkg"], wts["bd"])
    mw = MOBA_HEADS * HEAD_DIM
    n_pool = pool_moba.shape[0]
    o_m = _moba_sample(page_table, mq.reshape(db, ds, mw), mkv[:, 0:mw].reshape(db, ds, mw),
                       mkv[:, mw:2 * mw].reshape(db, ds, mw), pool_moba.reshape(n_pool, PAGE, 2 * mw))

    assert (past + ds) // CMP_STRIDE == past // CMP_STRIDE
    rows_pp = PAGE // CMP_STRIDE
    ncp = n_pages * rows_pp
    pool_chunks = pool_cmp.reshape(n_pool, rows_pp, CMP_STRIDE * 2 * NSA_GROUPS * HEAD_DIM)
    x_specs = [pl.BlockSpec((1, rows_pp, pool_chunks.shape[2]), lambda bi, pt, j=j: (pt[bi, j], 0, 0))
               for j in range(n_pages)]
    ck, cv = _compress_call([pool_chunks] * n_pages, x_specs, db, ncp, (db,), wts["cw"], 1, (page_table,))

    nqf = nq.astype(F32).reshape(db, ds, NSA_HEADS, HEAD_DIM).transpose(0, 2, 1, 3)
    zero = jnp.zeros_like(nqf)
    hh = jnp.arange(NSA_HEADS)[None, :, None, None]
    q_pair = jnp.where(hh % 2 == 0, jnp.concatenate([nqf, zero], -1), jnp.concatenate([zero, nqf], -1))
    q_grp = jnp.where(hh // NSA_HPG == 0, jnp.concatenate([nqf, zero], -1), jnp.concatenate([zero, nqf], -1))
    q_pair = q_pair.reshape(db, NSA_HEADS * ds, LANES).astype(BF16)
    q_grp = q_grp.reshape(db, NSA_HEADS * ds, LANES).astype(BF16)
    ns = -(-(past + ds) // SEL_BLOCK)
    nsp = -(-ns // LANES) * LANES
    ovl = _overlap_matrix(ncp, ncp - CMP_LEN // CMP_STRIDE + 1, ns, nsp)
    expand = jnp.asarray((np.arange(nsp)[:, None] == np.arange(past)[None, :] // SEL_BLOCK).astype(np.float32), BF16)
    wst = win_state.reshape(db, win_state.shape[1], -1)
    o_n = _nsa_sample(page_table, q_pair, q_grp, ck, cv, skv.reshape(db, ds, -1), wkv.reshape(db, ds, -1), wst,
                      ng.reshape(db, ds, LANES), ovl, expand, pool_sel.reshape(n_pool, PAGE, -1))

    h2d = _merge(x2d, o_m.reshape(t, -1), o_n.reshape(t, -1), mg, wts["wa"], wts["wb"], wts["wo"])
    st0 = jnp.repeat(conv_state[:, 0], ds, axis=0)
    st1 = jnp.repeat(conv_state[:, 1], ds, axis=0)
    y2d, g2d = _ffn_sample(h2d, ds, st0, st1, wts["gf"], wts["win"], wts["cwt"], wts["cb"], wts["wd"])
    gp = jnp.concatenate([conv_state, g2d.reshape(db, ds, -1)], axis=1)
    wcat = jnp.concatenate([win_state, wkv.reshape(db, ds, 2, NSA_GROUPS, HEAD_DIM)], axis=1)
    wrows = min(WINDOW, wcat.shape[1])
    outs = (mkv.reshape(db, ds, 2, MOBA_HEADS, HEAD_DIM), ckv.reshape(db, ds, 2, NSA_GROUPS, HEAD_DIM),
            skv.reshape(db, ds, 2, NSA_GROUPS, HEAD_DIM), wcat[:, wcat.shape[1] - wrows:], gp[:, ds:])
    return y2d.reshape(db, ds, d), outs


def kernel(x_prompt, x_sample, cache_moba_kv, cache_nsa_cmp_kv, cache_nsa_sel_kv, state_nsa_win_kv, state_ffn_conv,
           page_table, attn_norm_g, w_in, qk_norm_g, nsa_cmp_pe, nsa_cmp_w1, nsa_cmp_b1, nsa_cmp_w2, nsa_cmp_b2,
           w_branch_moba, w_branch_nsa, w_out, ffn_norm_g, w_ffn_in, ffn_conv_w, ffn_conv_b, w_ffn_down):
    depth = w_in.shape[0]
    hp, hs = x_prompt, x_sample
    new = [[] for _ in range(10)]
    for l in range(depth):
        wts = _layer_weights(l, attn_norm_g, w_in, qk_norm_g, nsa_cmp_pe, nsa_cmp_w1, nsa_cmp_b1, nsa_cmp_w2,
                             nsa_cmp_b2, w_branch_moba, w_branch_nsa, w_out, ffn_norm_g, w_ffn_in, ffn_conv_w,
                             ffn_conv_b, w_ffn_down)
        hp, outs_p = _prompt_layer(hp, wts)
        n_pool = cache_moba_kv.shape[1]
        pools = [c.reshape((depth * n_pool,) + c.shape[2:])
                 for c in (cache_moba_kv, cache_nsa_cmp_kv, cache_nsa_sel_kv)]
        hs, outs_s = _sample_layer(hs, wts, *pools, state_nsa_win_kv[l], state_ffn_conv[l],
                                   page_table + l * n_pool)
        for lst, arr in zip(new[:5], outs_p):
            lst.append(arr)
        for lst, arr in zip(new[5:], outs_s):
            lst.append(arr)
    st = [jnp.stack(v) for v in new]
    return (hp, hs, st[0], st[5], st[1], st[6], st[2], st[7], st[3], st[8], st[4], st[9])
```

```python
import functools

import numpy as np
import jax
import jax.numpy as jnp
from jax import lax
from jax.experimental import pallas as pl
from jax.experimental.pallas import tpu as pltpu

F32 = jnp.float32
BF16 = jnp.bfloat16

HEAD_DIM = 64
MOBA_HEADS = 8
MOBA_BLOCK = 256
MOBA_TOPK = 3
NSA_HEADS = 8
NSA_GROUPS = 2
NSA_HPG = NSA_HEADS // NSA_GROUPS
CMP_LEN = 32
CMP_STRIDE = 16
CMP_HIDDEN = 128
SEL_BLOCK = 64
SEL_TOPN = 8
WINDOW = 512
CONV_W = 3
PAGE = 128
RMS_EPS = 1e-6
NEG = -1e30
BIG = 1e30
QK_SCALE = HEAD_DIM ** -0.5

LANES = 128
TQ = 256
TM = 256
PAGES_PER_STEP = 8
VMEM_LIMIT = 56 * 1024 * 1024

MOBA_SLOPES = tuple(float(2.0 ** (-8.0 * (h + 1) / MOBA_HEADS)) for h in range(MOBA_HEADS))
NSA_SLOPES = tuple(float(2.0 ** (-8.0 * (h + 1) / NSA_HEADS)) for h in range(NSA_HEADS))


def _dot(a, b):
    return jnp.dot(a, b, preferred_element_type=F32)


def _dot_nt(a, b):
    return lax.dot_general(a, b, (((1,), (1,)), ((), ())), preferred_element_type=F32)


def _div_pow2(x, n):
    assert n & (n - 1) == 0
    return lax.shift_right_logical(x, jnp.int32(n.bit_length() - 1))


def _mod_pow2(x, n):
    assert n & (n - 1) == 0
    return x & (n - 1)


def _split_bf16(x):
    hi = x.astype(BF16)
    lo = (x - hi.astype(F32)).astype(BF16)
    return hi, lo


def _cparams(sem):
    return pltpu.CompilerParams(dimension_semantics=sem, vmem_limit_bytes=VMEM_LIMIT)


def _const_spec(shape):
    nd = len(shape)
    return pl.BlockSpec(shape, lambda *_: (0,) * nd)


def _seg_rms(y, bd, gain):
    wb = bd.shape[0]
    outs = []
    for c in range(y.shape[1] // wb):
        s = y[:, c * wb:(c + 1) * wb]
        ss = _dot((s * s).astype(BF16), bd)
        outs.append(s * lax.rsqrt(ss * (1.0 / HEAD_DIM) + RMS_EPS))
    r = outs[0] if len(outs) == 1 else jnp.concatenate(outs, axis=1)
    return r * gain


def _top_rank(val, n, lane):
    rank = jnp.zeros(val.shape, F32)
    for j2 in range(n):
        c = val[:, j2:j2 + 1]
        before = (c > val) | ((c == val) & (lane > j2))
        rank = rank + jnp.where(before, 1.0, 0.0)
    return rank


def _top_rank_rows(val, n, rowid):
    rank = jnp.zeros(val.shape, F32)
    for j2 in range(n):
        c = val[j2:j2 + 1, :]
        before = (c > val) | ((c == val) & (rowid > j2))
        rank = rank + jnp.where(before, 1.0, 0.0)
    return rank


def _inproj_kernel(x_ref, g_ref, w1_ref, wng_ref, wmg_ref, qkg_ref, bd_ref,
                   mq_ref, mkv_ref, nq_ref, ckv_ref, skv_ref, wkv_ref, ng_ref, mg_ref, ksum_ref):
    x = x_ref[...]
    xn = (x * lax.rsqrt(jnp.mean(x * x, axis=-1, keepdims=True) + RMS_EPS) * g_ref[...]).astype(BF16)
    y = _dot(xn, w1_ref[...])
    bd = bd_ref[...]
    mq_ref[...] = (_seg_rms(y[:, 0:512], bd, qkg_ref[0:1, :]) * QK_SCALE).astype(BF16)
    mk = _seg_rms(y[:, 512:1024], bd, qkg_ref[1:2, :])
    mkv_ref[:, 0:512] = mk
    mkv_ref[:, 512:1024] = y[:, 1024:1536]
    ksum_ref[0] = jnp.sum(mk, axis=0, keepdims=True)
    nq_ref[...] = (_seg_rms(y[:, 1536:2048], bd, qkg_ref[2:3, :]) * QK_SCALE).astype(BF16)
    ckv_ref[...] = y[:, 2048:2304]
    skwk = _seg_rms(y[:, 2304:2560], bd, qkg_ref[3:4, 0:256])
    skv_ref[:, 0:128] = skwk[:, 0:128]
    skv_ref[:, 128:256] = y[:, 2560:2688]
    wkv_ref[:, 0:128] = skwk[:, 128:256]
    wkv_ref[:, 128:256] = y[:, 2688:2816]
    ng_ref[...] = jax.nn.sigmoid(_dot(xn, wng_ref[...]))
    mg_ref[...] = jax.nn.sigmoid(_dot(xn, wmg_ref[...]))


def _inproj(x2d, g_attn, w1, wng, wmg, qkg, bd):
    t, d = x2d.shape
    nt = t // TM
    row = lambda w: pl.BlockSpec((TM, w), lambda i: (i, 0))
    out_shape = (
        jax.ShapeDtypeStruct((t, 512), BF16), jax.ShapeDtypeStruct((t, 1024), F32),
        jax.ShapeDtypeStruct((t, 512), BF16), jax.ShapeDtypeStruct((t, 256), F32),
        jax.ShapeDtypeStruct((t, 256), F32), jax.ShapeDtypeStruct((t, 256), F32),
        jax.ShapeDtypeStruct((t, 128), F32), jax.ShapeDtypeStruct((t, 2 * d), F32),
        jax.ShapeDtypeStruct((nt, 1, 512), F32))
    return pl.pallas_call(
        _inproj_kernel, grid=(nt,),
        in_specs=[row(d), _const_spec(g_attn.shape), _const_spec(w1.shape), _const_spec(wng.shape),
                  _const_spec(wmg.shape), _const_spec(qkg.shape), _const_spec(bd.shape)],
        out_specs=(row(512), row(1024), row(512), row(256), row(256), row(256), row(128), row(2 * d),
                   pl.BlockSpec((1, 1, 512), lambda i: (i, 0, 0))),
        out_shape=out_shape, compiler_params=_cparams(("arbitrary",)), name="inproj",
    )(x2d, g_attn, w1, wng, wmg, qkg, bd)


def _alibi_tables(slopes):
    rel = (np.arange(TQ)[:, None] - np.arange(TQ)[None, :]).astype(np.float32)
    bias = np.asarray(slopes, np.float32)[:, None, None] * rel[None]
    diag = np.where(rel[None] <= 0, bias, np.float32(NEG))
    edge = np.where(rel[None] >= 0, bias, np.float32(NEG))
    return jnp.asarray(bias), jnp.asarray(diag), jnp.asarray(edge)


def _flash_cols(s, c, sel, h, m_sc, l_sc):
    m_old = m_sc[h:h + 1, :]
    bm = jnp.max(s, axis=0, keepdims=True) + c
    if sel is None:
        m_new = jnp.maximum(m_old, bm)
        shift = m_new - c
    else:
        m_new = jnp.maximum(m_old, jnp.where(sel, bm, NEG))
        shift = jnp.where(sel, m_new - c, BIG)
    alpha = jnp.exp(m_old - m_new)
    p = jnp.exp(s - shift)
    l_sc[h:h + 1, :] = alpha * l_sc[h:h + 1, :] + jnp.sum(p, axis=0, keepdims=True)
    m_sc[h:h + 1, :] = m_new
    return p, alpha


def _moba_prompt_kernel(qt_ref, k_ref, vt_ref, mean_ref, bias_ref, diag_ref, o_ref,
                        qz_sc, sel_sc, m_sc, l_sc, acc_sc, *, nb):
    i = pl.program_id(1)
    npair = MOBA_HEADS // 2
    zeros = jnp.zeros((HEAD_DIM, TQ), BF16)
    for pair in range(npair):
        r0 = 2 * pair * HEAD_DIM
        qz_sc[pair, :, 0:TQ] = jnp.concatenate([qt_ref[0, r0:r0 + HEAD_DIM, :], zeros], axis=0)
        qz_sc[pair, :, TQ:2 * TQ] = jnp.concatenate([zeros, qt_ref[0, r0 + HEAD_DIM:r0 + 2 * HEAD_DIM, :]], axis=0)
    m_sc[...] = jnp.full(m_sc.shape, NEG, F32)
    l_sc[...] = jnp.zeros(l_sc.shape, F32)
    acc_sc[...] = jnp.zeros(acc_sc.shape, F32)

    nbp = mean_ref.shape[1]
    rowb = lax.broadcasted_iota(jnp.int32, (nbp, TQ), 0)
    cand = rowb < i
    for h in range(MOBA_HEADS):
        pair, half = divmod(h, 2)
        qz = qz_sc[pair, :, half * TQ:(half + 1) * TQ]
        mh, ml = _split_bf16(mean_ref[0, :, pair * LANES:(pair + 1) * LANES])
        ss = _dot(mh, qz) + _dot(ml, qz)
        rank = _top_rank_rows(jnp.where(cand, ss, -jnp.inf), nb, rowb)
        sel_sc[h] = jnp.where(cand & (rank < MOBA_TOPK), 1.0, 0.0)

    def kv_block(kb, own):
        start = pl.multiple_of(kb * TQ, TQ)
        k_blk = k_ref[0, pl.ds(start, TQ), :]
        vt_blk = vt_ref[0, :, pl.ds(start, TQ)]
        off = ((kb - i) * TQ).astype(F32)
        for pair in range(npair):
            s2 = _dot(k_blk[:, pair * LANES:(pair + 1) * LANES], qz_sc[pair])
            for half in range(2):
                h = 2 * pair + half
                rows = slice(h * HEAD_DIM, (h + 1) * HEAD_DIM)
                if own:
                    p, alpha = _flash_cols(s2[:, half * TQ:(half + 1) * TQ] + diag_ref[h], 0.0, None, h, m_sc, l_sc)
                else:
                    sel = sel_sc[h, pl.ds(kb, 1), :] > 0.5
                    p, alpha = _flash_cols(s2[:, half * TQ:(half + 1) * TQ] + bias_ref[h], MOBA_SLOPES[h] * off,
                                           sel, h, m_sc, l_sc)
                acc_sc[rows, :] = alpha * acc_sc[rows, :] + _dot(vt_blk[rows, :], p.astype(BF16))

    def body(kb, carry):
        kv_block(kb, False)
        return carry

    lax.fori_loop(0, i, body, 0)
    kv_block(i, True)

    for h in range(MOBA_HEADS):
        rows = slice(h * HEAD_DIM, (h + 1) * HEAD_DIM)
        acc_sc[rows, :] = acc_sc[rows, :] / l_sc[h:h + 1, :]
    o_ref[0] = jnp.transpose(acc_sc[...]).astype(BF16)


def _moba_prompt(qt, k, vt, means, bias, diag):
    b, w, s = qt.shape
    nb = s // TQ
    return pl.pallas_call(
        functools.partial(_moba_prompt_kernel, nb=nb), grid=(b, nb),
        in_specs=[pl.BlockSpec((1, w, TQ), lambda bi, i: (bi, 0, i)),
                  pl.BlockSpec((1, s, w), lambda bi, i: (bi, 0, 0)),
                  pl.BlockSpec((1, w, s), lambda bi, i: (bi, 0, 0)),
                  pl.BlockSpec((1,) + means.shape[1:], lambda bi, i: (bi, 0, 0)),
                  _const_spec(bias.shape), _const_spec(diag.shape)],
        out_specs=pl.BlockSpec((1, TQ, w), lambda bi, i: (bi, i, 0)),
        out_shape=jax.ShapeDtypeStruct((b, s, w), BF16),
        scratch_shapes=[pltpu.VMEM((MOBA_HEADS // 2, LANES, 2 * TQ), BF16),
                        pltpu.VMEM((MOBA_HEADS, means.shape[1], TQ), F32),
                        pltpu.VMEM((MOBA_HEADS, TQ), F32), pltpu.VMEM((MOBA_HEADS, TQ), F32),
                        pltpu.VMEM((w, TQ), F32)],
        compiler_params=_cparams(("arbitrary", "arbitrary")), name="moba_prompt",
    )(qt, k, vt, means, bias, diag)


def _compress_core(load_rows, ncp, wfull_ref, pe_ref, w1_ref, b1_ref, w2_ref, b2_ref, gk_ref, bd_ref, ck_ref, cv_ref):
    kw = NSA_GROUPS * HEAD_DIM
    hw = wfull_ref.shape[1] // 2
    projs = []
    for kv in range(2):
        acc = jnp.zeros((ncp, hw), F32)
        for t in range(CMP_STRIDE):
            r0 = t * 2 * kw + kv * kw
            acc = acc + _dot(load_rows(kv, t).astype(BF16), wfull_ref[r0:r0 + kw, kv * hw:(kv + 1) * hw])
        projs.append(acc)
    proj = jnp.concatenate(projs, axis=1)
    hids = []
    for kvg in range(2 * NSA_GROUPS):
        kv = kvg // NSA_GROUPS
        pe_term = _dot(pe_ref[kv], w1_ref[kv])[0:1, :]
        c0 = kvg * 2 * CMP_HIDDEN
        p0 = proj[:, c0:c0 + CMP_HIDDEN]
        p1 = pltpu.roll(proj[:, c0 + CMP_HIDDEN:c0 + 2 * CMP_HIDDEN], ncp - 1, 0)
        hids.append(jax.nn.gelu(b1_ref[kv:kv + 1, :] + p0 + p1 + pe_term))
    hid = jnp.concatenate(hids, axis=1).astype(BF16)
    out = _dot(hid, w2_ref[...]) + b2_ref[...]
    ck_ref[0] = _seg_rms(out[:, 0:kw], bd_ref[...], gk_ref[...]).astype(BF16)
    cv_ref[0] = out[:, kw:2 * kw].astype(BF16)


def _compress_prompt_kernel(xk_ref, xv_ref, *refs):
    ncp = xk_ref.shape[1] // CMP_STRIDE
    x_refs = (xk_ref, xv_ref)
    _compress_core(lambda kv, t: x_refs[kv][0, pl.ds(t, ncp, stride=CMP_STRIDE), :], ncp, *refs)


def _compress_sample_kernel(pt_ref, *refs, n_pages):
    pg_refs = refs[:n_pages]
    x_scs = refs[-2:]
    kw = NSA_GROUPS * HEAD_DIM
    for j in range(n_pages):
        for kv in range(2):
            x_scs[kv][j * PAGE:(j + 1) * PAGE, :] = jnp.transpose(pg_refs[j][0, kv * kw:(kv + 1) * kw, :])
    ncp = n_pages * PAGE // CMP_STRIDE
    _compress_core(lambda kv, t: x_scs[kv][pl.ds(t, ncp, stride=CMP_STRIDE), :], ncp, *refs[n_pages:-2])


def _compress_consts(cw):
    return (cw["wfull"], cw["pe"], cw["w1"], cw["b1"], cw["w2bd"], cw["b2row"], cw["gk"], cw["bd"])


def _compress_prompt(xk, xv, cw):
    b, s, kw = xk.shape
    ncp = s // CMP_STRIDE
    consts = _compress_consts(cw)
    xspec = pl.BlockSpec((1, s, kw), lambda bi: (bi, 0, 0))
    ospec = pl.BlockSpec((1, ncp, kw), lambda bi: (bi, 0, 0))
    return pl.pallas_call(
        _compress_prompt_kernel, grid=(b,),
        in_specs=[xspec, xspec] + [_const_spec(a.shape) for a in consts],
        out_specs=(ospec, ospec),
        out_shape=(jax.ShapeDtypeStruct((b, ncp, kw), BF16), jax.ShapeDtypeStruct((b, ncp, kw), BF16)),
        compiler_params=_cparams(("arbitrary",)), name="nsa_compress_prompt",
    )(xk, xv, *consts)


def _compress_sample(page_table, pool_t, cw):
    db, n_pages = page_table.shape
    fw = pool_t.shape[1]
    ncp = n_pages * PAGE // CMP_STRIDE
    consts = _compress_consts(cw)
    kw = NSA_GROUPS * HEAD_DIM
    cst = lambda a: pl.BlockSpec(a.shape, lambda bi, pt, nd=a.ndim: (0,) * nd)
    ospec = pl.BlockSpec((1, ncp, kw), lambda bi, pt: (bi, 0, 0))
    pg_specs = [pl.BlockSpec((1, fw, PAGE), lambda bi, pt, j=j: (pt[bi, j], 0, 0)) for j in range(n_pages)]
    gs = pltpu.PrefetchScalarGridSpec(
        num_scalar_prefetch=1, grid=(db,), in_specs=pg_specs + [cst(a) for a in consts], out_specs=(ospec, ospec),
        scratch_shapes=[pltpu.VMEM((n_pages * PAGE, kw), F32), pltpu.VMEM((n_pages * PAGE, kw), F32)])
    return pl.pallas_call(
        functools.partial(_compress_sample_kernel, n_pages=n_pages), grid_spec=gs,
        out_shape=(jax.ShapeDtypeStruct((db, ncp, kw), BF16), jax.ShapeDtypeStruct((db, ncp, kw), BF16)),
        compiler_params=_cparams(("arbitrary",)), name="nsa_compress_sample",
    )(page_table, *([pool_t] * n_pages), *consts)


def _nsa_prompt_kernel(qt_ref, ck_ref, cvt_ref, sk_ref, svt_ref, wk_ref, wvt_ref, ng_ref, ovl_ref, cb_ref,
                       bias_ref, diag_ref, edge_ref, o_ref,
                       qz_sc, p_sc, sel_sc, oc_sc, ms_sc, ls_sc, as_sc, mw_sc, lw_sc, aw_sc, *, ns):
    i = pl.program_id(1)
    ncp = ck_ref.shape[1]
    zeros = jnp.zeros((HEAD_DIM, TQ), BF16)
    for h in range(NSA_HEADS):
        g, p_ = divmod(h, NSA_HPG)
        qh = qt_ref[0, h * HEAD_DIM:(h + 1) * HEAD_DIM, :]
        qz_sc[g, :, p_ * TQ:(p_ + 1) * TQ] = jnp.concatenate([qh, zeros] if g == 0 else [zeros, qh], axis=0)
    for m_ref, l_ref, a_ref in ((ms_sc, ls_sc, as_sc), (mw_sc, lw_sc, aw_sc)):
        m_ref[...] = jnp.full(m_ref.shape, NEG, F32)
        l_ref[...] = jnp.zeros(l_ref.shape, F32)
        a_ref[...] = jnp.zeros(a_ref.shape, F32)

    qpos = i * TQ + lax.broadcasted_iota(jnp.int32, (1, TQ), 1)
    cend = lax.broadcasted_iota(jnp.int32, (ncp, 1), 0) * CMP_STRIDE + (CMP_LEN - 1)
    cmask = cend <= qpos
    nsr = sel_sc.shape[1]
    rowj = lax.broadcasted_iota(jnp.int32, (nsr, TQ), 0)
    cur = _div_pow2(qpos, SEL_BLOCK)
    forced = (rowj == 0) | (rowj == cur) | (rowj == cur - 1)
    cand = rowj <= cur
    ioff = (i * TQ).astype(F32)
    for g in range(NSA_GROUPS):
        grow = slice(g * HEAD_DIM, (g + 1) * HEAD_DIM)
        s4 = _dot(ck_ref[0], qz_sc[g])
        psum = jnp.zeros((ncp, TQ), F32)
        for p_ in range(NSA_HPG):
            h = g * NSA_HPG + p_
            s = jnp.where(cmask, s4[:, p_ * TQ:(p_ + 1) * TQ] + cb_ref[h] - NSA_SLOPES[h] * ioff, NEG)
            e = jnp.where(cmask, jnp.exp(s - jnp.max(s, axis=0, keepdims=True)), 0.0)
            den = jnp.sum(e, axis=0, keepdims=True)
            pn = e / jnp.where(den > 0.0, den, 1.0)
            psum = psum + pn
            p_sc[0:ncp, p_ * TQ:(p_ + 1) * TQ] = pn.astype(BF16)
        oc4 = _dot(cvt_ref[0], p_sc[0:ncp, :])
        for p_ in range(NSA_HPG):
            h = g * NSA_HPG + p_
            oc_sc[h * HEAD_DIM:(h + 1) * HEAD_DIM, :] = oc4[grow, p_ * TQ:(p_ + 1) * TQ]
        ph, plo = _split_bf16(psum)
        imp = _dot(ovl_ref[...], ph) + _dot(ovl_ref[...], plo)
        val = jnp.where(cand, jnp.where(forced, jnp.inf, imp), -jnp.inf)
        rank = _top_rank_rows(val, ns, rowj)
        sel_sc[g] = jnp.where(cand & (rank < SEL_TOPN), 1.0, 0.0)

    spb = TQ // SEL_BLOCK

    def attend(k_blk, vt_blk, tab_ref, off, kb_sel, m_ref, l_ref, a_ref):
        for g in range(NSA_GROUPS):
            grow = slice(g * HEAD_DIM, (g + 1) * HEAD_DIM)
            s4 = _dot(k_blk, qz_sc[g])
            alphas = []
            for p_ in range(NSA_HPG):
                h = g * NSA_HPG + p_
                s = s4[:, p_ * TQ:(p_ + 1) * TQ] + tab_ref[h]
                c = NSA_SLOPES[h] * off
                if kb_sel is None:
                    p, alpha = _flash_cols(s, c, None, h, m_ref, l_ref)
                    p_sc[:, p_ * TQ:(p_ + 1) * TQ] = p.astype(BF16)
                else:
                    m_old = m_ref[h:h + 1, :]
                    sels = [sel_sc[g, pl.ds(kb_sel * spb + j, 1), :] > 0.5 for j in range(spb)]
                    m_new = m_old
                    for j in range(spb):
                        bm = jnp.max(s[j * SEL_BLOCK:(j + 1) * SEL_BLOCK, :], axis=0, keepdims=True) + c
                        m_new = jnp.maximum(m_new, jnp.where(sels[j], bm, NEG))
                    alpha = jnp.exp(m_old - m_new)
                    lsum = alpha * l_ref[h:h + 1, :]
                    for j in range(spb):
                        shift = jnp.where(sels[j], m_new - c, BIG)
                        p = jnp.exp(s[j * SEL_BLOCK:(j + 1) * SEL_BLOCK, :] - shift)
                        lsum = lsum + jnp.sum(p, axis=0, keepdims=True)
                        p_sc[j * SEL_BLOCK:(j + 1) * SEL_BLOCK, p_ * TQ:(p_ + 1) * TQ] = p.astype(BF16)
                    l_ref[h:h + 1, :] = lsum
                    m_ref[h:h + 1, :] = m_new
                alphas.append(alpha)
            pv4 = _dot(vt_blk, p_sc[...])
            for p_ in range(NSA_HPG):
                h = g * NSA_HPG + p_
                rows = slice(h * HEAD_DIM, (h + 1) * HEAD_DIM)
                a_ref[rows, :] = alphas[p_] * a_ref[rows, :] + pv4[grow, p_ * TQ:(p_ + 1) * TQ]

    def sel_tile(kb, tab_ref):
        start = pl.multiple_of(kb * TQ, TQ)
        attend(sk_ref[0, pl.ds(start, TQ), :], svt_ref[0, :, pl.ds(start, TQ)], tab_ref,
               ((kb - i) * TQ).astype(F32), kb, ms_sc, ls_sc, as_sc)

    def win_tile(back, tab_ref):
        kb = jnp.maximum(i - back, 0)
        start = pl.multiple_of(kb * TQ, TQ)
        attend(wk_ref[0, pl.ds(start, TQ), :], wvt_ref[0, :, pl.ds(start, TQ)], tab_ref,
               -float(back * TQ), None, mw_sc, lw_sc, aw_sc)

    def sel_body(kb, carry):
        sel_tile(kb, bias_ref)
        return carry

    lax.fori_loop(0, i, sel_body, 0)
    sel_tile(i, diag_ref)

    @pl.when(i >= 2)
    def _():
        win_tile(2, edge_ref)

    @pl.when(i >= 1)
    def _():
        win_tile(1, bias_ref)

    win_tile(0, diag_ref)

    ng = ng_ref[0]
    for h in range(NSA_HEADS):
        rows = slice(h * HEAD_DIM, (h + 1) * HEAD_DIM)
        o_s = as_sc[rows, :] / ls_sc[h:h + 1, :]
        o_w = aw_sc[rows, :] / lw_sc[h:h + 1, :]
        oc_sc[rows, :] = (ng[h:h + 1, :] * oc_sc[rows, :] + ng[NSA_HEADS + h:NSA_HEADS + h + 1, :] * o_s
                          + ng[2 * NSA_HEADS + h:2 * NSA_HEADS + h + 1, :] * o_w)
    o_ref[0] = jnp.transpose(oc_sc[...]).astype(BF16)


def _nsa_prompt(qt, ck, cvt, sk, svt, wk, wvt, ngt, ovl, cb, bias, diag, edge):
    b, w, s = qt.shape
    assert WINDOW == 2 * TQ
    ns = s // SEL_BLOCK
    kw = NSA_GROUPS * HEAD_DIM
    full = lambda a: pl.BlockSpec((1,) + a.shape[1:], lambda bi, i: (bi, 0, 0))
    hq = NSA_HPG * TQ
    return pl.pallas_call(
        functools.partial(_nsa_prompt_kernel, ns=ns), grid=(b, s // TQ),
        in_specs=[pl.BlockSpec((1, w, TQ), lambda bi, i: (bi, 0, i)), full(ck), full(cvt), full(sk), full(svt),
                  full(wk), full(wvt), pl.BlockSpec((1, ngt.shape[1], TQ), lambda bi, i: (bi, 0, i)),
                  _const_spec(ovl.shape), _const_spec(cb.shape), _const_spec(bias.shape), _const_spec(diag.shape),
                  _const_spec(edge.shape)],
        out_specs=pl.BlockSpec((1, TQ, w), lambda bi, i: (bi, i, 0)),
        out_shape=jax.ShapeDtypeStruct((b, s, w), BF16),
        scratch_shapes=[pltpu.VMEM((NSA_GROUPS, kw, hq), BF16), pltpu.VMEM((TQ, hq), BF16),
                        pltpu.VMEM((NSA_GROUPS, ovl.shape[0], TQ), F32), pltpu.VMEM((w, TQ), F32),
                        pltpu.VMEM((NSA_HEADS, TQ), F32), pltpu.VMEM((NSA_HEADS, TQ), F32), pltpu.VMEM((w, TQ), F32),
                        pltpu.VMEM((NSA_HEADS, TQ), F32), pltpu.VMEM((NSA_HEADS, TQ), F32), pltpu.VMEM((w, TQ), F32)],
        compiler_params=_cparams(("arbitrary", "arbitrary")), name="nsa_prompt",
    )(qt, ck, cvt, sk, svt, wk, wvt, ngt, ovl, cb, bias, diag, edge)


def _moba_sample_kernel(pt_ref, q_ref, kn_ref, vn_ref, *rest, past, ds, nbs):
    pg_refs = rest[:PAGES_PER_STEP]
    o_ref, m_sc, l_sc, o_sc, ksum_sc = rest[PAGES_PER_STEP:]
    c = pl.program_id(1)
    nch = pl.num_programs(1)
    rows = MOBA_HEADS * ds
    w = MOBA_HEADS * HEAD_DIM
    ppb = MOBA_BLOCK // PAGE
    row_w = lax.broadcasted_iota(jnp.int32, (rows, w), 0)
    diag = _div_pow2(lax.broadcasted_iota(jnp.int32, (rows, w), 1), HEAD_DIM) == _div_pow2(row_w, ds)
    lane = lax.broadcasted_iota(jnp.int32, (rows, LANES), 1)
    lane_w = lax.broadcasted_iota(jnp.int32, (w, LANES), 1)
    row1 = lax.broadcasted_iota(jnp.int32, (rows, 1), 0)
    qi = _mod_pow2(row1, ds)
    slope = jnp.zeros((rows, 1), F32)
    for h in range(MOBA_HEADS):
        slope = jnp.where(_div_pow2(row1, ds) == h, MOBA_SLOPES[h], slope)
    q8 = q_ref[0].astype(F32)
    qbd = jnp.where(diag, jnp.concatenate([q8] * MOBA_HEADS, axis=0), 0.0).astype(BF16)

    @pl.when(c == 0)
    def _():
        m_sc[...] = jnp.full(m_sc.shape, NEG, F32)
        l_sc[...] = jnp.zeros(l_sc.shape, F32)
        ksum_sc[...] = jnp.zeros(ksum_sc.shape, F32)

    col = lax.broadcasted_iota(jnp.int32, (rows, MOBA_BLOCK), 1)
    for bb in range(PAGES_PER_STEP // ppb):
        blk = c * (PAGES_PER_STEP // ppb) + bb
        kt = jnp.concatenate([pg_refs[bb * ppb + j][0, 0] for j in range(ppb)], axis=1)
        vt = jnp.concatenate([pg_refs[bb * ppb + j][0, 1] for j in range(ppb)], axis=1).astype(BF16)
        ksum_sc[...] = jnp.where(lane_w == blk, jnp.sum(kt, axis=1, keepdims=True), ksum_sc[...])
        kdist = (col - qi + (blk * MOBA_BLOCK - past)).astype(F32)
        s = _dot(qbd, kt.astype(BF16)) + slope * kdist
        m_b = jnp.max(s, axis=1, keepdims=True)
        p = jnp.exp(s - m_b)
        l_b = jnp.sum(p, axis=1, keepdims=True)
        o_sc[blk] = jnp.where(diag, _dot_nt(p.astype(BF16), vt), 0.0)
        m_sc[...] = jnp.where(lane == blk, m_b, m_sc[...])
        l_sc[...] = jnp.where(lane == blk, l_b, l_sc[...])

    @pl.when(c == nch - 1)
    def _():
        mh, ml = _split_bf16(ksum_sc[...] * (1.0 / MOBA_BLOCK))
        ss = _dot(qbd, mh) + _dot(qbd, ml)
        own = past // MOBA_BLOCK
        cand = lane < own
        rank = _top_rank(jnp.where(cand, ss, -jnp.inf), nbs, lane)
        selb = cand & (rank < MOBA_TOPK)
        m_all = m_sc[...]
        m_past = jnp.max(jnp.where(selb, m_all, NEG), axis=1, keepdims=True)
        zpad = jnp.zeros((LANES - ds, w), F32)
        k_new = jnp.concatenate([kn_ref[0], zpad], axis=0).astype(BF16)
        v_new = jnp.concatenate([vn_ref[0], zpad], axis=0).astype(BF16)
        s_own = _dot_nt(qbd, k_new) + slope * (lane - qi).astype(F32)
        s_own = jnp.where(lane <= qi, s_own, NEG)
        m_tot = jnp.maximum(m_past, jnp.max(s_own, axis=1, keepdims=True))
        wgt = jnp.where(selb, jnp.exp(m_all - m_tot), 0.0)
        p_own = jnp.exp(s_own - m_tot)
        den = jnp.sum(wgt * l_sc[...], axis=1, keepdims=True) + jnp.sum(p_own, axis=1, keepdims=True)
        out = jnp.where(diag, _dot(p_own.astype(BF16), v_new), 0.0)
        for b2 in range(nbs):
            out = out + wgt[:, b2:b2 + 1] * o_sc[b2]
        out = out / den
        o8 = out[0:ds, :]
        for h in range(1, MOBA_HEADS):
            o8 = o8 + out[h * ds:(h + 1) * ds, :]
        o_ref[0] = o8.astype(BF16)


def _moba_sample(page_table, q, k_new, v_new, pool_t):
    db, ds, w = q.shape
    n_pages = page_table.shape[1]
    past = n_pages * PAGE
    nbs = past // MOBA_BLOCK
    assert n_pages % PAGES_PER_STEP == 0 and past % MOBA_BLOCK == 0 and nbs <= LANES and ds == 8
    tok = pl.BlockSpec((1, ds, w), lambda bi, c, pt: (bi, 0, 0))
    pg_specs = [pl.BlockSpec((1, 2, w, PAGE), lambda bi, c, pt, j=j: (pt[bi, c * PAGES_PER_STEP + j], 0, 0, 0))
                for j in range(PAGES_PER_STEP)]
    gs = pltpu.PrefetchScalarGridSpec(
        num_scalar_prefetch=1, grid=(db, n_pages // PAGES_PER_STEP),
        in_specs=[tok, tok, tok] + pg_specs, out_specs=tok,
        scratch_shapes=[pltpu.VMEM((MOBA_HEADS * ds, LANES), F32), pltpu.VMEM((MOBA_HEADS * ds, LANES), F32),
                        pltpu.VMEM((nbs, MOBA_HEADS * ds, w), F32), pltpu.VMEM((w, LANES), F32)])
    return pl.pallas_call(
        functools.partial(_moba_sample_kernel, past=past, ds=ds, nbs=nbs), grid_spec=gs,
        out_shape=jax.ShapeDtypeStruct((db, ds, w), BF16),
        compiler_params=_cparams(("arbitrary", "arbitrary")), name="moba_sample",
    )(page_table, q, k_new, v_new, *([pool_t] * PAGES_PER_STEP))


def _nsa_sample_kernel(pt_ref, qg_ref, ck_ref, cv_ref, skn_ref, wkn_ref, wst_ref, ng_ref, ovl_ref, exp_ref,
                       *rest, past, ds, n_pages, ns):
    pg_refs = rest[:n_pages]
    o_ref = rest[n_pages]
    rows = NSA_HEADS * ds
    grows = NSA_HPG * ds
    kw = NSA_GROUPS * HEAD_DIM
    ncp = ck_ref.shape[1]
    nsp = ovl_ref.shape[1]
    wb = wst_ref.shape[2]
    row1 = lax.broadcasted_iota(jnp.int32, (rows, 1), 0)
    qi = _mod_pow2(row1, ds)
    qpos = past + qi
    slope = jnp.zeros((rows, 1), F32)
    for h in range(NSA_HEADS):
        slope = jnp.where(_div_pow2(row1, ds) == h, NSA_SLOPES[h], slope)
    lane = lax.broadcasted_iota(jnp.int32, (rows, LANES), 1)
    grp_half = _div_pow2(lane, HEAD_DIM) == _div_pow2(row1, grows)
    q_grp = qg_ref[0]

    def softmax_parts(s_list, m_list):
        m = None
        for s, mk in zip(s_list, m_list):
            mm = jnp.max(jnp.where(mk, s, NEG), axis=1, keepdims=True)
            m = mm if m is None else jnp.maximum(m, mm)
        ps = [jnp.where(mk, jnp.exp(jnp.where(mk, s, NEG) - m), 0.0) for s, mk in zip(s_list, m_list)]
        den = sum(jnp.sum(p, axis=1, keepdims=True) for p in ps)
        return ps, jnp.where(den > 0.0, den, 1.0)

    def own_group(o):
        return jnp.where(grp_half, o, pltpu.roll(o, HEAD_DIM, 1))

    cend = lax.broadcasted_iota(jnp.int32, (rows, ncp), 1) * CMP_STRIDE + (CMP_LEN - 1)
    cmask = cend <= qpos
    s_c = _dot_nt(q_grp, ck_ref[0]) + slope * (cend - qpos).astype(F32)
    (e,), den = softmax_parts([s_c], [cmask])
    pn = e / den
    o_c = own_group(_dot(pn.astype(BF16), cv_ref[0]))
    lane_s = lax.broadcasted_iota(jnp.int32, (ds, nsp), 1)
    cur = _div_pow2(past + lax.broadcasted_iota(jnp.int32, (ds, 1), 0), SEL_BLOCK)
    forced = (lane_s == 0) | (lane_s == cur) | (lane_s == cur - 1)
    cand = lane_s <= cur
    sel_parts = []
    for g in range(NSA_GROUPS):
        psum = pn[g * grows:g * grows + ds]
        for p_ in range(1, NSA_HPG):
            psum = psum + pn[g * grows + p_ * ds:g * grows + (p_ + 1) * ds]
        ph, plo = _split_bf16(psum)
        imp = _dot(ph, ovl_ref[...]) + _dot(plo, ovl_ref[...])
        val = jnp.where(cand, jnp.where(forced, jnp.inf, imp), -jnp.inf)
        rank = _top_rank(val, ns, lane_s)
        sel_g = jnp.where(cand & (rank < SEL_TOPN), 1.0, 0.0)
        sel_parts.append(jnp.concatenate([sel_g] * NSA_HPG, axis=0))
    sel_f32 = jnp.concatenate(sel_parts, axis=0)
    sel_rows = sel_f32.astype(BF16)

    kt_all = jnp.concatenate([r[0, 0:kw, :] for r in pg_refs], axis=1).astype(BF16)
    vt_all = jnp.concatenate([r[0, kw:2 * kw, :] for r in pg_refs], axis=1).astype(BF16)
    kpos = lax.broadcasted_iota(jnp.int32, (rows, past), 1)
    s_past = _dot(q_grp, kt_all) + slope * (kpos - qpos).astype(F32)
    m_past = _dot(sel_rows, exp_ref[...]) > 0.5
    zpad = jnp.zeros((LANES - ds, 2 * kw), F32)
    skn = jnp.concatenate([skn_ref[0], zpad], axis=0)
    s_new = _dot_nt(q_grp, skn[:, 0:kw].astype(BF16)) + slope * (lane - qi).astype(F32)
    new_blk = past // SEL_BLOCK
    m_new = (lane <= qi) & (sel_f32[:, new_blk:new_blk + 1] > 0.5)
    (p_past, p_new), den = softmax_parts([s_past, s_new], [m_past, m_new])
    o_s = _dot_nt(p_past.astype(BF16), vt_all) + _dot(p_new.astype(BF16), skn[:, kw:2 * kw].astype(BF16))
    o_s = own_group(o_s / den)

    jst = lax.broadcasted_iota(jnp.int32, (rows, wb), 1)
    dist_st = qpos - (past - wb + jst)
    s_st = _dot(q_grp, wst_ref[0, 0:kw, :].astype(BF16)) - slope * dist_st.astype(F32)
    m_st = (dist_st >= 0) & (dist_st <= WINDOW)
    wkn = jnp.concatenate([wkn_ref[0], zpad], axis=0)
    s_wn = _dot_nt(q_grp, wkn[:, 0:kw].astype(BF16)) + slope * (lane - qi).astype(F32)
    m_wn = lane <= qi
    (p_st, p_wn), den = softmax_parts([s_st, s_wn], [m_st, m_wn])
    o_w = _dot_nt(p_st.astype(BF16), wst_ref[0, kw:2 * kw, :].astype(BF16)) \
        + _dot(p_wn.astype(BF16), wkn[:, kw:2 * kw].astype(BF16))
    o_w = own_group(o_w / den)

    ng = ng_ref[0]
    gate = lambda br: jnp.concatenate([ng[:, br * NSA_HEADS + h:br * NSA_HEADS + h + 1] for h in range(NSA_HEADS)],
                                      axis=0)
    o = gate(0) * o_c + gate(1) * o_s + gate(2) * o_w
    lane8 = lax.broadcasted_iota(jnp.int32, (ds, LANES), 1)
    for pair in range(NSA_HEADS // 2):
        o_even = o[(2 * pair) * ds:(2 * pair + 1) * ds]
        o_odd = o[(2 * pair + 1) * ds:(2 * pair + 2) * ds]
        o_ref[0, :, pair * LANES:(pair + 1) * LANES] = jnp.where(lane8 < HEAD_DIM, o_even, o_odd).astype(BF16)


def _nsa_sample(page_table, q_grp, ck, cv, sk_new, wk_new, wstate_t, ng, ovl, expand, pool_t):
    db, rows, _ = q_grp.shape
    ds = rows // NSA_HEADS
    n_pages = page_table.shape[1]
    past = n_pages * PAGE
    ns = -(-(past + ds) // SEL_BLOCK)
    fw = pool_t.shape[1]
    per_b = lambda a: pl.BlockSpec((1,) + a.shape[1:], lambda bi, pt: (bi, 0, 0))
    cst = lambda a: pl.BlockSpec(a.shape, lambda bi, pt, nd=a.ndim: (0,) * nd)
    pg_specs = [pl.BlockSpec((1, fw, PAGE), lambda bi, pt, j=j: (pt[bi, j], 0, 0)) for j in range(n_pages)]
    gs = pltpu.PrefetchScalarGridSpec(
        num_scalar_prefetch=1, grid=(db,),
        in_specs=[per_b(q_grp), per_b(ck), per_b(cv), per_b(sk_new), per_b(wk_new), per_b(wstate_t),
                  per_b(ng), cst(ovl), cst(expand)] + pg_specs,
        out_specs=pl.BlockSpec((1, ds, NSA_HEADS * HEAD_DIM), lambda bi, pt: (bi, 0, 0)))
    return pl.pallas_call(
        functools.partial(_nsa_sample_kernel, past=past, ds=ds, n_pages=n_pages, ns=ns), grid_spec=gs,
        out_shape=jax.ShapeDtypeStruct((db, ds, NSA_HEADS * HEAD_DIM), BF16),
        compiler_params=_cparams(("arbitrary",)), name="nsa_sample",
    )(page_table, q_grp, ck, cv, sk_new, wk_new, wstate_t, ng, ovl, expand, *([pool_t] * n_pages))


def _merge_kernel(x_ref, om_ref, on_ref, mg_ref, wa_ref, wb_ref, wo_ref, h_ref):
    d = x_ref.shape[1]
    a = _dot(om_ref[...], wa_ref[...])
    b = _dot(on_ref[...], wb_ref[...])
    merged = mg_ref[:, 0:d] * a + mg_ref[:, d:2 * d] * b
    h_ref[...] = x_ref[...] + _dot(merged.astype(BF16), wo_ref[...])


def _merge(x2d, o_m, o_n, mg, wa, wb, wo):
    t, d = x2d.shape
    row = lambda w: pl.BlockSpec((TM, w), lambda i: (i, 0))
    return pl.pallas_call(
        _merge_kernel, grid=(t // TM,),
        in_specs=[row(d), row(o_m.shape[1]), row(o_n.shape[1]), row(2 * d),
                  _const_spec(wa.shape), _const_spec(wb.shape), _const_spec(wo.shape)],
        out_specs=row(d), out_shape=jax.ShapeDtypeStruct((t, d), F32),
        compiler_params=_cparams(("arbitrary",)), name="merge",
    )(x2d, o_m, o_n, mg, wa, wb, wo)


def _ffn_kernel(h_ref, *rest, seq_len, per_seq_state):
    if per_seq_state:
        st_ref, gf_ref, win_ref, cw_ref, cb_ref, wd_ref, y_ref, cs_ref, carry_sc = rest
    else:
        st0_ref, st1_ref, gf_ref, win_ref, cw_ref, cb_ref, wd_ref, y_ref, g_ref = rest
    f = cw_ref.shape[1]
    h = h_ref[...]
    hn = (h * lax.rsqrt(jnp.mean(h * h, axis=-1, keepdims=True) + RMS_EPS) * gf_ref[...]).astype(BF16)
    gu = _dot(hn, win_ref[...])
    g = gu[:, 0:f]
    u = gu[:, f:2 * f]
    idx = _mod_pow2(lax.broadcasted_iota(jnp.int32, (TM, 1), 0), seq_len)
    if per_seq_state:
        @pl.when(pl.program_id(1) == 0)
        def _():
            carry_sc[0:CONV_W - 1, :] = st_ref[0]
        st0 = carry_sc[0:1, :]
        st1 = carry_sc[1:2, :]
    else:
        st0 = st0_ref[...]
        st1 = st1_ref[...]
    g1 = pltpu.roll(g, 1, 0)
    g2 = pltpu.roll(g, 2, 0)
    prev1 = jnp.where(idx == 0, st1, g1)
    prev2 = jnp.where(idx == 0, st0, jnp.where(idx == 1, st1, g2))
    gc = cb_ref[...] + prev2 * cw_ref[0:1, :] + prev1 * cw_ref[1:2, :] + g * cw_ref[2:3, :]
    act = (jax.nn.silu(gc) * u).astype(BF16)
    y_ref[...] = h + _dot(act, wd_ref[...])
    if per_seq_state:
        carry_sc[0:CONV_W - 1, :] = g[TM - (CONV_W - 1):TM, :]
        cs_ref[0] = g[TM - (CONV_W - 1):TM, :]
    else:
        g_ref[...] = g


def _ffn_prompt(h2d, n, conv_state, gf, win, cw, cb, wd):
    t, d = h2d.shape
    f = cw.shape[1]
    nt = (t // n) // TM
    row = pl.BlockSpec((TM, d), lambda bi, i: (bi * nt + i, 0))
    st = pl.BlockSpec((1, CONV_W - 1, f), lambda bi, i: (bi, 0, 0))
    cst = lambda a: pl.BlockSpec(a.shape, lambda bi, i, nd=a.ndim: (0,) * nd)
    return pl.pallas_call(
        functools.partial(_ffn_kernel, seq_len=TM, per_seq_state=True), grid=(n, nt),
        in_specs=[row, st, cst(gf), cst(win), cst(cw), cst(cb), cst(wd)],
        out_specs=(row, st),
        out_shape=(jax.ShapeDtypeStruct((t, d), F32), jax.ShapeDtypeStruct((n, CONV_W - 1, f), F32)),
        scratch_shapes=[pltpu.VMEM((8, f), F32)],
        compiler_params=_cparams(("arbitrary", "arbitrary")), name="ffn_prompt",
    )(h2d, conv_state, gf, win, cw, cb, wd)


def _ffn_sample(h2d, seq_len, st0, st1, gf, win, cw, cb, wd):
    t, d = h2d.shape
    f = cw.shape[1]
    row = lambda w: pl.BlockSpec((TM, w), lambda i: (i, 0))
    return pl.pallas_call(
        functools.partial(_ffn_kernel, seq_len=seq_len, per_seq_state=False), grid=(t // TM,),
        in_specs=[row(d), row(f), row(f), _const_spec(gf.shape), _const_spec(win.shape), _const_spec(cw.shape),
                  _const_spec(cb.shape), _const_spec(wd.shape)],
        out_specs=(row(d), row(f)),
        out_shape=(jax.ShapeDtypeStruct((t, d), F32), jax.ShapeDtypeStruct((t, f), F32)),
        compiler_params=_cparams(("arbitrary",)), name="ffn_sample",
    )(h2d, st0, st1, gf, win, cw, cb, wd)


def _block_diag_ones(n, blk):
    i = np.arange(n)
    return jnp.asarray((i[:, None] // blk == i[None, :] // blk).astype(np.float32), BF16)


def _overlap_matrix(ncp, nc, ns, nsp):
    c0 = np.arange(ncp) * CMP_STRIDE
    j0 = np.arange(nsp) * SEL_BLOCK
    m = (c0[:, None] < j0[None, :] + SEL_BLOCK) & (c0[:, None] + CMP_LEN > j0[None, :])
    m &= (np.arange(ncp)[:, None] < nc) & (np.arange(nsp)[None, :] < ns)
    return m.astype(np.float32)


def _layer_weights(l, attn_norm_g, w_in, qk_norm_g, nsa_cmp_pe, nsa_cmp_w1, nsa_cmp_b1, nsa_cmp_w2, nsa_cmp_b2,
                   w_branch_moba, w_branch_nsa, w_out, ffn_norm_g, w_ffn_in, ffn_conv_w, ffn_conv_b, w_ffn_down):
    d = w_in.shape[1]
    wi = w_in[l]
    mw, nw, kvw = MOBA_HEADS * HEAD_DIM, NSA_HEADS * HEAD_DIM, NSA_GROUPS * HEAD_DIM
    o_nq = 3 * mw
    o_c = o_nq + nw
    o_s = o_c + 2 * kvw
    o_w = o_s + 2 * kvw
    o_ng = o_w + 2 * kvw
    o_mg = o_ng + 3 * NSA_HEADS
    w1 = jnp.concatenate([wi[:, :o_s], wi[:, o_s:o_s + kvw], wi[:, o_w:o_w + kvw],
                          wi[:, o_s + kvw:o_w], wi[:, o_w + kvw:o_ng]], axis=1).astype(BF16)
    wng = jnp.pad(wi[:, o_ng:o_mg], ((0, 0), (0, LANES - 3 * NSA_HEADS))).astype(BF16)
    wmg = wi[:, o_mg:o_mg + 2 * d].astype(BF16)
    g = qk_norm_g[l]
    qkg = jnp.stack([jnp.tile(g[0], MOBA_HEADS), jnp.tile(g[1], MOBA_HEADS), jnp.tile(g[2], NSA_HEADS),
                     jnp.concatenate([jnp.tile(g[4], NSA_GROUPS), jnp.tile(g[5], NSA_GROUPS),
                                      jnp.zeros((nw - 2 * kvw,), F32)])])
    r = CMP_LEN // CMP_STRIDE
    w1c = nsa_cmp_w1[l]
    w1r = w1c.reshape(2, r, CMP_STRIDE, HEAD_DIM, CMP_HIDDEN)
    nkvg = 2 * NSA_GROUPS
    wfull = jnp.zeros((CMP_STRIDE, nkvg, HEAD_DIM, nkvg, r, CMP_HIDDEN), F32)
    for kvg in range(nkvg):
        wfull = wfull.at[:, kvg, :, kvg].set(jnp.transpose(w1r[kvg // NSA_GROUPS], (1, 2, 0, 3)))
    wfull = wfull.reshape(CMP_STRIDE * nkvg * HEAD_DIM, nkvg * r * CMP_HIDDEN).astype(BF16)
    w2bd = jnp.zeros((nkvg, CMP_HIDDEN, nkvg, HEAD_DIM), F32)
    for kvg in range(nkvg):
        w2bd = w2bd.at[kvg, :, kvg].set(nsa_cmp_w2[l][kvg // NSA_GROUPS])
    w2bd = w2bd.reshape(nkvg * CMP_HIDDEN, nkvg * HEAD_DIM).astype(BF16)
    b2 = nsa_cmp_b2[l]
    cw = dict(
        wfull=wfull,
        pe=jnp.broadcast_to(nsa_cmp_pe[l].reshape(2, 1, CMP_LEN * HEAD_DIM), (2, 8, CMP_LEN * HEAD_DIM)).astype(BF16),
        w1=w1c.astype(BF16), b1=nsa_cmp_b1[l], w2bd=w2bd,
        b2row=jnp.concatenate([jnp.tile(b2[0], NSA_GROUPS), jnp.tile(b2[1], NSA_GROUPS)])[None, :],
        gk=jnp.tile(g[3], NSA_GROUPS)[None, :], bd=_block_diag_ones(kvw, HEAD_DIM))
    return dict(
        g_attn=attn_norm_g[l][None, :], w1=w1, wng=wng, wmg=wmg, qkg=qkg, bd=_block_diag_ones(256, HEAD_DIM), cw=cw,
        wa=w_branch_moba[l].astype(BF16), wb=w_branch_nsa[l].astype(BF16), wo=w_out[l].astype(BF16),
        gf=ffn_norm_g[l][None, :], win=w_ffn_in[l].astype(BF16), cwt=ffn_conv_w[l], cb=ffn_conv_b[l][None, :],
        wd=w_ffn_down[l].astype(BF16))


def _rows_t(a2d, n, length):
    return jnp.swapaxes(a2d.reshape(n, length, a2d.shape[1]), 1, 2)


def _prompt_layer(x, wts):
    b, s, d = x.shape
    assert s % (CMP_STRIDE * LANES) == 0 and s % TQ == 0
    t = b * s
    x2d = x.reshape(t, d)
    mq, mkv, nq, ckv, skv, wkv, ng, mg, ksum = _inproj(x2d, wts["g_attn"], wts["w1"], wts["wng"], wts["wmg"],
                                                      wts["qkg"], wts["bd"])
    nb = s // MOBA_BLOCK
    mw = MOBA_HEADS * HEAD_DIM
    kw = NSA_GROUPS * HEAD_DIM
    nbp = -(-nb // 16) * 16
    means = jnp.pad(ksum.reshape(b, nb, mw) * (1.0 / MOBA_BLOCK), ((0, 0), (0, nbp - nb), (0, 0)))
    m_bias, m_diag, _ = _alibi_tables(MOBA_SLOPES)
    o_m = _moba_prompt(_rows_t(mq, b, s), mkv[:, 0:mw].astype(BF16).reshape(b, s, mw),
                       _rows_t(mkv[:, mw:2 * mw].astype(BF16), b, s), means, m_bias, m_diag)

    ncp = s // CMP_STRIDE
    ck, cv = _compress_prompt(ckv[:, 0:kw].reshape(b, s, kw), ckv[:, kw:2 * kw].reshape(b, s, kw), wts["cw"])
    ns = s // SEL_BLOCK
    nsr = -(-ns // 8) * 8
    ovl_t = jnp.asarray(_overlap_matrix(ncp, ncp - CMP_LEN // CMP_STRIDE + 1, ns, nsr).T, BF16)
    cb = np.asarray(NSA_SLOPES, np.float32)[:, None, None] * (
        (np.arange(ncp) * CMP_STRIDE + CMP_LEN - 1)[None, :, None] - np.arange(TQ)[None, None, :]).astype(np.float32)
    n_bias, n_diag, n_edge = _alibi_tables(NSA_SLOPES)
    ngr = -(-3 * NSA_HEADS // 8) * 8
    o_n = _nsa_prompt(_rows_t(nq, b, s), ck, jnp.swapaxes(cv, 1, 2),
                      skv[:, 0:kw].astype(BF16).reshape(b, s, kw), _rows_t(skv[:, kw:2 * kw].astype(BF16), b, s),
                      wkv[:, 0:kw].astype(BF16).reshape(b, s, kw), _rows_t(wkv[:, kw:2 * kw].astype(BF16), b, s),
                      _rows_t(ng[:, 0:ngr], b, s), ovl_t, jnp.asarray(cb), n_bias, n_diag, n_edge)

    h2d = _merge(x2d, o_m.reshape(t, -1), o_n.reshape(t, -1), mg, wts["wa"], wts["wb"], wts["wo"])
    f = wts["cwt"].shape[1]
    y2d, conv = _ffn_prompt(h2d, b, jnp.zeros((b, CONV_W - 1, f), F32), wts["gf"], wts["win"], wts["cwt"],
                            wts["cb"], wts["wd"])
    wrows = min(WINDOW, s)
    outs = (mkv.reshape(b, s, 2, MOBA_HEADS, HEAD_DIM), ckv.reshape(b, s, 2, NSA_GROUPS, HEAD_DIM),
            skv.reshape(b, s, 2, NSA_GROUPS, HEAD_DIM),
            wkv.reshape(b, s, 2, NSA_GROUPS, HEAD_DIM)[:, s - wrows:], conv)
    return y2d.reshape(b, s, d), outs


def _pages_t(cache):
    dp, npool = cache.shape[0], cache.shape[1]
    return jnp.transpose(cache, (0, 1, 3, 4, 5, 2)).reshape(dp * npool, -1, PAGE)


def _sample_layer(x, wts, pool_moba_t, pool_cmp_t, pool_sel_t, win_state, conv_state, page_table):
    db, ds, d = x.shape
    t = db * ds
    assert t % TM == 0 and ds == 8
    n_pages = page_table.shape[1]
    past = n_pages * PAGE
    x2d = x.reshape(t, d)
    mq, mkv, nq, ckv, skv, wkv, ng, mg, _ = _inproj(x2d, wts["g_attn"], wts["w1"], wts["wng"], wts["wmg"],
                                                   wts["qkg"], wts["bd"])
    mw = MOBA_HEADS * HEAD_DIM
    kw = NSA_GROUPS * HEAD_DIM
    o_m = _moba_sample(page_table, mq.reshape(db, ds, mw), mkv[:, 0:mw].reshape(db, ds, mw),
                       mkv[:, mw:2 * mw].reshape(db, ds, mw), pool_moba_t.reshape(-1, 2, mw, PAGE))

    assert (past + ds) // CMP_STRIDE == past // CMP_STRIDE
    ncp = past // CMP_STRIDE
    ck, cv = _compress_sample(page_table, pool_cmp_t, wts["cw"])

    nqf = nq.astype(F32).reshape(db, ds, NSA_HEADS, HEAD_DIM).transpose(0, 2, 1, 3)
    zero = jnp.zeros_like(nqf)
    hh = jnp.arange(NSA_HEADS)[None, :, None, None]
    q_grp = jnp.where(hh // NSA_HPG == 0, jnp.concatenate([nqf, zero], -1), jnp.concatenate([zero, nqf], -1))
    q_grp = q_grp.reshape(db, NSA_HEADS * ds, LANES).astype(BF16)
    ns = -(-(past + ds) // SEL_BLOCK)
    nsp = -(-ns // LANES) * LANES
    ovl = jnp.asarray(_overlap_matrix(ncp, ncp - CMP_LEN // CMP_STRIDE + 1, ns, nsp), BF16)
    expand = jnp.asarray((np.arange(nsp)[:, None] == np.arange(past)[None, :] // SEL_BLOCK).astype(np.float32), BF16)
    wb = win_state.shape[1]
    wst_t = jnp.transpose(win_state, (0, 2, 3, 4, 1)).reshape(db, 2 * kw, wb)
    o_n = _nsa_sample(page_table, q_grp, ck, cv, skv.reshape(db, ds, -1), wkv.reshape(db, ds, -1), wst_t,
                      ng.reshape(db, ds, LANES), ovl, expand, pool_sel_t)

    h2d = _merge(x2d, o_m.reshape(t, -1), o_n.reshape(t, -1), mg, wts["wa"], wts["wb"], wts["wo"])
    st0 = jnp.repeat(conv_state[:, 0], ds, axis=0)
    st1 = jnp.repeat(conv_state[:, 1], ds, axis=0)
    y2d, g2d = _ffn_sample(h2d, ds, st0, st1, wts["gf"], wts["win"], wts["cwt"], wts["cb"], wts["wd"])
    gp = jnp.concatenate([conv_state, g2d.reshape(db, ds, -1)], axis=1)
    wcat = jnp.concatenate([win_state, wkv.reshape(db, ds, 2, NSA_GROUPS, HEAD_DIM)], axis=1)
    wrows = min(WINDOW, wcat.shape[1])
    outs = (mkv.reshape(db, ds, 2, MOBA_HEADS, HEAD_DIM), ckv.reshape(db, ds, 2, NSA_GROUPS, HEAD_DIM),
            skv.reshape(db, ds, 2, NSA_GROUPS, HEAD_DIM), wcat[:, wcat.shape[1] - wrows:], gp[:, ds:])
    return y2d.reshape(db, ds, d), outs


def kernel(x_prompt, x_sample, cache_moba_kv, cache_nsa_cmp_kv, cache_nsa_sel_kv, state_nsa_win_kv, state_ffn_conv,
           page_table, attn_norm_g, w_in, qk_norm_g, nsa_cmp_pe, nsa_cmp_w1, nsa_cmp_b1, nsa_cmp_w2, nsa_cmp_b2,
           w_branch_moba, w_branch_nsa, w_out, ffn_norm_g, w_ffn_in, ffn_conv_w, ffn_conv_b, w_ffn_down):
    depth = w_in.shape[0]
    n_pool = cache_moba_kv.shape[1]
    pools = [_pages_t(c) for c in (cache_moba_kv, cache_nsa_cmp_kv, cache_nsa_sel_kv)]
    hp, hs = x_prompt, x_sample
    new = [[] for _ in range(10)]
    for l in range(depth):
        wts = _layer_weights(l, attn_norm_g, w_in, qk_norm_g, nsa_cmp_pe, nsa_cmp_w1, nsa_cmp_b1, nsa_cmp_w2,
                             nsa_cmp_b2, w_branch_moba, w_branch_nsa, w_out, ffn_norm_g, w_ffn_in, ffn_conv_w,
                             ffn_conv_b, w_ffn_down)
        hp, outs_p = _prompt_layer(hp, wts)
        hs, outs_s = _sample_layer(hs, wts, *pools, state_nsa_win_kv[l], state_ffn_conv[l],
                                   page_table + l * n_pool)
        for lst, arr in zip(new[:5], outs_p):
            lst.append(arr)
        for lst, arr in zip(new[5:], outs_s):
            lst.append(arr)
    st = [jnp.stack(v) for v in new]
    return (hp, hs, st[0], st[5], st[1], st[6], st[2], st[7], st[3], st[8], st[4], st[9])
```

```python
import functools

import numpy as np
import jax
import jax.numpy as jnp
from jax import lax
from jax.experimental import pallas as pl
from jax.experimental.pallas import tpu as pltpu

F32 = jnp.float32
BF16 = jnp.bfloat16

HEAD_DIM = 64
MOBA_HEADS = 8
MOBA_BLOCK = 256
MOBA_TOPK = 3
NSA_HEADS = 8
NSA_GROUPS = 2
NSA_HPG = NSA_HEADS // NSA_GROUPS
CMP_LEN = 32
CMP_STRIDE = 16
CMP_HIDDEN = 128
SEL_BLOCK = 64
SEL_TOPN = 8
WINDOW = 512
CONV_W = 3
PAGE = 128
RMS_EPS = 1e-6
NEG = -1e30
BIG = 1e30
LOG2E = 1.4426950408889634
QK_SCALE = HEAD_DIM ** -0.5 * LOG2E

LANES = 128
TQ = 256
KEY_CHUNK = 64
TM = 256
PAGES_PER_STEP = 8
VMEM_LIMIT = 56 * 1024 * 1024

MOBA_SLOPES = tuple(float(2.0 ** (-8.0 * (h + 1) / MOBA_HEADS)) * LOG2E for h in range(MOBA_HEADS))
NSA_SLOPES = tuple(float(2.0 ** (-8.0 * (h + 1) / NSA_HEADS)) * LOG2E for h in range(NSA_HEADS))


def _dot(a, b):
    return jnp.dot(a, b, preferred_element_type=F32)


def _dot_nt(a, b):
    return lax.dot_general(a, b, (((1,), (1,)), ((), ())), preferred_element_type=F32)


def _div_pow2(x, n):
    assert n & (n - 1) == 0
    return lax.shift_right_logical(x, jnp.int32(n.bit_length() - 1))


def _mod_pow2(x, n):
    assert n & (n - 1) == 0
    return x & (n - 1)


def _split_bf16(x):
    hi = x.astype(BF16)
    lo = (x - hi.astype(F32)).astype(BF16)
    return hi, lo


def _cparams(sem):
    return pltpu.CompilerParams(dimension_semantics=sem, vmem_limit_bytes=VMEM_LIMIT)


def _const_spec(shape):
    nd = len(shape)
    return pl.BlockSpec(shape, lambda *_: (0,) * nd)


def _seg_rms(y, bd, gain):
    wb = bd.shape[0]
    outs = []
    for c in range(y.shape[1] // wb):
        s = y[:, c * wb:(c + 1) * wb]
        ss = _dot((s * s).astype(BF16), bd)
        outs.append(s * lax.rsqrt(ss * (1.0 / HEAD_DIM) + RMS_EPS))
    r = outs[0] if len(outs) == 1 else jnp.concatenate(outs, axis=1)
    return r * gain


def _top_rank(val, n, lane):
    rank = jnp.zeros(val.shape, F32)
    for j2 in range(n):
        c = val[:, j2:j2 + 1]
        before = (c > val) | ((c == val) & (lane > j2))
        rank = rank + jnp.where(before, 1.0, 0.0)
    return rank


def _top_rank_rows(val, n, rowid):
    rank = jnp.zeros(val.shape, F32)
    for j2 in range(n):
        c = val[j2:j2 + 1, :]
        before = (c > val) | ((c == val) & (rowid > j2))
        rank = rank + jnp.where(before, 1.0, 0.0)
    return rank


MW = MOBA_HEADS * HEAD_DIM
KW = NSA_GROUPS * HEAD_DIM
R_MQ, R_MK, R_MV, R_NQ = 0, MW, 2 * MW, 3 * MW
R_CK = 4 * MW
R_CV, R_SK, R_SV, R_WK, R_WV, R_END = (R_CK + j * KW for j in range(1, 7))
NG_ROWS = -(-3 * NSA_HEADS // 16) * 16


def _inproj_kernel(x_ref, g_ref, w1t_ref, wngt_ref, wmg_ref, gcol_ref,
                   mqt_ref, mkvt_ref, mk_ref, mvt_ref, nqt_ref, ckvt_ref, ck_ref, cv_ref, skvt_ref, sk_ref, svt_ref,
                   wkvt_ref, wk_ref, wvt_ref, ngt_ref, mg_ref, ksum_ref):
    x = x_ref[0]
    xn = x * lax.rsqrt(jnp.mean(x * x, axis=-1, keepdims=True) + RMS_EPS) * g_ref[...]
    mg_ref[...] = jax.nn.sigmoid(_dot(xn.astype(BF16), wmg_ref[...]))
    xnt = jnp.transpose(xn).astype(BF16)
    ngt_ref[0] = jax.nn.sigmoid(_dot(wngt_ref[...], xnt))
    yt = _dot(w1t_ref[...], xnt)

    def normed(r0, n_heads):
        outs = []
        for j in range(n_heads):
            seg = yt[r0 + j * HEAD_DIM:r0 + (j + 1) * HEAD_DIM, :]
            ss = jnp.sum(seg * seg, axis=0, keepdims=True)
            outs.append(seg * lax.rsqrt(ss * (1.0 / HEAD_DIM) + RMS_EPS)
                        * gcol_ref[r0 + j * HEAD_DIM:r0 + (j + 1) * HEAD_DIM, :])
        return jnp.concatenate(outs, axis=0)

    mqt_ref[0] = (normed(R_MQ, MOBA_HEADS) * QK_SCALE).astype(BF16)
    mkt = normed(R_MK, MOBA_HEADS)
    mkvt_ref[0, 0:MW, :] = mkt
    mkvt_ref[0, MW:2 * MW, :] = yt[R_MV:R_NQ, :]
    mk = jnp.transpose(mkt)
    mk_ref[0] = mk.astype(BF16)
    ksum_ref[0] = jnp.sum(mk, axis=0, keepdims=True)
    mvt_ref[0] = yt[R_MV:R_NQ, :].astype(BF16)
    nqt_ref[0] = (normed(R_NQ, NSA_HEADS) * QK_SCALE).astype(BF16)
    ckvt_ref[0] = yt[R_CK:R_SK, :]
    ck_ref[0] = jnp.transpose(yt[R_CK:R_CV, :])
    cv_ref[0] = jnp.transpose(yt[R_CV:R_SK, :])
    skt = normed(R_SK, NSA_GROUPS)
    skvt_ref[0, 0:KW, :] = skt
    skvt_ref[0, KW:2 * KW, :] = yt[R_SV:R_WK, :]
    sk_ref[0] = jnp.transpose(skt).astype(BF16)
    svt_ref[0] = yt[R_SV:R_WK, :].astype(BF16)
    wkt = normed(R_WK, NSA_GROUPS)
    wkvt_ref[0, 0:KW, :] = wkt
    wkvt_ref[0, KW:2 * KW, :] = yt[R_WV:R_END, :]
    wk_ref[0] = jnp.transpose(wkt).astype(BF16)
    wvt_ref[0] = yt[R_WV:R_END, :].astype(BF16)


def _inproj(x, g_attn, w1t, wngt, wmg, gcol):
    n, length, d = x.shape
    nt = length // TM
    ft = lambda rows: pl.BlockSpec((1, rows, TM), lambda b, i: (b, 0, i))
    rw = lambda w: pl.BlockSpec((1, TM, w), lambda b, i: (b, i, 0))
    fts = lambda rows, dt: jax.ShapeDtypeStruct((n, rows, length), dt)
    rws = lambda w, dt: jax.ShapeDtypeStruct((n, length, w), dt)
    cst = lambda a: pl.BlockSpec(a.shape, lambda b, i, nd=a.ndim: (0,) * nd)
    out_specs = (ft(MW), ft(2 * MW), rw(MW), ft(MW), ft(MW), ft(2 * KW), rw(KW), rw(KW), ft(2 * KW), rw(KW), ft(KW),
                 ft(2 * KW), rw(KW), ft(KW), ft(NG_ROWS),
                 pl.BlockSpec((TM, 2 * d), lambda b, i: (b * nt + i, 0)),
                 pl.BlockSpec((1, 1, MW), lambda b, i: (b * nt + i, 0, 0)))
    out_shape = (fts(MW, BF16), fts(2 * MW, F32), rws(MW, BF16), fts(MW, BF16), fts(MW, BF16), fts(2 * KW, F32),
                 rws(KW, F32), rws(KW, F32), fts(2 * KW, F32), rws(KW, BF16), fts(KW, BF16),
                 fts(2 * KW, F32), rws(KW, BF16), fts(KW, BF16), fts(NG_ROWS, F32),
                 jax.ShapeDtypeStruct((n * length, 2 * d), F32), jax.ShapeDtypeStruct((n * nt, 1, MW), F32))
    names = ("mqt", "mkvt", "mk", "mvt", "nqt", "ckvt", "ck", "cv", "skvt", "sk", "svt", "wkvt", "wk", "wvt", "ngt",
             "mg", "ksum")
    outs = pl.pallas_call(
        _inproj_kernel, grid=(n, nt),
        in_specs=[pl.BlockSpec((1, TM, d), lambda b, i: (b, i, 0)), cst(g_attn), cst(w1t), cst(wngt), cst(wmg),
                  cst(gcol)],
        out_specs=out_specs, out_shape=out_shape,
        compiler_params=_cparams(("arbitrary", "arbitrary")), name="inproj",
    )(x, g_attn, w1t, wngt, wmg, gcol)
    return dict(zip(names, outs))


def _alibi_tables(slopes):
    rel = (np.arange(TQ)[:, None] - np.arange(TQ)[None, :]).astype(np.float32)
    bias = np.asarray(slopes, np.float32)[:, None, None] * rel[None]
    diag = np.where(rel[None] <= 0, bias, np.float32(NEG))
    edge = np.where(rel[None] >= 0, bias, np.float32(NEG))
    return jnp.asarray(bias), jnp.asarray(diag), jnp.asarray(edge)


def _moba_prompt_kernel(qt_ref, k_ref, vt_ref, mean_ref, bias_ref, diag_ref, o_ref,
                        qz_sc, sel_sc, m_sc, l_sc, acc_sc, s_sc, p_sc, *, nb):
    i = pl.program_id(1)
    npair = MOBA_HEADS // 2
    zeros = jnp.zeros((HEAD_DIM, TQ), BF16)
    for pair in range(npair):
        r0 = 2 * pair * HEAD_DIM
        qz_sc[pair, :, 0:TQ] = jnp.concatenate([qt_ref[0, r0:r0 + HEAD_DIM, :], zeros], axis=0)
        qz_sc[pair, :, TQ:2 * TQ] = jnp.concatenate([zeros, qt_ref[0, r0 + HEAD_DIM:r0 + 2 * HEAD_DIM, :]], axis=0)
    m_sc[...] = jnp.full(m_sc.shape, NEG, F32)
    l_sc[...] = jnp.zeros(l_sc.shape, F32)
    acc_sc[...] = jnp.zeros(acc_sc.shape, F32)

    nbp = mean_ref.shape[1]
    rowb = lax.broadcasted_iota(jnp.int32, (nbp, TQ), 0)
    cand = rowb < i
    for h in range(MOBA_HEADS):
        pair, half = divmod(h, 2)
        qz = qz_sc[pair, :, half * TQ:(half + 1) * TQ]
        mh, ml = _split_bf16(mean_ref[0, :, pair * LANES:(pair + 1) * LANES])
        ss = _dot(mh, qz) + _dot(ml, qz)
        rank = _top_rank_rows(jnp.where(cand, ss, -jnp.inf), nb, rowb)
        sel_sc[h] = jnp.where(cand & (rank < MOBA_TOPK), 1.0, 0.0)

    def kv_block(kb, own):
        start = pl.multiple_of(kb * TQ, TQ)
        k_blk = k_ref[0, pl.ds(start, TQ), :]
        vt_blk = vt_ref[0, :, pl.ds(start, TQ)]
        off = ((kb - i) * TQ).astype(F32)
        tab_ref = diag_ref if own else bias_ref
        bms = []
        for pair in range(npair):
            s2 = _dot(k_blk[:, pair * LANES:(pair + 1) * LANES], qz_sc[pair])
            for half in range(2):
                h = 2 * pair + half
                s = s2[:, half * TQ:(half + 1) * TQ] + tab_ref[h]
                s_sc[h] = s
                bms.append(jnp.max(s, axis=0, keepdims=True))
        alphas = []
        for h in range(MOBA_HEADS):
            m_old = m_sc[h:h + 1, :]
            if own:
                m_new = jnp.maximum(m_old, bms[h])
                shift = m_new
            else:
                c = MOBA_SLOPES[h] * off
                sel = sel_sc[h, pl.ds(kb, 1), :] > 0.5
                m_new = jnp.maximum(m_old, jnp.where(sel, bms[h] + c, NEG))
                shift = jnp.where(sel, m_new - c, BIG)
            alpha = jnp.exp2(m_old - m_new)
            psum = jnp.zeros((8, TQ), F32)
            for r0 in range(0, TQ, KEY_CHUNK):
                rs = slice(r0, r0 + KEY_CHUNK)
                p = jnp.exp2(s_sc[h, rs, :] - shift)
                psum = psum + jnp.sum(p.reshape(KEY_CHUNK // 8, 8, TQ), axis=0)
                p_sc[h, rs, :] = p.astype(BF16)
            l_sc[h:h + 1, :] = alpha * l_sc[h:h + 1, :] + jnp.sum(psum, axis=0, keepdims=True)
            m_sc[h:h + 1, :] = m_new
            alphas.append(alpha)
        for h in range(MOBA_HEADS):
            rows = slice(h * HEAD_DIM, (h + 1) * HEAD_DIM)
            acc_sc[rows, :] = alphas[h] * acc_sc[rows, :] + _dot(vt_blk[rows, :], p_sc[h])

    def body(kb, carry):
        kv_block(kb, False)
        return carry

    lax.fori_loop(0, i, body, 0)
    kv_block(i, True)

    for h in range(MOBA_HEADS):
        rows = slice(h * HEAD_DIM, (h + 1) * HEAD_DIM)
        acc_sc[rows, :] = acc_sc[rows, :] / l_sc[h:h + 1, :]
    o_ref[0] = jnp.transpose(acc_sc[...]).astype(BF16)


def _moba_prompt(qt, k, vt, means, bias, diag):
    b, w, s = qt.shape
    nb = s // TQ
    return pl.pallas_call(
        functools.partial(_moba_prompt_kernel, nb=nb), grid=(b, nb),
        in_specs=[pl.BlockSpec((1, w, TQ), lambda bi, i: (bi, 0, i)),
                  pl.BlockSpec((1, s, w), lambda bi, i: (bi, 0, 0)),
                  pl.BlockSpec((1, w, s), lambda bi, i: (bi, 0, 0)),
                  pl.BlockSpec((1,) + means.shape[1:], lambda bi, i: (bi, 0, 0)),
                  _const_spec(bias.shape), _const_spec(diag.shape)],
        out_specs=pl.BlockSpec((1, TQ, w), lambda bi, i: (bi, i, 0)),
        out_shape=jax.ShapeDtypeStruct((b, s, w), BF16),
        scratch_shapes=[pltpu.VMEM((MOBA_HEADS // 2, LANES, 2 * TQ), BF16),
                        pltpu.VMEM((MOBA_HEADS, means.shape[1], TQ), F32),
                        pltpu.VMEM((MOBA_HEADS, TQ), F32), pltpu.VMEM((MOBA_HEADS, TQ), F32),
                        pltpu.VMEM((w, TQ), F32), pltpu.VMEM((MOBA_HEADS, TQ, TQ), F32),
                        pltpu.VMEM((MOBA_HEADS, TQ, TQ), BF16)],
        compiler_params=_cparams(("arbitrary", "arbitrary")), name="moba_prompt",
    )(qt, k, vt, means, bias, diag)


def _compress_core(load_rows, ncp, wfull_ref, pe_ref, w1_ref, b1_ref, w2_ref, b2_ref, gk_ref, bd_ref, ck_ref, cv_ref):
    kw = NSA_GROUPS * HEAD_DIM
    hw = wfull_ref.shape[1] // 2
    projs = []
    for kv in range(2):
        acc = jnp.zeros((ncp, hw), F32)
        for t in range(CMP_STRIDE):
            r0 = t * 2 * kw + kv * kw
            acc = acc + _dot(load_rows(kv, t).astype(BF16), wfull_ref[r0:r0 + kw, kv * hw:(kv + 1) * hw])
        projs.append(acc)
    proj = jnp.concatenate(projs, axis=1)
    hids = []
    for kvg in range(2 * NSA_GROUPS):
        kv = kvg // NSA_GROUPS
        pe_term = _dot(pe_ref[kv], w1_ref[kv])[0:1, :]
        c0 = kvg * 2 * CMP_HIDDEN
        p0 = proj[:, c0:c0 + CMP_HIDDEN]
        p1 = pltpu.roll(proj[:, c0 + CMP_HIDDEN:c0 + 2 * CMP_HIDDEN], ncp - 1, 0)
        hids.append(jax.nn.gelu(b1_ref[kv:kv + 1, :] + p0 + p1 + pe_term))
    hid = jnp.concatenate(hids, axis=1).astype(BF16)
    out = _dot(hid, w2_ref[...]) + b2_ref[...]
    ck_ref[0] = _seg_rms(out[:, 0:kw], bd_ref[...], gk_ref[...]).astype(BF16)
    cv_ref[0] = out[:, kw:2 * kw].astype(BF16)


def _compress_prompt_kernel(xk_ref, xv_ref, *refs):
    ncp = xk_ref.shape[1] // CMP_STRIDE
    x_refs = (xk_ref, xv_ref)
    _compress_core(lambda kv, t: x_refs[kv][0, pl.ds(t, ncp, stride=CMP_STRIDE), :], ncp, *refs)


def _compress_sample_kernel(pt_ref, *refs, n_pages):
    pg_refs = refs[:n_pages]
    x_scs = refs[-2:]
    kw = NSA_GROUPS * HEAD_DIM
    for j in range(n_pages):
        for kv in range(2):
            x_scs[kv][j * PAGE:(j + 1) * PAGE, :] = jnp.transpose(pg_refs[j][0, kv * kw:(kv + 1) * kw, :])
    ncp = n_pages * PAGE // CMP_STRIDE
    _compress_core(lambda kv, t: x_scs[kv][pl.ds(t, ncp, stride=CMP_STRIDE), :], ncp, *refs[n_pages:-2])


def _compress_consts(cw):
    return (cw["wfull"], cw["pe"], cw["w1"], cw["b1"], cw["w2bd"], cw["b2row"], cw["gk"], cw["bd"])


def _compress_prompt(xk, xv, cw):
    b, s, kw = xk.shape
    ncp = s // CMP_STRIDE
    consts = _compress_consts(cw)
    xspec = pl.BlockSpec((1, s, kw), lambda bi: (bi, 0, 0))
    ospec = pl.BlockSpec((1, ncp, kw), lambda bi: (bi, 0, 0))
    return pl.pallas_call(
        _compress_prompt_kernel, grid=(b,),
        in_specs=[xspec, xspec] + [_const_spec(a.shape) for a in consts],
        out_specs=(ospec, ospec),
        out_shape=(jax.ShapeDtypeStruct((b, ncp, kw), BF16), jax.ShapeDtypeStruct((b, ncp, kw), BF16)),
        compiler_params=_cparams(("arbitrary",)), name="nsa_compress_prompt",
    )(xk, xv, *consts)


def _compress_sample(page_table, pool_t, cw):
    db, n_pages = page_table.shape
    fw = pool_t.shape[1]
    ncp = n_pages * PAGE // CMP_STRIDE
    consts = _compress_consts(cw)
    kw = NSA_GROUPS * HEAD_DIM
    cst = lambda a: pl.BlockSpec(a.shape, lambda bi, pt, nd=a.ndim: (0,) * nd)
    ospec = pl.BlockSpec((1, ncp, kw), lambda bi, pt: (bi, 0, 0))
    pg_specs = [pl.BlockSpec((1, fw, PAGE), lambda bi, pt, j=j: (pt[bi, j], 0, 0)) for j in range(n_pages)]
    gs = pltpu.PrefetchScalarGridSpec(
        num_scalar_prefetch=1, grid=(db,), in_specs=pg_specs + [cst(a) for a in consts], out_specs=(ospec, ospec),
        scratch_shapes=[pltpu.VMEM((n_pages * PAGE, kw), F32), pltpu.VMEM((n_pages * PAGE, kw), F32)])
    return pl.pallas_call(
        functools.partial(_compress_sample_kernel, n_pages=n_pages), grid_spec=gs,
        out_shape=(jax.ShapeDtypeStruct((db, ncp, kw), BF16), jax.ShapeDtypeStruct((db, ncp, kw), BF16)),
        compiler_params=_cparams(("arbitrary",)), name="nsa_compress_sample",
    )(page_table, *([pool_t] * n_pages), *consts)


def _nsa_prompt_kernel(qt_ref, ck_ref, cvt_ref, sk_ref, svt_ref, wk_ref, wvt_ref, ng_ref, ovl_ref, cb_ref,
                       bias_ref, diag_ref, edge_ref, o_ref,
                       qz_sc, p_sc, s_sc, sel_sc, oc_sc, ms_sc, ls_sc, as_sc, mw_sc, lw_sc, aw_sc, *, ns):
    i = pl.program_id(1)
    ncp = ck_ref.shape[1]
    zeros = jnp.zeros((HEAD_DIM, TQ), BF16)
    for h in range(NSA_HEADS):
        g, p_ = divmod(h, NSA_HPG)
        qh = qt_ref[0, h * HEAD_DIM:(h + 1) * HEAD_DIM, :]
        qz_sc[g, :, p_ * TQ:(p_ + 1) * TQ] = jnp.concatenate([qh, zeros] if g == 0 else [zeros, qh], axis=0)
    for m_ref, l_ref, a_ref in ((ms_sc, ls_sc, as_sc), (mw_sc, lw_sc, aw_sc)):
        m_ref[...] = jnp.full(m_ref.shape, NEG, F32)
        l_ref[...] = jnp.zeros(l_ref.shape, F32)
        a_ref[...] = jnp.zeros(a_ref.shape, F32)

    qpos = i * TQ + lax.broadcasted_iota(jnp.int32, (1, TQ), 1)
    cend = lax.broadcasted_iota(jnp.int32, (ncp, 1), 0) * CMP_STRIDE + (CMP_LEN - 1)
    cmask = cend <= qpos
    nsr = sel_sc.shape[1]
    rowj = lax.broadcasted_iota(jnp.int32, (nsr, TQ), 0)
    cur = _div_pow2(qpos, SEL_BLOCK)
    forced = (rowj == 0) | (rowj == cur) | (rowj == cur - 1)
    cand = rowj <= cur
    ioff = (i * TQ).astype(F32)
    for g in range(NSA_GROUPS):
        grow = slice(g * HEAD_DIM, (g + 1) * HEAD_DIM)
        s4 = _dot(ck_ref[0], qz_sc[g])
        psum = jnp.zeros((ncp, TQ), F32)
        for p_ in range(NSA_HPG):
            h = g * NSA_HPG + p_
            s = jnp.where(cmask, s4[:, p_ * TQ:(p_ + 1) * TQ] + cb_ref[h] - NSA_SLOPES[h] * ioff, NEG)
            e = jnp.where(cmask, jnp.exp2(s - jnp.max(s, axis=0, keepdims=True)), 0.0)
            den = jnp.sum(e, axis=0, keepdims=True)
            pn = e / jnp.where(den > 0.0, den, 1.0)
            psum = psum + pn
            p_sc[g, 0:ncp, p_ * TQ:(p_ + 1) * TQ] = pn.astype(BF16)
        oc4 = _dot(cvt_ref[0], p_sc[g, 0:ncp, :])
        for p_ in range(NSA_HPG):
            h = g * NSA_HPG + p_
            oc_sc[h * HEAD_DIM:(h + 1) * HEAD_DIM, :] = oc4[grow, p_ * TQ:(p_ + 1) * TQ]
        ph, plo = _split_bf16(psum)
        imp = _dot(ovl_ref[...], ph) + _dot(ovl_ref[...], plo)
        val = jnp.where(cand, jnp.where(forced, jnp.inf, imp), -jnp.inf)
        rank = _top_rank_rows(val, ns, rowj)
        sel_sc[g] = jnp.where(cand & (rank < SEL_TOPN), 1.0, 0.0)

    spb = TQ // SEL_BLOCK

    def attend(k_blk, vt_blk, tab_ref, off, kb_sel, m_ref, l_ref, a_ref):
        blocks = [slice(j * SEL_BLOCK, (j + 1) * SEL_BLOCK) for j in range(spb)]
        parts = []
        for g in range(NSA_GROUPS):
            s4 = _dot(k_blk, qz_sc[g])
            for p_ in range(NSA_HPG):
                h = g * NSA_HPG + p_
                s = s4[:, p_ * TQ:(p_ + 1) * TQ] + tab_ref[h]
                s_sc[h] = s
                parts.append([jnp.max(s[bl, :].reshape(SEL_BLOCK // 8, 8, TQ), axis=0) for bl in blocks])
        alphas = []
        for h in range(NSA_HEADS):
            g, p_ = divmod(h, NSA_HPG)
            c = NSA_SLOPES[h] * off
            m_old = m_ref[h:h + 1, :]
            if kb_sel is None:
                sels = [None] * spb
                part = functools.reduce(jnp.maximum, parts[h])
            else:
                sels = [sel_sc[g, pl.ds(kb_sel * spb + j, 1), :] > 0.5 for j in range(spb)]
                part = functools.reduce(jnp.maximum, [jnp.where(sl, pt, NEG) for sl, pt in zip(sels, parts[h])])
            m_new = jnp.maximum(m_old, jnp.max(part, axis=0, keepdims=True) + c)
            alpha = jnp.exp2(m_old - m_new)
            psum = jnp.zeros((8, TQ), F32)
            for bl, sl in zip(blocks, sels):
                shift = m_new - c if sl is None else jnp.where(sl, m_new - c, BIG)
                p = jnp.exp2(s_sc[h, bl, :] - shift)
                psum = psum + jnp.sum(p.reshape(SEL_BLOCK // 8, 8, TQ), axis=0)
                p_sc[g, bl, p_ * TQ:(p_ + 1) * TQ] = p.astype(BF16)
            l_ref[h:h + 1, :] = alpha * l_ref[h:h + 1, :] + jnp.sum(psum, axis=0, keepdims=True)
            m_ref[h:h + 1, :] = m_new
            alphas.append(alpha)
        for g in range(NSA_GROUPS):
            grow = slice(g * HEAD_DIM, (g + 1) * HEAD_DIM)
            pv4 = _dot(vt_blk, p_sc[g])
            for p_ in range(NSA_HPG):
                h = g * NSA_HPG + p_
                rows = slice(h * HEAD_DIM, (h + 1) * HEAD_DIM)
                a_ref[rows, :] = alphas[h] * a_ref[rows, :] + pv4[grow, p_ * TQ:(p_ + 1) * TQ]

    def sel_tile(kb, tab_ref):
        start = pl.multiple_of(kb * TQ, TQ)
        attend(sk_ref[0, pl.ds(start, TQ), :], svt_ref[0, :, pl.ds(start, TQ)], tab_ref,
               ((kb - i) * TQ).astype(F32), kb, ms_sc, ls_sc, as_sc)

    def win_tile(back, tab_ref):
        kb = jnp.maximum(i - back, 0)
        start = pl.multiple_of(kb * TQ, TQ)
        attend(wk_ref[0, pl.ds(start, TQ), :], wvt_ref[0, :, pl.ds(start, TQ)], tab_ref,
               -float(back * TQ), None, mw_sc, lw_sc, aw_sc)

    def sel_body(kb, carry):
        sel_tile(kb, bias_ref)
        return carry

    lax.fori_loop(0, i, sel_body, 0)
    sel_tile(i, diag_ref)

    @pl.when(i >= 2)
    def _():
        win_tile(2, edge_ref)

    @pl.when(i >= 1)
    def _():
        win_tile(1, bias_ref)

    win_tile(0, diag_ref)

    ng = ng_ref[0]
    for h in range(NSA_HEADS):
        rows = slice(h * HEAD_DIM, (h + 1) * HEAD_DIM)
        o_s = as_sc[rows, :] / ls_sc[h:h + 1, :]
        o_w = aw_sc[rows, :] / lw_sc[h:h + 1, :]
        oc_sc[rows, :] = (ng[h:h + 1, :] * oc_sc[rows, :] + ng[NSA_HEADS + h:NSA_HEADS + h + 1, :] * o_s
                          + ng[2 * NSA_HEADS + h:2 * NSA_HEADS + h + 1, :] * o_w)
    o_ref[0] = jnp.transpose(oc_sc[...]).astype(BF16)


def _nsa_prompt(qt, ck, cvt, sk, svt, wk, wvt, ngt, ovl, cb, bias, diag, edge):
    b, w, s = qt.shape
    assert WINDOW == 2 * TQ
    ns = s // SEL_BLOCK
    kw = NSA_GROUPS * HEAD_DIM
    full = lambda a: pl.BlockSpec((1,) + a.shape[1:], lambda bi, i: (bi, 0, 0))
    hq = NSA_HPG * TQ
    return pl.pallas_call(
        functools.partial(_nsa_prompt_kernel, ns=ns), grid=(b, s // TQ),
        in_specs=[pl.BlockSpec((1, w, TQ), lambda bi, i: (bi, 0, i)), full(ck), full(cvt), full(sk), full(svt),
                  full(wk), full(wvt), pl.BlockSpec((1, ngt.shape[1], TQ), lambda bi, i: (bi, 0, i)),
                  _const_spec(ovl.shape), _const_spec(cb.shape), _const_spec(bias.shape), _const_spec(diag.shape),
                  _const_spec(edge.shape)],
        out_specs=pl.BlockSpec((1, TQ, w), lambda bi, i: (bi, i, 0)),
        out_shape=jax.ShapeDtypeStruct((b, s, w), BF16),
        scratch_shapes=[pltpu.VMEM((NSA_GROUPS, kw, hq), BF16), pltpu.VMEM((NSA_GROUPS, TQ, hq), BF16),
                        pltpu.VMEM((NSA_HEADS, TQ, TQ), F32),
                        pltpu.VMEM((NSA_GROUPS, ovl.shape[0], TQ), F32), pltpu.VMEM((w, TQ), F32),
                        pltpu.VMEM((NSA_HEADS, TQ), F32), pltpu.VMEM((NSA_HEADS, TQ), F32), pltpu.VMEM((w, TQ), F32),
                        pltpu.VMEM((NSA_HEADS, TQ), F32), pltpu.VMEM((NSA_HEADS, TQ), F32), pltpu.VMEM((w, TQ), F32)],
        compiler_params=_cparams(("arbitrary", "arbitrary")), name="nsa_prompt",
    )(qt, ck, cvt, sk, svt, wk, wvt, ngt, ovl, cb, bias, diag, edge)


def _moba_sample_kernel(pt_ref, q_ref, kn_ref, vn_ref, *rest, past, ds, nbs):
    pg_refs = rest[:PAGES_PER_STEP]
    o_ref, m_sc, l_sc, o_sc, ksum_sc = rest[PAGES_PER_STEP:]
    c = pl.program_id(1)
    nch = pl.num_programs(1)
    rows = MOBA_HEADS * ds
    w = MOBA_HEADS * HEAD_DIM
    ppb = MOBA_BLOCK // PAGE
    row_w = lax.broadcasted_iota(jnp.int32, (rows, w), 0)
    diag = _div_pow2(lax.broadcasted_iota(jnp.int32, (rows, w), 1), HEAD_DIM) == _div_pow2(row_w, ds)
    lane = lax.broadcasted_iota(jnp.int32, (rows, LANES), 1)
    lane_w = lax.broadcasted_iota(jnp.int32, (w, LANES), 1)
    row1 = lax.broadcasted_iota(jnp.int32, (rows, 1), 0)
    qi = _mod_pow2(row1, ds)
    slope = jnp.zeros((rows, 1), F32)
    for h in range(MOBA_HEADS):
        slope = jnp.where(_div_pow2(row1, ds) == h, MOBA_SLOPES[h], slope)
    q8 = q_ref[0].astype(F32)
    qbd = jnp.where(diag, jnp.concatenate([q8] * MOBA_HEADS, axis=0), 0.0).astype(BF16)

    @pl.when(c == 0)
    def _():
        m_sc[...] = jnp.full(m_sc.shape, NEG, F32)
        l_sc[...] = jnp.zeros(l_sc.shape, F32)
        ksum_sc[...] = jnp.zeros(ksum_sc.shape, F32)

    col = lax.broadcasted_iota(jnp.int32, (rows, MOBA_BLOCK), 1)
    for bb in range(PAGES_PER_STEP // ppb):
        blk = c * (PAGES_PER_STEP // ppb) + bb
        kt = jnp.concatenate([pg_refs[bb * ppb + j][0, 0] for j in range(ppb)], axis=1)
        vt = jnp.concatenate([pg_refs[bb * ppb + j][0, 1] for j in range(ppb)], axis=1).astype(BF16)
        ksum_sc[...] = jnp.where(lane_w == blk, jnp.sum(kt, axis=1, keepdims=True), ksum_sc[...])
        kdist = (col - qi + (blk * MOBA_BLOCK - past)).astype(F32)
        s = _dot(qbd, kt.astype(BF16)) + slope * kdist
        m_b = jnp.max(s, axis=1, keepdims=True)
        p = jnp.exp2(s - m_b)
        l_b = jnp.sum(p, axis=1, keepdims=True)
        o_sc[blk] = jnp.where(diag, _dot_nt(p.astype(BF16), vt), 0.0)
        m_sc[...] = jnp.where(lane == blk, m_b, m_sc[...])
        l_sc[...] = jnp.where(lane == blk, l_b, l_sc[...])

    @pl.when(c == nch - 1)
    def _():
        mh, ml = _split_bf16(ksum_sc[...] * (1.0 / MOBA_BLOCK))
        ss = _dot(qbd, mh) + _dot(qbd, ml)
        own = past // MOBA_BLOCK
        cand = lane < own
        rank = _top_rank(jnp.where(cand, ss, -jnp.inf), nbs, lane)
        selb = cand & (rank < MOBA_TOPK)
        m_all = m_sc[...]
        m_past = jnp.max(jnp.where(selb, m_all, NEG), axis=1, keepdims=True)
        zpad = jnp.zeros((LANES - ds, w), F32)
        k_new = jnp.concatenate([kn_ref[0], zpad], axis=0).astype(BF16)
        v_new = jnp.concatenate([vn_ref[0], zpad], axis=0).astype(BF16)
        s_own = _dot_nt(qbd, k_new) + slope * (lane - qi).astype(F32)
        s_own = jnp.where(lane <= qi, s_own, NEG)
        m_tot = jnp.maximum(m_past, jnp.max(s_own, axis=1, keepdims=True))
        wgt = jnp.where(selb, jnp.exp2(m_all - m_tot), 0.0)
        p_own = jnp.exp2(s_own - m_tot)
        den = jnp.sum(wgt * l_sc[...], axis=1, keepdims=True) + jnp.sum(p_own, axis=1, keepdims=True)
        out = jnp.where(diag, _dot(p_own.astype(BF16), v_new), 0.0)
        for b2 in range(nbs):
            out = out + wgt[:, b2:b2 + 1] * o_sc[b2]
        out = out / den
        o8 = out[0:ds, :]
        for h in range(1, MOBA_HEADS):
            o8 = o8 + out[h * ds:(h + 1) * ds, :]
        o_ref[0] = o8.astype(BF16)


def _moba_sample(page_table, q, k_new, v_new, pool_t):
    db, ds, w = q.shape
    n_pages = page_table.shape[1]
    past = n_pages * PAGE
    nbs = past // MOBA_BLOCK
    assert n_pages % PAGES_PER_STEP == 0 and past % MOBA_BLOCK == 0 and nbs <= LANES and ds == 8
    tok = pl.BlockSpec((1, ds, w), lambda bi, c, pt: (bi, 0, 0))
    pg_specs = [pl.BlockSpec((1, 2, w, PAGE), lambda bi, c, pt, j=j: (pt[bi, c * PAGES_PER_STEP + j], 0, 0, 0))
                for j in range(PAGES_PER_STEP)]
    gs = pltpu.PrefetchScalarGridSpec(
        num_scalar_prefetch=1, grid=(db, n_pages // PAGES_PER_STEP),
        in_specs=[tok, tok, tok] + pg_specs, out_specs=tok,
        scratch_shapes=[pltpu.VMEM((MOBA_HEADS * ds, LANES), F32), pltpu.VMEM((MOBA_HEADS * ds, LANES), F32),
                        pltpu.VMEM((nbs, MOBA_HEADS * ds, w), F32), pltpu.VMEM((w, LANES), F32)])
    return pl.pallas_call(
        functools.partial(_moba_sample_kernel, past=past, ds=ds, nbs=nbs), grid_spec=gs,
        out_shape=jax.ShapeDtypeStruct((db, ds, w), BF16),
        compiler_params=_cparams(("arbitrary", "arbitrary")), name="moba_sample",
    )(page_table, q, k_new, v_new, *([pool_t] * PAGES_PER_STEP))


def _nsa_sample_kernel(pt_ref, qg_ref, ck_ref, cv_ref, skn_ref, wkn_ref, wst_ref, ng_ref, ovl_ref, exp_ref,
                       *rest, past, ds, n_pages, ns):
    pg_refs = rest[:n_pages]
    o_ref = rest[n_pages]
    rows = NSA_HEADS * ds
    grows = NSA_HPG * ds
    kw = NSA_GROUPS * HEAD_DIM
    ncp = ck_ref.shape[1]
    nsp = ovl_ref.shape[1]
    wb = wst_ref.shape[2]
    row1 = lax.broadcasted_iota(jnp.int32, (rows, 1), 0)
    qi = _mod_pow2(row1, ds)
    qpos = past + qi
    slope = jnp.zeros((rows, 1), F32)
    for h in range(NSA_HEADS):
        slope = jnp.where(_div_pow2(row1, ds) == h, NSA_SLOPES[h], slope)
    lane = lax.broadcasted_iota(jnp.int32, (rows, LANES), 1)
    grp_half = _div_pow2(lane, HEAD_DIM) == _div_pow2(row1, grows)
    q_grp = qg_ref[0]

    def softmax_parts(s_list, m_list):
        m = None
        for s, mk in zip(s_list, m_list):
            mm = jnp.max(jnp.where(mk, s, NEG), axis=1, keepdims=True)
            m = mm if m is None else jnp.maximum(m, mm)
        ps = [jnp.where(mk, jnp.exp2(jnp.where(mk, s, NEG) - m), 0.0) for s, mk in zip(s_list, m_list)]
        den = sum(jnp.sum(p, axis=1, keepdims=True) for p in ps)
        return ps, jnp.where(den > 0.0, den, 1.0)

    def own_group(o):
        return jnp.where(grp_half, o, pltpu.roll(o, HEAD_DIM, 1))

    cend = lax.broadcasted_iota(jnp.int32, (rows, ncp), 1) * CMP_STRIDE + (CMP_LEN - 1)
    cmask = cend <= qpos
    s_c = _dot_nt(q_grp, ck_ref[0]) + slope * (cend - qpos).astype(F32)
    (e,), den = softmax_parts([s_c], [cmask])
    pn = e / den
    o_c = own_group(_dot(pn.astype(BF16), cv_ref[0]))
    lane_s = lax.broadcasted_iota(jnp.int32, (ds, nsp), 1)
    cur = _div_pow2(past + lax.broadcasted_iota(jnp.int32, (ds, 1), 0), SEL_BLOCK)
    forced = (lane_s == 0) | (lane_s == cur) | (lane_s == cur - 1)
    cand = lane_s <= cur
    sel_parts = []
    for g in range(NSA_GROUPS):
        psum = pn[g * grows:g * grows + ds]
        for p_ in range(1, NSA_HPG):
            psum = psum + pn[g * grows + p_ * ds:g * grows + (p_ + 1) * ds]
        ph, plo = _split_bf16(psum)
        imp = _dot(ph, ovl_ref[...]) + _dot(plo, ovl_ref[...])
        val = jnp.where(cand, jnp.where(forced, jnp.inf, imp), -jnp.inf)
        rank = _top_rank(val, ns, lane_s)
        sel_g = jnp.where(cand & (rank < SEL_TOPN), 1.0, 0.0)
        sel_parts.append(jnp.concatenate([sel_g] * NSA_HPG, axis=0))
    sel_f32 = jnp.concatenate(sel_parts, axis=0)
    sel_rows = sel_f32.astype(BF16)

    kt_all = jnp.concatenate([r[0, 0:kw, :] for r in pg_refs], axis=1).astype(BF16)
    vt_all = jnp.concatenate([r[0, kw:2 * kw, :] for r in pg_refs], axis=1).astype(BF16)
    kpos = lax.broadcasted_iota(jnp.int32, (rows, past), 1)
    s_past = _dot(q_grp, kt_all) + slope * (kpos - qpos).astype(F32)
    m_past = _dot(sel_rows, exp_ref[...]) > 0.5
    zpad = jnp.zeros((LANES - ds, 2 * kw), F32)
    skn = jnp.concatenate([skn_ref[0], zpad], axis=0)
    s_new = _dot_nt(q_grp, skn[:, 0:kw].astype(BF16)) + slope * (lane - qi).astype(F32)
    new_blk = past // SEL_BLOCK
    m_new = (lane <= qi) & (sel_f32[:, new_blk:new_blk + 1] > 0.5)
    (p_past, p_new), den = softmax_parts([s_past, s_new], [m_past, m_new])
    o_s = _dot_nt(p_past.astype(BF16), vt_all) + _dot(p_new.astype(BF16), skn[:, kw:2 * kw].astype(BF16))
    o_s = own_group(o_s / den)

    jst = lax.broadcasted_iota(jnp.int32, (rows, wb), 1)
    dist_st = qpos - (past - wb + jst)
    s_st = _dot(q_grp, wst_ref[0, 0:kw, :].astype(BF16)) - slope * dist_st.astype(F32)
    m_st = (dist_st >= 0) & (dist_st <= WINDOW)
    wkn = jnp.concatenate([wkn_ref[0], zpad], axis=0)
    s_wn = _dot_nt(q_grp, wkn[:, 0:kw].astype(BF16)) + slope * (lane - qi).astype(F32)
    m_wn = lane <= qi
    (p_st, p_wn), den = softmax_parts([s_st, s_wn], [m_st, m_wn])
    o_w = _dot_nt(p_st.astype(BF16), wst_ref[0, kw:2 * kw, :].astype(BF16)) \
        + _dot(p_wn.astype(BF16), wkn[:, kw:2 * kw].astype(BF16))
    o_w = own_group(o_w / den)

    ng = ng_ref[0]
    gate = lambda br: jnp.concatenate([ng[:, br * NSA_HEADS + h:br * NSA_HEADS + h + 1] for h in range(NSA_HEADS)],
                                      axis=0)
    o = gate(0) * o_c + gate(1) * o_s + gate(2) * o_w
    lane8 = lax.broadcasted_iota(jnp.int32, (ds, LANES), 1)
    for pair in range(NSA_HEADS // 2):
        o_even = o[(2 * pair) * ds:(2 * pair + 1) * ds]
        o_odd = o[(2 * pair + 1) * ds:(2 * pair + 2) * ds]
        o_ref[0, :, pair * LANES:(pair + 1) * LANES] = jnp.where(lane8 < HEAD_DIM, o_even, o_odd).astype(BF16)


def _nsa_sample(page_table, q_grp, ck, cv, sk_new, wk_new, wstate_t, ng, ovl, expand, pool_t):
    db, rows, _ = q_grp.shape
    ds = rows // NSA_HEADS
    n_pages = page_table.shape[1]
    past = n_pages * PAGE
    ns = -(-(past + ds) // SEL_BLOCK)
    fw = pool_t.shape[1]
    per_b = lambda a: pl.BlockSpec((1,) + a.shape[1:], lambda bi, pt: (bi, 0, 0))
    cst = lambda a: pl.BlockSpec(a.shape, lambda bi, pt, nd=a.ndim: (0,) * nd)
    pg_specs = [pl.BlockSpec((1, fw, PAGE), lambda bi, pt, j=j: (pt[bi, j], 0, 0)) for j in range(n_pages)]
    gs = pltpu.PrefetchScalarGridSpec(
        num_scalar_prefetch=1, grid=(db,),
        in_specs=[per_b(q_grp), per_b(ck), per_b(cv), per_b(sk_new), per_b(wk_new), per_b(wstate_t),
                  per_b(ng), cst(ovl), cst(expand)] + pg_specs,
        out_specs=pl.BlockSpec((1, ds, NSA_HEADS * HEAD_DIM), lambda bi, pt: (bi, 0, 0)))
    return pl.pallas_call(
        functools.partial(_nsa_sample_kernel, past=past, ds=ds, n_pages=n_pages, ns=ns), grid_spec=gs,
        out_shape=jax.ShapeDtypeStruct((db, ds, NSA_HEADS * HEAD_DIM), BF16),
        compiler_params=_cparams(("arbitrary",)), name="nsa_sample",
    )(page_table, q_grp, ck, cv, sk_new, wk_new, wstate_t, ng, ovl, expand, *([pool_t] * n_pages))


def _merge_kernel(x_ref, om_ref, on_ref, mg_ref, wa_ref, wb_ref, wo_ref, h_ref):
    d = x_ref.shape[1]
    a = _dot(om_ref[...], wa_ref[...])
    b = _dot(on_ref[...], wb_ref[...])
    merged = mg_ref[:, 0:d] * a + mg_ref[:, d:2 * d] * b
    h_ref[...] = x_ref[...] + _dot(merged.astype(BF16), wo_ref[...])


def _merge(x2d, o_m, o_n, mg, wa, wb, wo):
    t, d = x2d.shape
    row = lambda w: pl.BlockSpec((TM, w), lambda i: (i, 0))
    return pl.pallas_call(
        _merge_kernel, grid=(t // TM,),
        in_specs=[row(d), row(o_m.shape[1]), row(o_n.shape[1]), row(2 * d),
                  _const_spec(wa.shape), _const_spec(wb.shape), _const_spec(wo.shape)],
        out_specs=row(d), out_shape=jax.ShapeDtypeStruct((t, d), F32),
        compiler_params=_cparams(("arbitrary",)), name="merge",
    )(x2d, o_m, o_n, mg, wa, wb, wo)


def _ffn_kernel(h_ref, *rest, seq_len, per_seq_state):
    if per_seq_state:
        st_ref, gf_ref, win_ref, cw_ref, cb_ref, wd_ref, y_ref, cs_ref, carry_sc = rest
    else:
        st0_ref, st1_ref, gf_ref, win_ref, cw_ref, cb_ref, wd_ref, y_ref, g_ref = rest
    f = cw_ref.shape[1]
    h = h_ref[...]
    hn = (h * lax.rsqrt(jnp.mean(h * h, axis=-1, keepdims=True) + RMS_EPS) * gf_ref[...]).astype(BF16)
    gu = _dot(hn, win_ref[...])
    g = gu[:, 0:f]
    u = gu[:, f:2 * f]
    idx = _mod_pow2(lax.broadcasted_iota(jnp.int32, (TM, 1), 0), seq_len)
    if per_seq_state:
        @pl.when(pl.program_id(1) == 0)
        def _():
            carry_sc[0:CONV_W - 1, :] = st_ref[0]
        st0 = carry_sc[0:1, :]
        st1 = carry_sc[1:2, :]
    else:
        st0 = st0_ref[...]
        st1 = st1_ref[...]
    g1 = pltpu.roll(g, 1, 0)
    g2 = pltpu.roll(g, 2, 0)
    prev1 = jnp.where(idx == 0, st1, g1)
    prev2 = jnp.where(idx == 0, st0, jnp.where(idx == 1, st1, g2))
    gc = cb_ref[...] + prev2 * cw_ref[0:1, :] + prev1 * cw_ref[1:2, :] + g * cw_ref[2:3, :]
    act = (jax.nn.silu(gc) * u).astype(BF16)
    y_ref[...] = h + _dot(act, wd_ref[...])
    if per_seq_state:
        carry_sc[0:CONV_W - 1, :] = g[TM - (CONV_W - 1):TM, :]
        cs_ref[0] = g[TM - (CONV_W - 1):TM, :]
    else:
        g_ref[...] = g


def _ffn_prompt(h2d, n, conv_state, gf, win, cw, cb, wd):
    t, d = h2d.shape
    f = cw.shape[1]
    nt = (t // n) // TM
    row = pl.BlockSpec((TM, d), lambda bi, i: (bi * nt + i, 0))
    st = pl.BlockSpec((1, CONV_W - 1, f), lambda bi, i: (bi, 0, 0))
    cst = lambda a: pl.BlockSpec(a.shape, lambda bi, i, nd=a.ndim: (0,) * nd)
    return pl.pallas_call(
        functools.partial(_ffn_kernel, seq_len=TM, per_seq_state=True), grid=(n, nt),
        in_specs=[row, st, cst(gf), cst(win), cst(cw), cst(cb), cst(wd)],
        out_specs=(row, st),
        out_shape=(jax.ShapeDtypeStruct((t, d), F32), jax.ShapeDtypeStruct((n, CONV_W - 1, f), F32)),
        scratch_shapes=[pltpu.VMEM((8, f), F32)],
        compiler_params=_cparams(("arbitrary", "arbitrary")), name="ffn_prompt",
    )(h2d, conv_state, gf, win, cw, cb, wd)


def _ffn_sample(h2d, seq_len, st0, st1, gf, win, cw, cb, wd):
    t, d = h2d.shape
    f = cw.shape[1]
    row = lambda w: pl.BlockSpec((TM, w), lambda i: (i, 0))
    return pl.pallas_call(
        functools.partial(_ffn_kernel, seq_len=seq_len, per_seq_state=False), grid=(t // TM,),
        in_specs=[row(d), row(f), row(f), _const_spec(gf.shape), _const_spec(win.shape), _const_spec(cw.shape),
                  _const_spec(cb.shape), _const_spec(wd.shape)],
        out_specs=(row(d), row(f)),
        out_shape=(jax.ShapeDtypeStruct((t, d), F32), jax.ShapeDtypeStruct((t, f), F32)),
        compiler_params=_cparams(("arbitrary",)), name="ffn_sample",
    )(h2d, st0, st1, gf, win, cw, cb, wd)


def _block_diag_ones(n, blk):
    i = np.arange(n)
    return jnp.asarray((i[:, None] // blk == i[None, :] // blk).astype(np.float32), BF16)


def _overlap_matrix(ncp, nc, ns, nsp):
    c0 = np.arange(ncp) * CMP_STRIDE
    j0 = np.arange(nsp) * SEL_BLOCK
    m = (c0[:, None] < j0[None, :] + SEL_BLOCK) & (c0[:, None] + CMP_LEN > j0[None, :])
    m &= (np.arange(ncp)[:, None] < nc) & (np.arange(nsp)[None, :] < ns)
    return m.astype(np.float32)


def _layer_weights(l, attn_norm_g, w_in, qk_norm_g, nsa_cmp_pe, nsa_cmp_w1, nsa_cmp_b1, nsa_cmp_w2, nsa_cmp_b2,
                   w_branch_moba, w_branch_nsa, w_out, ffn_norm_g, w_ffn_in, ffn_conv_w, ffn_conv_b, w_ffn_down):
    d = w_in.shape[1]
    wi = w_in[l]
    kvw = KW
    o_mg = R_END + 3 * NSA_HEADS
    w1t = jnp.transpose(wi[:, :R_END]).astype(BF16)
    wngt = jnp.pad(jnp.transpose(wi[:, R_END:o_mg]), ((0, NG_ROWS - 3 * NSA_HEADS), (0, 0))).astype(BF16)
    wmg = wi[:, o_mg:o_mg + 2 * d].astype(BF16)
    g = qk_norm_g[l]
    ones = lambda n: jnp.ones((n,), F32)
    gcol = jnp.concatenate([jnp.tile(g[0], MOBA_HEADS), jnp.tile(g[1], MOBA_HEADS), ones(MW), jnp.tile(g[2], NSA_HEADS),
                            ones(2 * KW), jnp.tile(g[4], NSA_GROUPS), ones(KW), jnp.tile(g[5], NSA_GROUPS),
                            ones(KW)])[:, None]
    r = CMP_LEN // CMP_STRIDE
    w1c = nsa_cmp_w1[l]
    w1r = w1c.reshape(2, r, CMP_STRIDE, HEAD_DIM, CMP_HIDDEN)
    nkvg = 2 * NSA_GROUPS
    wfull = jnp.zeros((CMP_STRIDE, nkvg, HEAD_DIM, nkvg, r, CMP_HIDDEN), F32)
    for kvg in range(nkvg):
        wfull = wfull.at[:, kvg, :, kvg].set(jnp.transpose(w1r[kvg // NSA_GROUPS], (1, 2, 0, 3)))
    wfull = wfull.reshape(CMP_STRIDE * nkvg * HEAD_DIM, nkvg * r * CMP_HIDDEN).astype(BF16)
    w2bd = jnp.zeros((nkvg, CMP_HIDDEN, nkvg, HEAD_DIM), F32)
    for kvg in range(nkvg):
        w2bd = w2bd.at[kvg, :, kvg].set(nsa_cmp_w2[l][kvg // NSA_GROUPS])
    w2bd = w2bd.reshape(nkvg * CMP_HIDDEN, nkvg * HEAD_DIM).astype(BF16)
    b2 = nsa_cmp_b2[l]
    cw = dict(
        wfull=wfull,
        pe=jnp.broadcast_to(nsa_cmp_pe[l].reshape(2, 1, CMP_LEN * HEAD_DIM), (2, 8, CMP_LEN * HEAD_DIM)).astype(BF16),
        w1=w1c.astype(BF16), b1=nsa_cmp_b1[l], w2bd=w2bd,
        b2row=jnp.concatenate([jnp.tile(b2[0], NSA_GROUPS), jnp.tile(b2[1], NSA_GROUPS)])[None, :],
        gk=jnp.tile(g[3], NSA_GROUPS)[None, :], bd=_block_diag_ones(kvw, HEAD_DIM))
    return dict(
        g_attn=attn_norm_g[l][None, :], w1t=w1t, wngt=wngt, wmg=wmg, gcol=gcol, cw=cw,
        wa=w_branch_moba[l].astype(BF16), wb=w_branch_nsa[l].astype(BF16), wo=w_out[l].astype(BF16),
        gf=ffn_norm_g[l][None, :], win=w_ffn_in[l].astype(BF16), cwt=ffn_conv_w[l], cb=ffn_conv_b[l][None, :],
        wd=w_ffn_down[l].astype(BF16))


def _prompt_layer(x, wts):
    b, s, d = x.shape
    assert s % (CMP_STRIDE * LANES) == 0 and s % TQ == 0
    t = b * s
    x2d = x.reshape(t, d)
    assert TM == MOBA_BLOCK
    pr = _inproj(x, wts["g_attn"], wts["w1t"], wts["wngt"], wts["wmg"], wts["gcol"])
    mg = pr["mg"]
    nb = s // MOBA_BLOCK
    nbp = -(-nb // 16) * 16
    means = jnp.pad(pr["ksum"].reshape(b, nb, MW) * (1.0 / MOBA_BLOCK), ((0, 0), (0, nbp - nb), (0, 0)))
    m_bias, m_diag, _ = _alibi_tables(MOBA_SLOPES)
    o_m = _moba_prompt(pr["mqt"], pr["mk"], pr["mvt"], means, m_bias, m_diag)

    ncp = s // CMP_STRIDE
    ck, cv = _compress_prompt(pr["ck"], pr["cv"], wts["cw"])
    ns = s // SEL_BLOCK
    nsr = -(-ns // 8) * 8
    ovl_t = jnp.asarray(_overlap_matrix(ncp, ncp - CMP_LEN // CMP_STRIDE + 1, ns, nsr).T, BF16)
    cb = np.asarray(NSA_SLOPES, np.float32)[:, None, None] * (
        (np.arange(ncp) * CMP_STRIDE + CMP_LEN - 1)[None, :, None] - np.arange(TQ)[None, None, :]).astype(np.float32)
    n_bias, n_diag, n_edge = _alibi_tables(NSA_SLOPES)
    o_n = _nsa_prompt(pr["nqt"], ck, jnp.swapaxes(cv, 1, 2), pr["sk"], pr["svt"], pr["wk"], pr["wvt"], pr["ngt"],
                      ovl_t, jnp.asarray(cb), n_bias, n_diag, n_edge)

    h2d = _merge(x2d, o_m.reshape(t, -1), o_n.reshape(t, -1), mg, wts["wa"], wts["wb"], wts["wo"])
    f = wts["cwt"].shape[1]
    y2d, conv = _ffn_prompt(h2d, b, jnp.zeros((b, CONV_W - 1, f), F32), wts["gf"], wts["win"], wts["cwt"],
                            wts["cb"], wts["wd"])
    wrows = min(WINDOW, s)
    rows_view = lambda a, heads: jnp.transpose(a.reshape(b, 2, heads, HEAD_DIM, a.shape[2]), (0, 4, 1, 2, 3))
    outs = (rows_view(pr["mkvt"], MOBA_HEADS), rows_view(pr["ckvt"], NSA_GROUPS), rows_view(pr["skvt"], NSA_GROUPS),
            rows_view(pr["wkvt"][:, :, s - wrows:], NSA_GROUPS), conv)
    return y2d.reshape(b, s, d), outs


def _pages_t(cache):
    dp, npool = cache.shape[0], cache.shape[1]
    return jnp.transpose(cache, (0, 1, 3, 4, 5, 2)).reshape(dp * npool, -1, PAGE)


def _sample_layer(x, wts, pool_moba_t, pool_cmp_t, pool_sel_t, win_state, conv_state, page_table):
    db, ds, d = x.shape
    t = db * ds
    assert t % TM == 0 and ds == 8
    n_pages = page_table.shape[1]
    past = n_pages * PAGE
    x2d = x.reshape(t, d)
    pr = _inproj(x2d[None], wts["g_attn"], wts["w1t"], wts["wngt"], wts["wmg"], wts["gcol"])
    rows = lambda a: jnp.transpose(a[0])
    mq, mkv, nq, ckv, skv, wkv = (rows(pr[k]) for k in ("mqt", "mkvt", "nqt", "ckvt", "skvt", "wkvt"))
    ng = jnp.pad(rows(pr["ngt"]), ((0, 0), (0, LANES - NG_ROWS)))
    mg = pr["mg"]
    mw = MW
    kw = KW
    o_m = _moba_sample(page_table, mq.reshape(db, ds, mw), mkv[:, 0:mw].reshape(db, ds, mw),
                       mkv[:, mw:2 * mw].reshape(db, ds, mw), pool_moba_t.reshape(-1, 2, mw, PAGE))

    assert (past + ds) // CMP_STRIDE == past // CMP_STRIDE
    ncp = past // CMP_STRIDE
    ck, cv = _compress_sample(page_table, pool_cmp_t, wts["cw"])

    nqf = nq.astype(F32).reshape(db, ds, NSA_HEADS, HEAD_DIM).transpose(0, 2, 1, 3)
    zero = jnp.zeros_like(nqf)
    hh = jnp.arange(NSA_HEADS)[None, :, None, None]
    q_grp = jnp.where(hh // NSA_HPG == 0, jnp.concatenate([nqf, zero], -1), jnp.concatenate([zero, nqf], -1))
    q_grp = q_grp.reshape(db, NSA_HEADS * ds, LANES).astype(BF16)
    ns = -(-(past + ds) // SEL_BLOCK)
    nsp = -(-ns // LANES) * LANES
    ovl = jnp.asarray(_overlap_matrix(ncp, ncp - CMP_LEN // CMP_STRIDE + 1, ns, nsp), BF16)
    expand = jnp.asarray((np.arange(nsp)[:, None] == np.arange(past)[None, :] // SEL_BLOCK).astype(np.float32), BF16)
    wb = win_state.shape[1]
    wst_t = jnp.transpose(win_state, (0, 2, 3, 4, 1)).reshape(db, 2 * kw, wb)
    o_n = _nsa_sample(page_table, q_grp, ck, cv, skv.reshape(db, ds, -1), wkv.reshape(db, ds, -1), wst_t,
                      ng.reshape(db, ds, LANES), ovl, expand, pool_sel_t)

    h2d = _merge(x2d, o_m.reshape(t, -1), o_n.reshape(t, -1), mg, wts["wa"], wts["wb"], wts["wo"])
    st0 = jnp.repeat(conv_state[:, 0], ds, axis=0)
    st1 = jnp.repeat(conv_state[:, 1], ds, axis=0)
    y2d, g2d = _ffn_sample(h2d, ds, st0, st1, wts["gf"], wts["win"], wts["cwt"], wts["cb"], wts["wd"])
    gp = jnp.concatenate([conv_state, g2d.reshape(db, ds, -1)], axis=1)
    wcat = jnp.concatenate([win_state, wkv.reshape(db, ds, 2, NSA_GROUPS, HEAD_DIM)], axis=1)
    wrows = min(WINDOW, wcat.shape[1])
    outs = (mkv.reshape(db, ds, 2, MOBA_HEADS, HEAD_DIM), ckv.reshape(db, ds, 2, NSA_GROUPS, HEAD_DIM),
            skv.reshape(db, ds, 2, NSA_GROUPS, HEAD_DIM), wcat[:, wcat.shape[1] - wrows:], gp[:, ds:])
    return y2d.reshape(db, ds, d), outs


def kernel(x_prompt, x_sample, cache_moba_kv, cache_nsa_cmp_kv, cache_nsa_sel_kv, state_nsa_win_kv, state_ffn_conv,
           page_table, attn_norm_g, w_in, qk_norm_g, nsa_cmp_pe, nsa_cmp_w1, nsa_cmp_b1, nsa_cmp_w2, nsa_cmp_b2,
           w_branch_moba, w_branch_nsa, w_out, ffn_norm_g, w_ffn_in, ffn_conv_w, ffn_conv_b, w_ffn_down):
    depth = w_in.shape[0]
    n_pool = cache_moba_kv.shape[1]
    pools = [_pages_t(c) for c in (cache_moba_kv, cache_nsa_cmp_kv, cache_nsa_sel_kv)]
    hp, hs = x_prompt, x_sample
    new = [[] for _ in range(10)]
    for l in range(depth):
        wts = _layer_weights(l, attn_norm_g, w_in, qk_norm_g, nsa_cmp_pe, nsa_cmp_w1, nsa_cmp_b1, nsa_cmp_w2,
                             nsa_cmp_b2, w_branch_moba, w_branch_nsa, w_out, ffn_norm_g, w_ffn_in, ffn_conv_w,
                             ffn_conv_b, w_ffn_down)
        hp, outs_p = _prompt_layer(hp, wts)
        hs, outs_s = _sample_layer(hs, wts, *pools, state_nsa_win_kv[l], state_ffn_conv[l],
                                   page_table + l * n_pool)
        for lst, arr in zip(new[:5], outs_p):
            lst.append(arr)
        for lst, arr in zip(new[5:], outs_s):
            lst.append(arr)
    st = [jnp.stack(v) for v in new]
    return (hp, hs, st[0], st[5], st[1], st[6], st[2], st[7], st[3], st[8], st[4], st[9])
```

```python
import functools

import numpy as np
import jax
import jax.numpy as jnp
from jax import lax
from jax.experimental import pallas as pl
from jax.experimental.pallas import tpu as pltpu

F32 = jnp.float32
BF16 = jnp.bfloat16

HEAD_DIM = 64
MOBA_HEADS = 8
MOBA_BLOCK = 256
MOBA_TOPK = 3
NSA_HEADS = 8
NSA_GROUPS = 2
NSA_HPG = NSA_HEADS // NSA_GROUPS
CMP_LEN = 32
CMP_STRIDE = 16
CMP_HIDDEN = 128
SEL_BLOCK = 64
SEL_TOPN = 8
WINDOW = 512
CONV_W = 3
PAGE = 128
RMS_EPS = 1e-6
NEG = -1e30
BIG = 1e30
LOG2E = 1.4426950408889634
QK_SCALE = HEAD_DIM ** -0.5 * LOG2E

LANES = 128
TQ = 256
FFN_TM = 512
KEY_CHUNK = 64
TM = 256
PAGES_PER_STEP = 16
VMEM_LIMIT = 56 * 1024 * 1024

MOBA_SLOPES = tuple(float(2.0 ** (-8.0 * (h + 1) / MOBA_HEADS)) * LOG2E for h in range(MOBA_HEADS))
NSA_SLOPES = tuple(float(2.0 ** (-8.0 * (h + 1) / NSA_HEADS)) * LOG2E for h in range(NSA_HEADS))


def _dot(a, b):
    return jnp.dot(a, b, preferred_element_type=F32)


def _dot_nt(a, b):
    return lax.dot_general(a, b, (((1,), (1,)), ((), ())), preferred_element_type=F32)


def _div_pow2(x, n):
    assert n & (n - 1) == 0
    return lax.shift_right_logical(x, jnp.int32(n.bit_length() - 1))


def _mod_pow2(x, n):
    assert n & (n - 1) == 0
    return x & (n - 1)


def _split_bf16(x):
    hi = x.astype(BF16)
    lo = (x - hi.astype(F32)).astype(BF16)
    return hi, lo


def _cparams(sem):
    return pltpu.CompilerParams(dimension_semantics=sem, vmem_limit_bytes=VMEM_LIMIT)


def _const_spec(shape):
    nd = len(shape)
    return pl.BlockSpec(shape, lambda *_: (0,) * nd)


def _seg_rms(y, bd, gain):
    wb = bd.shape[0]
    outs = []
    for c in range(y.shape[1] // wb):
        s = y[:, c * wb:(c + 1) * wb]
        ss = _dot((s * s).astype(BF16), bd)
        outs.append(s * lax.rsqrt(ss * (1.0 / HEAD_DIM) + RMS_EPS))
    r = outs[0] if len(outs) == 1 else jnp.concatenate(outs, axis=1)
    return r * gain


def _top_rank(val, n, lane):
    rank = jnp.zeros(val.shape, F32)
    for j2 in range(n):
        c = val[:, j2:j2 + 1]
        before = (c > val) | ((c == val) & (lane > j2))
        rank = rank + jnp.where(before, 1.0, 0.0)
    return rank


def _top_rank_rows(val, n, rowid):
    rank = jnp.zeros(val.shape, F32)
    for j2 in range(n):
        c = val[j2:j2 + 1, :]
        before = (c > val) | ((c == val) & (rowid > j2))
        rank = rank + jnp.where(before, 1.0, 0.0)
    return rank


MW = MOBA_HEADS * HEAD_DIM
KW = NSA_GROUPS * HEAD_DIM
R_MQ, R_MK, R_MV, R_NQ = 0, MW, 2 * MW, 3 * MW
R_CK = 4 * MW
R_CV, R_SK, R_SV, R_WK, R_WV, R_END = (R_CK + j * KW for j in range(1, 7))
NG_ROWS = -(-3 * NSA_HEADS // 16) * 16


def _inproj_kernel(x_ref, g_ref, w1t_ref, wngt_ref, wmg_ref, gcol_ref,
                   mqt_ref, mkvt_ref, mk_ref, mvt_ref, nqt_ref, ckvt_ref, ck_ref, cv_ref, skvt_ref, sk_ref, svt_ref,
                   wkvt_ref, wk_ref, wvt_ref, ngt_ref, mg_ref, ksum_ref):
    x = x_ref[0]
    xn = x * lax.rsqrt(jnp.mean(x * x, axis=-1, keepdims=True) + RMS_EPS) * g_ref[...]
    mg_ref[...] = jax.nn.sigmoid(_dot(xn.astype(BF16), wmg_ref[...])).astype(BF16)
    xnt = jnp.transpose(xn).astype(BF16)
    ngt_ref[0] = jax.nn.sigmoid(_dot(wngt_ref[...], xnt))
    yt = _dot(w1t_ref[...], xnt)

    def normed(r0, n_heads):
        outs = []
        for j in range(n_heads):
            seg = yt[r0 + j * HEAD_DIM:r0 + (j + 1) * HEAD_DIM, :]
            ss = jnp.sum(seg * seg, axis=0, keepdims=True)
            outs.append(seg * lax.rsqrt(ss * (1.0 / HEAD_DIM) + RMS_EPS)
                        * gcol_ref[r0 + j * HEAD_DIM:r0 + (j + 1) * HEAD_DIM, :])
        return jnp.concatenate(outs, axis=0)

    mqt_ref[0] = (normed(R_MQ, MOBA_HEADS) * QK_SCALE).astype(BF16)
    mkt = normed(R_MK, MOBA_HEADS)
    mkvt_ref[0, 0:MW, :] = mkt
    mkvt_ref[0, MW:2 * MW, :] = yt[R_MV:R_NQ, :]
    mk = jnp.transpose(mkt)
    mk_ref[0] = mk.astype(BF16)
    ksum_ref[0] = jnp.sum(mk, axis=0, keepdims=True)
    mvt_ref[0] = yt[R_MV:R_NQ, :].astype(BF16)
    nqt_ref[0] = (normed(R_NQ, NSA_HEADS) * QK_SCALE).astype(BF16)
    ckvt_ref[0] = yt[R_CK:R_SK, :]
    ck_ref[0] = jnp.transpose(yt[R_CK:R_CV, :])
    cv_ref[0] = jnp.transpose(yt[R_CV:R_SK, :])
    skt = normed(R_SK, NSA_GROUPS)
    skvt_ref[0, 0:KW, :] = skt
    skvt_ref[0, KW:2 * KW, :] = yt[R_SV:R_WK, :]
    sk_ref[0] = jnp.transpose(skt).astype(BF16)
    svt_ref[0] = yt[R_SV:R_WK, :].astype(BF16)
    wkt = normed(R_WK, NSA_GROUPS)
    wkvt_ref[0, 0:KW, :] = wkt
    wkvt_ref[0, KW:2 * KW, :] = yt[R_WV:R_END, :]
    wk_ref[0] = jnp.transpose(wkt).astype(BF16)
    wvt_ref[0] = yt[R_WV:R_END, :].astype(BF16)


def _inproj(x, g_attn, w1t, wngt, wmg, gcol):
    n, length, d = x.shape
    nt = length // TM
    ft = lambda rows: pl.BlockSpec((1, rows, TM), lambda b, i: (b, 0, i))
    rw = lambda w: pl.BlockSpec((1, TM, w), lambda b, i: (b, i, 0))
    fts = lambda rows, dt: jax.ShapeDtypeStruct((n, rows, length), dt)
    rws = lambda w, dt: jax.ShapeDtypeStruct((n, length, w), dt)
    cst = lambda a: pl.BlockSpec(a.shape, lambda b, i, nd=a.ndim: (0,) * nd)
    out_specs = (ft(MW), ft(2 * MW), rw(MW), ft(MW), ft(MW), ft(2 * KW), rw(KW), rw(KW), ft(2 * KW), rw(KW), ft(KW),
                 ft(2 * KW), rw(KW), ft(KW), ft(NG_ROWS),
                 pl.BlockSpec((TM, 2 * d), lambda b, i: (b * nt + i, 0)),
                 pl.BlockSpec((1, 1, MW), lambda b, i: (b * nt + i, 0, 0)))
    out_shape = (fts(MW, BF16), fts(2 * MW, F32), rws(MW, BF16), fts(MW, BF16), fts(MW, BF16), fts(2 * KW, F32),
                 rws(KW, F32), rws(KW, F32), fts(2 * KW, F32), rws(KW, BF16), fts(KW, BF16),
                 fts(2 * KW, F32), rws(KW, BF16), fts(KW, BF16), fts(NG_ROWS, F32),
                 jax.ShapeDtypeStruct((n * length, 2 * d), BF16), jax.ShapeDtypeStruct((n * nt, 1, MW), F32))
    names = ("mqt", "mkvt", "mk", "mvt", "nqt", "ckvt", "ck", "cv", "skvt", "sk", "svt", "wkvt", "wk", "wvt", "ngt",
             "mg", "ksum")
    outs = pl.pallas_call(
        _inproj_kernel, grid=(n, nt),
        in_specs=[pl.BlockSpec((1, TM, d), lambda b, i: (b, i, 0)), cst(g_attn), cst(w1t), cst(wngt), cst(wmg),
                  cst(gcol)],
        out_specs=out_specs, out_shape=out_shape,
        compiler_params=_cparams(("arbitrary", "arbitrary")), name="inproj",
    )(x, g_attn, w1t, wngt, wmg, gcol)
    return dict(zip(names, outs))


def _alibi_tables(slopes):
    rel = (np.arange(TQ)[:, None] - np.arange(TQ)[None, :]).astype(np.float32)
    bias = np.asarray(slopes, np.float32)[:, None, None] * rel[None]
    diag = np.where(rel[None] <= 0, bias, np.float32(NEG))
    edge = np.where(rel[None] >= 0, bias, np.float32(NEG))
    return jnp.asarray(bias), jnp.asarray(diag), jnp.asarray(edge)


def _moba_prompt_kernel(qt_ref, k_ref, vt_ref, mean_ref, bias_ref, diag_ref, o_ref,
                        qz_sc, sel_sc, m_sc, l_sc, acc_sc, s_sc, p_sc, *, nb):
    i = pl.program_id(1)
    npair = MOBA_HEADS // 2
    zeros = jnp.zeros((HEAD_DIM, TQ), BF16)
    for pair in range(npair):
        r0 = 2 * pair * HEAD_DIM
        qz_sc[pair, :, 0:TQ] = jnp.concatenate([qt_ref[0, r0:r0 + HEAD_DIM, :], zeros], axis=0)
        qz_sc[pair, :, TQ:2 * TQ] = jnp.concatenate([zeros, qt_ref[0, r0 + HEAD_DIM:r0 + 2 * HEAD_DIM, :]], axis=0)
    m_sc[...] = jnp.full(m_sc.shape, NEG, F32)
    l_sc[...] = jnp.zeros(l_sc.shape, F32)
    acc_sc[...] = jnp.zeros(acc_sc.shape, F32)

    nbp = mean_ref.shape[1]
    rowb = lax.broadcasted_iota(jnp.int32, (nbp, TQ), 0)
    cand = rowb < i
    for h in range(MOBA_HEADS):
        pair, half = divmod(h, 2)
        qz = qz_sc[pair, :, half * TQ:(half + 1) * TQ]
        mh, ml = _split_bf16(mean_ref[0, :, pair * LANES:(pair + 1) * LANES])
        ss = _dot(mh, qz) + _dot(ml, qz)
        rank = _top_rank_rows(jnp.where(cand, ss, -jnp.inf), nb, rowb)
        sel_sc[h] = jnp.where(cand & (rank < MOBA_TOPK), 1.0, 0.0)

    def kv_block(kb, own):
        start = pl.multiple_of(kb * TQ, TQ)
        k_blk = k_ref[0, pl.ds(start, TQ), :]
        vt_blk = vt_ref[0, :, pl.ds(start, TQ)]
        off = ((kb - i) * TQ).astype(F32)
        tab_ref = diag_ref if own else bias_ref
        bms = []
        for pair in range(npair):
            s2 = _dot(k_blk[:, pair * LANES:(pair + 1) * LANES], qz_sc[pair])
            for half in range(2):
                h = 2 * pair + half
                s = s2[:, half * TQ:(half + 1) * TQ] + tab_ref[h]
                s_sc[h] = s
                bms.append(jnp.max(s, axis=0, keepdims=True))
        alphas = []
        for h in range(MOBA_HEADS):
            m_old = m_sc[h:h + 1, :]
            if own:
                m_new = jnp.maximum(m_old, bms[h])
                shift = m_new
            else:
                c = MOBA_SLOPES[h] * off
                sel = sel_sc[h, pl.ds(kb, 1), :] > 0.5
                m_new = jnp.maximum(m_old, jnp.where(sel, bms[h] + c, NEG))
                shift = jnp.where(sel, m_new - c, BIG)
            alpha = jnp.exp2(m_old - m_new)
            psum = jnp.zeros((8, TQ), F32)
            for r0 in range(0, TQ, KEY_CHUNK):
                rs = slice(r0, r0 + KEY_CHUNK)
                p = jnp.exp2(s_sc[h, rs, :] - shift)
                psum = psum + jnp.sum(p.reshape(KEY_CHUNK // 8, 8, TQ), axis=0)
                p_sc[h, rs, :] = p.astype(BF16)
            l_sc[h:h + 1, :] = alpha * l_sc[h:h + 1, :] + jnp.sum(psum, axis=0, keepdims=True)
            m_sc[h:h + 1, :] = m_new
            alphas.append(alpha)
        for h in range(MOBA_HEADS):
            rows = slice(h * HEAD_DIM, (h + 1) * HEAD_DIM)
            acc_sc[rows, :] = alphas[h] * acc_sc[rows, :] + _dot(vt_blk[rows, :], p_sc[h])

    def body(kb, carry):
        kv_block(kb, False)
        return carry

    lax.fori_loop(0, i, body, 0)
    kv_block(i, True)

    for h in range(MOBA_HEADS):
        rows = slice(h * HEAD_DIM, (h + 1) * HEAD_DIM)
        acc_sc[rows, :] = acc_sc[rows, :] / l_sc[h:h + 1, :]
    o_ref[0] = jnp.transpose(acc_sc[...]).astype(BF16)


def _moba_prompt(qt, k, vt, means, bias, diag):
    b, w, s = qt.shape
    nb = s // TQ
    return pl.pallas_call(
        functools.partial(_moba_prompt_kernel, nb=nb), grid=(b, nb),
        in_specs=[pl.BlockSpec((1, w, TQ), lambda bi, i: (bi, 0, i)),
                  pl.BlockSpec((1, s, w), lambda bi, i: (bi, 0, 0)),
                  pl.BlockSpec((1, w, s), lambda bi, i: (bi, 0, 0)),
                  pl.BlockSpec((1,) + means.shape[1:], lambda bi, i: (bi, 0, 0)),
                  _const_spec(bias.shape), _const_spec(diag.shape)],
        out_specs=pl.BlockSpec((1, TQ, w), lambda bi, i: (bi, i, 0)),
        out_shape=jax.ShapeDtypeStruct((b, s, w), BF16),
        scratch_shapes=[pltpu.VMEM((MOBA_HEADS // 2, LANES, 2 * TQ), BF16),
                        pltpu.VMEM((MOBA_HEADS, means.shape[1], TQ), F32),
                        pltpu.VMEM((MOBA_HEADS, TQ), F32), pltpu.VMEM((MOBA_HEADS, TQ), F32),
                        pltpu.VMEM((w, TQ), F32), pltpu.VMEM((MOBA_HEADS, TQ, TQ), F32),
                        pltpu.VMEM((MOBA_HEADS, TQ, TQ), BF16)],
        compiler_params=_cparams(("arbitrary", "arbitrary")), name="moba_prompt",
    )(qt, k, vt, means, bias, diag)


def _compress_core(load_rows, ncp, wfull_ref, pe_ref, w1_ref, b1_ref, w2_ref, b2_ref, gk_ref, bd_ref, ck_ref, cv_ref):
    kw = NSA_GROUPS * HEAD_DIM
    hw = wfull_ref.shape[2]
    projs = []
    for kv in range(2):
        acc = jnp.zeros((ncp, hw), F32)
        for t in range(0, CMP_STRIDE, 2):
            x2 = jnp.concatenate([load_rows(kv, t), load_rows(kv, t + 1)], axis=1).astype(BF16)
            acc = acc + _dot(x2, wfull_ref[kv, t * kw:(t + 2) * kw, :])
        projs.append(acc)
    proj = jnp.concatenate(projs, axis=1)
    hids = []
    for kvg in range(2 * NSA_GROUPS):
        kv = kvg // NSA_GROUPS
        pe_term = _dot(pe_ref[kv], w1_ref[kv])[0:1, :]
        c0 = kvg * 2 * CMP_HIDDEN
        p0 = proj[:, c0:c0 + CMP_HIDDEN]
        p1 = pltpu.roll(proj[:, c0 + CMP_HIDDEN:c0 + 2 * CMP_HIDDEN], ncp - 1, 0)
        hids.append(jax.nn.gelu(b1_ref[kv:kv + 1, :] + p0 + p1 + pe_term))
    hid = jnp.concatenate(hids, axis=1).astype(BF16)
    out = _dot(hid, w2_ref[...]) + b2_ref[...]
    ck_ref[0] = _seg_rms(out[:, 0:kw], bd_ref[...], gk_ref[...]).astype(BF16)
    cv_ref[0] = out[:, kw:2 * kw].astype(BF16)


def _compress_prompt_kernel(xk_ref, xv_ref, *refs):
    ncp = xk_ref.shape[1] // CMP_STRIDE
    x_refs = (xk_ref, xv_ref)
    _compress_core(lambda kv, t: x_refs[kv][0, pl.ds(t, ncp, stride=CMP_STRIDE), :], ncp, *refs)


def _compress_sample_kernel(pt_ref, *refs, n_pages):
    pg_refs = refs[:n_pages]
    x_scs = refs[-2:]
    kw = NSA_GROUPS * HEAD_DIM
    for j in range(n_pages):
        for kv in range(2):
            x_scs[kv][j * PAGE:(j + 1) * PAGE, :] = jnp.transpose(pg_refs[j][0, kv * kw:(kv + 1) * kw, :])
    ncp = n_pages * PAGE // CMP_STRIDE
    _compress_core(lambda kv, t: x_scs[kv][pl.ds(t, ncp, stride=CMP_STRIDE), :], ncp, *refs[n_pages:-2])


def _compress_consts(cw):
    return (cw["wfull"], cw["pe"], cw["w1"], cw["b1"], cw["w2bd"], cw["b2row"], cw["gk"], cw["bd"])


def _compress_prompt(xk, xv, cw):
    b, s, kw = xk.shape
    ncp = s // CMP_STRIDE
    consts = _compress_consts(cw)
    xspec = pl.BlockSpec((1, s, kw), lambda bi: (bi, 0, 0))
    ospec = pl.BlockSpec((1, ncp, kw), lambda bi: (bi, 0, 0))
    return pl.pallas_call(
        _compress_prompt_kernel, grid=(b,),
        in_specs=[xspec, xspec] + [_const_spec(a.shape) for a in consts],
        out_specs=(ospec, ospec),
        out_shape=(jax.ShapeDtypeStruct((b, ncp, kw), BF16), jax.ShapeDtypeStruct((b, ncp, kw), BF16)),
        compiler_params=_cparams(("arbitrary",)), name="nsa_compress_prompt",
    )(xk, xv, *consts)


def _compress_sample(page_table, pool_t, cw):
    db, n_pages = page_table.shape
    fw = pool_t.shape[1]
    ncp = n_pages * PAGE // CMP_STRIDE
    consts = _compress_consts(cw)
    kw = NSA_GROUPS * HEAD_DIM
    cst = lambda a: pl.BlockSpec(a.shape, lambda bi, pt, nd=a.ndim: (0,) * nd)
    ospec = pl.BlockSpec((1, ncp, kw), lambda bi, pt: (bi, 0, 0))
    pg_specs = [pl.BlockSpec((1, fw, PAGE), lambda bi, pt, j=j: (pt[bi, j], 0, 0)) for j in range(n_pages)]
    gs = pltpu.PrefetchScalarGridSpec(
        num_scalar_prefetch=1, grid=(db,), in_specs=pg_specs + [cst(a) for a in consts], out_specs=(ospec, ospec),
        scratch_shapes=[pltpu.VMEM((n_pages * PAGE, kw), F32), pltpu.VMEM((n_pages * PAGE, kw), F32)])
    return pl.pallas_call(
        functools.partial(_compress_sample_kernel, n_pages=n_pages), grid_spec=gs,
        out_shape=(jax.ShapeDtypeStruct((db, ncp, kw), BF16), jax.ShapeDtypeStruct((db, ncp, kw), BF16)),
        compiler_params=_cparams(("arbitrary",)), name="nsa_compress_sample",
    )(page_table, *([pool_t] * n_pages), *consts)


def _nsa_prompt_kernel(qt_ref, ck_ref, cvt_ref, sk_ref, svt_ref, wk_ref, wvt_ref, ng_ref, ovl_ref, cb_ref,
                       bias_ref, diag_ref, edge_ref, o_ref,
                       qz_sc, p_sc, s_sc, sel_sc, oc_sc, ms_sc, ls_sc, as_sc, mw_sc, lw_sc, aw_sc, *, ns):
    i = pl.program_id(1)
    ncp = ck_ref.shape[1]
    zeros = jnp.zeros((HEAD_DIM, TQ), BF16)
    for h in range(NSA_HEADS):
        g, p_ = divmod(h, NSA_HPG)
        qh = qt_ref[0, h * HEAD_DIM:(h + 1) * HEAD_DIM, :]
        qz_sc[g, :, p_ * TQ:(p_ + 1) * TQ] = jnp.concatenate([qh, zeros] if g == 0 else [zeros, qh], axis=0)
    for m_ref, l_ref, a_ref in ((ms_sc, ls_sc, as_sc), (mw_sc, lw_sc, aw_sc)):
        m_ref[...] = jnp.full(m_ref.shape, NEG, F32)
        l_ref[...] = jnp.zeros(l_ref.shape, F32)
        a_ref[...] = jnp.zeros(a_ref.shape, F32)

    qpos = i * TQ + lax.broadcasted_iota(jnp.int32, (1, TQ), 1)
    cend = lax.broadcasted_iota(jnp.int32, (ncp, 1), 0) * CMP_STRIDE + (CMP_LEN - 1)
    cmask = cend <= qpos
    nsr = sel_sc.shape[1]
    rowj = lax.broadcasted_iota(jnp.int32, (nsr, TQ), 0)
    cur = _div_pow2(qpos, SEL_BLOCK)
    forced = (rowj == 0) | (rowj == cur) | (rowj == cur - 1)
    cand = rowj <= cur
    ioff = (i * TQ).astype(F32)
    for g in range(NSA_GROUPS):
        grow = slice(g * HEAD_DIM, (g + 1) * HEAD_DIM)
        s4 = _dot(ck_ref[0], qz_sc[g])
        psum = jnp.zeros((ncp, TQ), F32)
        for p_ in range(NSA_HPG):
            h = g * NSA_HPG + p_
            s = jnp.where(cmask, s4[:, p_ * TQ:(p_ + 1) * TQ] + cb_ref[h] - NSA_SLOPES[h] * ioff, NEG)
            e = jnp.where(cmask, jnp.exp2(s - jnp.max(s, axis=0, keepdims=True)), 0.0)
            den = jnp.sum(e, axis=0, keepdims=True)
            pn = e / jnp.where(den > 0.0, den, 1.0)
            psum = psum + pn
            p_sc[g, 0:ncp, p_ * TQ:(p_ + 1) * TQ] = pn.astype(BF16)
        oc4 = _dot(cvt_ref[0], p_sc[g, 0:ncp, :])
        for p_ in range(NSA_HPG):
            h = g * NSA_HPG + p_
            oc_sc[h * HEAD_DIM:(h + 1) * HEAD_DIM, :] = oc4[grow, p_ * TQ:(p_ + 1) * TQ]
        ph, plo = _split_bf16(psum)
        imp = _dot(ovl_ref[...], ph) + _dot(ovl_ref[...], plo)
        val = jnp.where(cand, jnp.where(forced, jnp.inf, imp), -jnp.inf)
        rank = _top_rank_rows(val, ns, rowj)
        sel_sc[g] = jnp.where(cand & (rank < SEL_TOPN), 1.0, 0.0)

    spb = TQ // SEL_BLOCK

    def attend(k_blk, vt_blk, tab_ref, off, kb_sel, m_ref, l_ref, a_ref):
        blocks = [slice(j * SEL_BLOCK, (j + 1) * SEL_BLOCK) for j in range(spb)]
        parts = []
        for g in range(NSA_GROUPS):
            s4 = _dot(k_blk, qz_sc[g])
            for p_ in range(NSA_HPG):
                h = g * NSA_HPG + p_
                s = s4[:, p_ * TQ:(p_ + 1) * TQ] + tab_ref[h]
                s_sc[h] = s
                parts.append([jnp.max(s[bl, :].reshape(SEL_BLOCK // 8, 8, TQ), axis=0) for bl in blocks])
        alphas = []
        for h in range(NSA_HEADS):
            g, p_ = divmod(h, NSA_HPG)
            c = NSA_SLOPES[h] * off
            m_old = m_ref[h:h + 1, :]
            if kb_sel is None:
                sels = [None] * spb
                part = functools.reduce(jnp.maximum, parts[h])
            else:
                sels = [sel_sc[g, pl.ds(kb_sel * spb + j, 1), :] > 0.5 for j in range(spb)]
                part = functools.reduce(jnp.maximum, [jnp.where(sl, pt, NEG) for sl, pt in zip(sels, parts[h])])
            m_new = jnp.maximum(m_old, jnp.max(part, axis=0, keepdims=True) + c)
            alpha = jnp.exp2(m_old - m_new)
            psum = jnp.zeros((8, TQ), F32)
            for bl, sl in zip(blocks, sels):
                shift = m_new - c if sl is None else jnp.where(sl, m_new - c, BIG)
                p = jnp.exp2(s_sc[h, bl, :] - shift)
                psum = psum + jnp.sum(p.reshape(SEL_BLOCK // 8, 8, TQ), axis=0)
                p_sc[g, bl, p_ * TQ:(p_ + 1) * TQ] = p.astype(BF16)
            l_ref[h:h + 1, :] = alpha * l_ref[h:h + 1, :] + jnp.sum(psum, axis=0, keepdims=True)
            m_ref[h:h + 1, :] = m_new
            alphas.append(alpha)
        for g in range(NSA_GROUPS):
            grow = slice(g * HEAD_DIM, (g + 1) * HEAD_DIM)
            pv4 = _dot(vt_blk, p_sc[g])
            for p_ in range(NSA_HPG):
                h = g * NSA_HPG + p_
                rows = slice(h * HEAD_DIM, (h + 1) * HEAD_DIM)
                a_ref[rows, :] = alphas[h] * a_ref[rows, :] + pv4[grow, p_ * TQ:(p_ + 1) * TQ]

    def sel_tile(kb, tab_ref):
        start = pl.multiple_of(kb * TQ, TQ)
        attend(sk_ref[0, pl.ds(start, TQ), :], svt_ref[0, :, pl.ds(start, TQ)], tab_ref,
               ((kb - i) * TQ).astype(F32), kb, ms_sc, ls_sc, as_sc)

    def win_tile(back, tab_ref):
        kb = jnp.maximum(i - back, 0)
        start = pl.multiple_of(kb * TQ, TQ)
        attend(wk_ref[0, pl.ds(start, TQ), :], wvt_ref[0, :, pl.ds(start, TQ)], tab_ref,
               -float(back * TQ), None, mw_sc, lw_sc, aw_sc)

    def sel_body(kb, carry):
        sel_tile(kb, bias_ref)
        return carry

    lax.fori_loop(0, i, sel_body, 0)
    sel_tile(i, diag_ref)

    @pl.when(i >= 2)
    def _():
        win_tile(2, edge_ref)

    @pl.when(i >= 1)
    def _():
        win_tile(1, bias_ref)

    win_tile(0, diag_ref)

    ng = ng_ref[0]
    for h in range(NSA_HEADS):
        rows = slice(h * HEAD_DIM, (h + 1) * HEAD_DIM)
        o_s = as_sc[rows, :] / ls_sc[h:h + 1, :]
        o_w = aw_sc[rows, :] / lw_sc[h:h + 1, :]
        oc_sc[rows, :] = (ng[h:h + 1, :] * oc_sc[rows, :] + ng[NSA_HEADS + h:NSA_HEADS + h + 1, :] * o_s
                          + ng[2 * NSA_HEADS + h:2 * NSA_HEADS + h + 1, :] * o_w)
    o_ref[0] = jnp.transpose(oc_sc[...]).astype(BF16)


def _nsa_prompt(qt, ck, cvt, sk, svt, wk, wvt, ngt, ovl, cb, bias, diag, edge):
    b, w, s = qt.shape
    assert WINDOW == 2 * TQ
    ns = s // SEL_BLOCK
    kw = NSA_GROUPS * HEAD_DIM
    full = lambda a: pl.BlockSpec((1,) + a.shape[1:], lambda bi, i: (bi, 0, 0))
    hq = NSA_HPG * TQ
    return pl.pallas_call(
        functools.partial(_nsa_prompt_kernel, ns=ns), grid=(b, s // TQ),
        in_specs=[pl.BlockSpec((1, w, TQ), lambda bi, i: (bi, 0, i)), full(ck), full(cvt), full(sk), full(svt),
                  full(wk), full(wvt), pl.BlockSpec((1, ngt.shape[1], TQ), lambda bi, i: (bi, 0, i)),
                  _const_spec(ovl.shape), _const_spec(cb.shape), _const_spec(bias.shape), _const_spec(diag.shape),
                  _const_spec(edge.shape)],
        out_specs=pl.BlockSpec((1, TQ, w), lambda bi, i: (bi, i, 0)),
        out_shape=jax.ShapeDtypeStruct((b, s, w), BF16),
        scratch_shapes=[pltpu.VMEM((NSA_GROUPS, kw, hq), BF16), pltpu.VMEM((NSA_GROUPS, TQ, hq), BF16),
                        pltpu.VMEM((NSA_HEADS, TQ, TQ), F32),
                        pltpu.VMEM((NSA_GROUPS, ovl.shape[0], TQ), F32), pltpu.VMEM((w, TQ), F32),
                        pltpu.VMEM((NSA_HEADS, TQ), F32), pltpu.VMEM((NSA_HEADS, TQ), F32), pltpu.VMEM((w, TQ), F32),
                        pltpu.VMEM((NSA_HEADS, TQ), F32), pltpu.VMEM((NSA_HEADS, TQ), F32), pltpu.VMEM((w, TQ), F32)],
        compiler_params=_cparams(("arbitrary", "arbitrary")), name="nsa_prompt",
    )(qt, ck, cvt, sk, svt, wk, wvt, ngt, ovl, cb, bias, diag, edge)


def _moba_sample_kernel(pt_ref, q_ref, kn_ref, vn_ref, *rest, past, ds, nbs):
    pg_refs = rest[:PAGES_PER_STEP]
    o_ref, m_sc, l_sc, o_sc, ssum_sc = rest[PAGES_PER_STEP:]
    c = pl.program_id(1)
    nch = pl.num_programs(1)
    rows = MOBA_HEADS * ds
    w = MOBA_HEADS * HEAD_DIM
    ppb = MOBA_BLOCK // PAGE
    row_w = lax.broadcasted_iota(jnp.int32, (rows, w), 0)
    diag = _div_pow2(lax.broadcasted_iota(jnp.int32, (rows, w), 1), HEAD_DIM) == _div_pow2(row_w, ds)
    lane = lax.broadcasted_iota(jnp.int32, (rows, LANES), 1)
    row1 = lax.broadcasted_iota(jnp.int32, (rows, 1), 0)
    qi = _mod_pow2(row1, ds)
    slope = jnp.zeros((rows, 1), F32)
    for h in range(MOBA_HEADS):
        slope = jnp.where(_div_pow2(row1, ds) == h, MOBA_SLOPES[h], slope)
    q8 = q_ref[0].astype(F32)
    qbd = jnp.where(diag, jnp.concatenate([q8] * MOBA_HEADS, axis=0), 0.0).astype(BF16)

    @pl.when(c == 0)
    def _():
        m_sc[...] = jnp.full(m_sc.shape, NEG, F32)
        l_sc[...] = jnp.zeros(l_sc.shape, F32)
        ssum_sc[...] = jnp.zeros(ssum_sc.shape, F32)

    col = lax.broadcasted_iota(jnp.int32, (rows, MOBA_BLOCK), 1)
    m_all, l_all, ssum = m_sc[...], l_sc[...], ssum_sc[...]
    for bb in range(PAGES_PER_STEP // ppb):
        blk = c * (PAGES_PER_STEP // ppb) + bb
        kt = jnp.concatenate([pg_refs[bb * ppb + j][0, 0] for j in range(ppb)], axis=1)
        vt = jnp.concatenate([pg_refs[bb * ppb + j][0, 1] for j in range(ppb)], axis=1).astype(BF16)
        kdist = (col - qi + (blk * MOBA_BLOCK - past)).astype(F32)
        s_raw = _dot(qbd, kt.astype(BF16))
        ssum = jnp.where(lane == blk, jnp.sum(s_raw, axis=1, keepdims=True), ssum)
        s = s_raw + slope * kdist
        m_b = jnp.max(s, axis=1, keepdims=True)
        p = jnp.exp2(s - m_b)
        o_sc[blk] = jnp.where(diag, _dot_nt(p.astype(BF16), vt), 0.0)
        m_all = jnp.where(lane == blk, m_b, m_all)
        l_all = jnp.where(lane == blk, jnp.sum(p, axis=1, keepdims=True), l_all)
    m_sc[...] = m_all
    l_sc[...] = l_all
    ssum_sc[...] = ssum

    @pl.when(c == nch - 1)
    def _():
        ss = ssum_sc[...] * (1.0 / MOBA_BLOCK)
        own = past // MOBA_BLOCK
        cand = lane < own
        rank = _top_rank(jnp.where(cand, ss, -jnp.inf), nbs, lane)
        selb = cand & (rank < MOBA_TOPK)
        m_all = m_sc[...]
        m_past = jnp.max(jnp.where(selb, m_all, NEG), axis=1, keepdims=True)
        zpad = jnp.zeros((LANES - ds, w), F32)
        k_new = jnp.concatenate([kn_ref[0], zpad], axis=0).astype(BF16)
        v_new = jnp.concatenate([vn_ref[0], zpad], axis=0).astype(BF16)
        s_own = _dot_nt(qbd, k_new) + slope * (lane - qi).astype(F32)
        s_own = jnp.where(lane <= qi, s_own, NEG)
        m_tot = jnp.maximum(m_past, jnp.max(s_own, axis=1, keepdims=True))
        wgt = jnp.where(selb, jnp.exp2(m_all - m_tot), 0.0)
        p_own = jnp.exp2(s_own - m_tot)
        den = jnp.sum(wgt * l_sc[...], axis=1, keepdims=True) + jnp.sum(p_own, axis=1, keepdims=True)
        out = jnp.where(diag, _dot(p_own.astype(BF16), v_new), 0.0)
        for b2 in range(nbs):
            out = out + wgt[:, b2:b2 + 1] * o_sc[b2]
        out = out / den
        o8 = out[0:ds, :]
        for h in range(1, MOBA_HEADS):
            o8 = o8 + out[h * ds:(h + 1) * ds, :]
        o_ref[0] = o8.astype(BF16)


def _moba_sample(page_table, q, k_new, v_new, pool_t):
    db, ds, w = q.shape
    n_pages = page_table.shape[1]
    past = n_pages * PAGE
    nbs = past // MOBA_BLOCK
    assert n_pages % PAGES_PER_STEP == 0 and past % MOBA_BLOCK == 0 and nbs <= LANES and ds == 8
    tok = pl.BlockSpec((1, ds, w), lambda bi, c, pt: (bi, 0, 0))
    pg_specs = [pl.BlockSpec((1, 2, w, PAGE), lambda bi, c, pt, j=j: (pt[bi, c * PAGES_PER_STEP + j], 0, 0, 0))
                for j in range(PAGES_PER_STEP)]
    gs = pltpu.PrefetchScalarGridSpec(
        num_scalar_prefetch=1, grid=(db, n_pages // PAGES_PER_STEP),
        in_specs=[tok, tok, tok] + pg_specs, out_specs=tok,
        scratch_shapes=[pltpu.VMEM((MOBA_HEADS * ds, LANES), F32), pltpu.VMEM((MOBA_HEADS * ds, LANES), F32),
                        pltpu.VMEM((nbs, MOBA_HEADS * ds, w), F32), pltpu.VMEM((MOBA_HEADS * ds, LANES), F32)])
    return pl.pallas_call(
        functools.partial(_moba_sample_kernel, past=past, ds=ds, nbs=nbs), grid_spec=gs,
        out_shape=jax.ShapeDtypeStruct((db, ds, w), BF16),
        compiler_params=_cparams(("arbitrary", "arbitrary")), name="moba_sample",
    )(page_table, q, k_new, v_new, *([pool_t] * PAGES_PER_STEP))


def _nsa_sample_kernel(pt_ref, qg_ref, ck_ref, cv_ref, skn_ref, wkn_ref, wst_ref, ng_ref, ovl_ref, exp_ref,
                       *rest, past, ds, n_pages, ns):
    pg_refs = rest[:n_pages]
    o_ref = rest[n_pages]
    rows = NSA_HEADS * ds
    grows = NSA_HPG * ds
    kw = NSA_GROUPS * HEAD_DIM
    ncp = ck_ref.shape[1]
    nsp = ovl_ref.shape[1]
    wb = wst_ref.shape[2]
    row1 = lax.broadcasted_iota(jnp.int32, (rows, 1), 0)
    qi = _mod_pow2(row1, ds)
    qpos = past + qi
    slope = jnp.zeros((rows, 1), F32)
    for h in range(NSA_HEADS):
        slope = jnp.where(_div_pow2(row1, ds) == h, NSA_SLOPES[h], slope)
    lane = lax.broadcasted_iota(jnp.int32, (rows, LANES), 1)
    grp_half = _div_pow2(lane, HEAD_DIM) == _div_pow2(row1, grows)
    q_grp = qg_ref[0]

    def softmax_parts(s_list, m_list):
        m = None
        for s, mk in zip(s_list, m_list):
            mm = jnp.max(jnp.where(mk, s, NEG), axis=1, keepdims=True)
            m = mm if m is None else jnp.maximum(m, mm)
        ps = [jnp.where(mk, jnp.exp2(jnp.where(mk, s, NEG) - m), 0.0) for s, mk in zip(s_list, m_list)]
        den = sum(jnp.sum(p, axis=1, keepdims=True) for p in ps)
        return ps, jnp.where(den > 0.0, den, 1.0)

    def own_group(o):
        return jnp.where(grp_half, o, pltpu.roll(o, HEAD_DIM, 1))

    cend = lax.broadcasted_iota(jnp.int32, (rows, ncp), 1) * CMP_STRIDE + (CMP_LEN - 1)
    cmask = cend <= qpos
    s_c = _dot_nt(q_grp, ck_ref[0]) + slope * (cend - qpos).astype(F32)
    (e,), den = softmax_parts([s_c], [cmask])
    pn = e / den
    o_c = own_group(_dot(pn.astype(BF16), cv_ref[0]))
    lane_s = lax.broadcasted_iota(jnp.int32, (ds, nsp), 1)
    cur = _div_pow2(past + lax.broadcasted_iota(jnp.int32, (ds, 1), 0), SEL_BLOCK)
    forced = (lane_s == 0) | (lane_s == cur) | (lane_s == cur - 1)
    cand = lane_s <= cur
    sel_parts = []
    for g in range(NSA_GROUPS):
        psum = pn[g * grows:g * grows + ds]
        for p_ in range(1, NSA_HPG):
            psum = psum + pn[g * grows + p_ * ds:g * grows + (p_ + 1) * ds]
        ph, plo = _split_bf16(psum)
        imp = _dot(ph, ovl_ref[...]) + _dot(plo, ovl_ref[...])
        val = jnp.where(cand, jnp.where(forced, jnp.inf, imp), -jnp.inf)
        rank = _top_rank(val, ns, lane_s)
        sel_g = jnp.where(cand & (rank < SEL_TOPN), 1.0, 0.0)
        sel_parts.append(jnp.concatenate([sel_g] * NSA_HPG, axis=0))
    sel_f32 = jnp.concatenate(sel_parts, axis=0)
    sel_rows = sel_f32.astype(BF16)

    kt_all = jnp.concatenate([r[0, 0:kw, :] for r in pg_refs], axis=1).astype(BF16)
    vt_all = jnp.concatenate([r[0, kw:2 * kw, :] for r in pg_refs], axis=1).astype(BF16)
    kpos = lax.broadcasted_iota(jnp.int32, (rows, past), 1)
    s_past = _dot(q_grp, kt_all) + slope * (kpos - qpos).astype(F32)
    m_past = _dot(sel_rows, exp_ref[...]) > 0.5
    zpad = jnp.zeros((LANES - ds, 2 * kw), F32)
    skn = jnp.concatenate([skn_ref[0], zpad], axis=0)
    s_new = _dot_nt(q_grp, skn[:, 0:kw].astype(BF16)) + slope * (lane - qi).astype(F32)
    new_blk = past // SEL_BLOCK
    m_new = (lane <= qi) & (sel_f32[:, new_blk:new_blk + 1] > 0.5)
    (p_past, p_new), den = softmax_parts([s_past, s_new], [m_past, m_new])
    o_s = _dot_nt(p_past.astype(BF16), vt_all) + _dot(p_new.astype(BF16), skn[:, kw:2 * kw].astype(BF16))
    o_s = own_group(o_s / den)

    jst = lax.broadcasted_iota(jnp.int32, (rows, wb), 1)
    dist_st = qpos - (past - wb + jst)
    s_st = _dot(q_grp, wst_ref[0, 0:kw, :].astype(BF16)) - slope * dist_st.astype(F32)
    m_st = (dist_st >= 0) & (dist_st <= WINDOW)
    wkn = jnp.concatenate([wkn_ref[0], zpad], axis=0)
    s_wn = _dot_nt(q_grp, wkn[:, 0:kw].astype(BF16)) + slope * (lane - qi).astype(F32)
    m_wn = lane <= qi
    (p_st, p_wn), den = softmax_parts([s_st, s_wn], [m_st, m_wn])
    o_w = _dot_nt(p_st.astype(BF16), wst_ref[0, kw:2 * kw, :].astype(BF16)) \
        + _dot(p_wn.astype(BF16), wkn[:, kw:2 * kw].astype(BF16))
    o_w = own_group(o_w / den)

    ng = ng_ref[0]
    gate = lambda br: jnp.concatenate([ng[:, br * NSA_HEADS + h:br * NSA_HEADS + h + 1] for h in range(NSA_HEADS)],
                                      axis=0)
    o = gate(0) * o_c + gate(1) * o_s + gate(2) * o_w
    lane8 = lax.broadcasted_iota(jnp.int32, (ds, LANES), 1)
    for pair in range(NSA_HEADS // 2):
        o_even = o[(2 * pair) * ds:(2 * pair + 1) * ds]
        o_odd = o[(2 * pair + 1) * ds:(2 * pair + 2) * ds]
        o_ref[0, :, pair * LANES:(pair + 1) * LANES] = jnp.where(lane8 < HEAD_DIM, o_even, o_odd).astype(BF16)


def _nsa_sample(page_table, q_grp, ck, cv, sk_new, wk_new, wstate_t, ng, ovl, expand, pool_t):
    db, rows, _ = q_grp.shape
    ds = rows // NSA_HEADS
    n_pages = page_table.shape[1]
    past = n_pages * PAGE
    ns = -(-(past + ds) // SEL_BLOCK)
    fw = pool_t.shape[1]
    per_b = lambda a: pl.BlockSpec((1,) + a.shape[1:], lambda bi, pt: (bi, 0, 0))
    cst = lambda a: pl.BlockSpec(a.shape, lambda bi, pt, nd=a.ndim: (0,) * nd)
    pg_specs = [pl.BlockSpec((1, fw, PAGE), lambda bi, pt, j=j: (pt[bi, j], 0, 0)) for j in range(n_pages)]
    gs = pltpu.PrefetchScalarGridSpec(
        num_scalar_prefetch=1, grid=(db,),
        in_specs=[per_b(q_grp), per_b(ck), per_b(cv), per_b(sk_new), per_b(wk_new), per_b(wstate_t),
                  per_b(ng), cst(ovl), cst(expand)] + pg_specs,
        out_specs=pl.BlockSpec((1, ds, NSA_HEADS * HEAD_DIM), lambda bi, pt: (bi, 0, 0)))
    return pl.pallas_call(
        functools.partial(_nsa_sample_kernel, past=past, ds=ds, n_pages=n_pages, ns=ns), grid_spec=gs,
        out_shape=jax.ShapeDtypeStruct((db, ds, NSA_HEADS * HEAD_DIM), BF16),
        compiler_params=_cparams(("arbitrary",)), name="nsa_sample",
    )(page_table, q_grp, ck, cv, sk_new, wk_new, wstate_t, ng, ovl, expand, *([pool_t] * n_pages))


def _merge_kernel(x_ref, om_ref, on_ref, mg_ref, wa_ref, wb_ref, wo_ref, h_ref):
    d = x_ref.shape[1]
    a = _dot(om_ref[...], wa_ref[...])
    b = _dot(on_ref[...], wb_ref[...])
    merged = mg_ref[:, 0:d] * a + mg_ref[:, d:2 * d] * b
    h_ref[...] = x_ref[...] + _dot(merged.astype(BF16), wo_ref[...])


def _merge(x2d, o_m, o_n, mg, wa, wb, wo):
    t, d = x2d.shape
    row = lambda w: pl.BlockSpec((TM, w), lambda i: (i, 0))
    return pl.pallas_call(
        _merge_kernel, grid=(t // TM,),
        in_specs=[row(d), row(o_m.shape[1]), row(o_n.shape[1]), row(2 * d),
                  _const_spec(wa.shape), _const_spec(wb.shape), _const_spec(wo.shape)],
        out_specs=row(d), out_shape=jax.ShapeDtypeStruct((t, d), F32),
        compiler_params=_cparams(("arbitrary",)), name="merge",
    )(x2d, o_m, o_n, mg, wa, wb, wo)


def _ffn_kernel(h_ref, *rest, seq_len, per_seq_state):
    if per_seq_state:
        st_ref, gf_ref, win_ref, cw_ref, cb_ref, wd_ref, y_ref, cs_ref, carry_sc = rest
    else:
        st0_ref, st1_ref, gf_ref, win_ref, cw_ref, cb_ref, wd_ref, y_ref, g_ref = rest
    f = cw_ref.shape[1]
    h = h_ref[...]
    hn = (h * lax.rsqrt(jnp.mean(h * h, axis=-1, keepdims=True) + RMS_EPS) * gf_ref[...]).astype(BF16)
    gu = _dot(hn, win_ref[...])
    g = gu[:, 0:f]
    u = gu[:, f:2 * f]
    tm = h_ref.shape[0]
    idx = _mod_pow2(lax.broadcasted_iota(jnp.int32, (tm, 1), 0), seq_len)
    if per_seq_state:
        @pl.when(pl.program_id(1) == 0)
        def _():
            carry_sc[0:CONV_W - 1, :] = st_ref[0]
        st0 = carry_sc[0:1, :]
        st1 = carry_sc[1:2, :]
    else:
        st0 = st0_ref[...]
        st1 = st1_ref[...]
    g1 = pltpu.roll(g, 1, 0)
    g2 = pltpu.roll(g, 2, 0)
    prev1 = jnp.where(idx == 0, st1, g1)
    prev2 = jnp.where(idx == 0, st0, jnp.where(idx == 1, st1, g2))
    gc = cb_ref[...] + prev2 * cw_ref[0:1, :] + prev1 * cw_ref[1:2, :] + g * cw_ref[2:3, :]
    act = (jax.nn.silu(gc) * u).astype(BF16)
    y_ref[...] = h + _dot(act, wd_ref[...])
    if per_seq_state:
        carry_sc[0:CONV_W - 1, :] = g[tm - (CONV_W - 1):tm, :]
        cs_ref[0] = g[tm - (CONV_W - 1):tm, :]
    else:
        g_ref[...] = g


def _ffn_prompt(h2d, n, conv_state, gf, win, cw, cb, wd):
    t, d = h2d.shape
    f = cw.shape[1]
    nt = (t // n) // FFN_TM
    row = pl.BlockSpec((FFN_TM, d), lambda bi, i: (bi * nt + i, 0))
    st = pl.BlockSpec((1, CONV_W - 1, f), lambda bi, i: (bi, 0, 0))
    cst = lambda a: pl.BlockSpec(a.shape, lambda bi, i, nd=a.ndim: (0,) * nd, pipeline_mode=pl.Buffered(1))
    return pl.pallas_call(
        functools.partial(_ffn_kernel, seq_len=FFN_TM, per_seq_state=True), grid=(n, nt),
        in_specs=[row, st, cst(gf), cst(win), cst(cw), cst(cb), cst(wd)],
        out_specs=(row, st),
        out_shape=(jax.ShapeDtypeStruct((t, d), F32), jax.ShapeDtypeStruct((n, CONV_W - 1, f), F32)),
        scratch_shapes=[pltpu.VMEM((8, f), F32)],
        compiler_params=_cparams(("arbitrary", "arbitrary")), name="ffn_prompt",
    )(h2d, conv_state, gf, win, cw, cb, wd)


def _ffn_sample(h2d, seq_len, st0, st1, gf, win, cw, cb, wd):
    t, d = h2d.shape
    f = cw.shape[1]
    row = lambda w: pl.BlockSpec((TM, w), lambda i: (i, 0))
    return pl.pallas_call(
        functools.partial(_ffn_kernel, seq_len=seq_len, per_seq_state=False), grid=(t // TM,),
        in_specs=[row(d), row(f), row(f), _const_spec(gf.shape), _const_spec(win.shape), _const_spec(cw.shape),
                  _const_spec(cb.shape), _const_spec(wd.shape)],
        out_specs=(row(d), row(f)),
        out_shape=(jax.ShapeDtypeStruct((t, d), F32), jax.ShapeDtypeStruct((t, f), F32)),
        compiler_params=_cparams(("arbitrary",)), name="ffn_sample",
    )(h2d, st0, st1, gf, win, cw, cb, wd)


def _block_diag_ones(n, blk):
    i = np.arange(n)
    return jnp.asarray((i[:, None] // blk == i[None, :] // blk).astype(np.float32), BF16)


def _overlap_matrix(ncp, nc, ns, nsp):
    c0 = np.arange(ncp) * CMP_STRIDE
    j0 = np.arange(nsp) * SEL_BLOCK
    m = (c0[:, None] < j0[None, :] + SEL_BLOCK) & (c0[:, None] + CMP_LEN > j0[None, :])
    m &= (np.arange(ncp)[:, None] < nc) & (np.arange(nsp)[None, :] < ns)
    return m.astype(np.float32)


def _layer_weights(l, attn_norm_g, w_in, qk_norm_g, nsa_cmp_pe, nsa_cmp_w1, nsa_cmp_b1, nsa_cmp_w2, nsa_cmp_b2,
                   w_branch_moba, w_branch_nsa, w_out, ffn_norm_g, w_ffn_in, ffn_conv_w, ffn_conv_b, w_ffn_down):
    d = w_in.shape[1]
    wi = w_in[l]
    kvw = KW
    o_mg = R_END + 3 * NSA_HEADS
    w1t = jnp.transpose(wi[:, :R_END]).astype(BF16)
    wngt = jnp.pad(jnp.transpose(wi[:, R_END:o_mg]), ((0, NG_ROWS - 3 * NSA_HEADS), (0, 0))).astype(BF16)
    wmg = wi[:, o_mg:o_mg + 2 * d].astype(BF16)
    g = qk_norm_g[l]
    ones = lambda n: jnp.ones((n,), F32)
    gcol = jnp.concatenate([jnp.tile(g[0], MOBA_HEADS), jnp.tile(g[1], MOBA_HEADS), ones(MW), jnp.tile(g[2], NSA_HEADS),
                            ones(2 * KW), jnp.tile(g[4], NSA_GROUPS), ones(KW), jnp.tile(g[5], NSA_GROUPS),
                            ones(KW)])[:, None]
    r = CMP_LEN // CMP_STRIDE
    w1c = nsa_cmp_w1[l]
    w1r = w1c.reshape(2, r, CMP_STRIDE, HEAD_DIM, CMP_HIDDEN)
    nkvg = 2 * NSA_GROUPS
    wfull = jnp.zeros((CMP_STRIDE, nkvg, HEAD_DIM, nkvg, r, CMP_HIDDEN), F32)
    for kvg in range(nkvg):
        wfull = wfull.at[:, kvg, :, kvg].set(jnp.transpose(w1r[kvg // NSA_GROUPS], (1, 2, 0, 3)))
    wfull = wfull.reshape(CMP_STRIDE, 2, kvw, 2, NSA_GROUPS * r * CMP_HIDDEN)
    wfull = jnp.stack([wfull[:, kv, :, kv] for kv in range(2)]).reshape(2, CMP_STRIDE * kvw, -1).astype(BF16)
    w2bd = jnp.zeros((nkvg, CMP_HIDDEN, nkvg, HEAD_DIM), F32)
    for kvg in range(nkvg):
        w2bd = w2bd.at[kvg, :, kvg].set(nsa_cmp_w2[l][kvg // NSA_GROUPS])
    w2bd = w2bd.reshape(nkvg * CMP_HIDDEN, nkvg * HEAD_DIM).astype(BF16)
    b2 = nsa_cmp_b2[l]
    cw = dict(
        wfull=wfull,
        pe=jnp.broadcast_to(nsa_cmp_pe[l].reshape(2, 1, CMP_LEN * HEAD_DIM), (2, 8, CMP_LEN * HEAD_DIM)).astype(BF16),
        w1=w1c.astype(BF16), b1=nsa_cmp_b1[l], w2bd=w2bd,
        b2row=jnp.concatenate([jnp.tile(b2[0], NSA_GROUPS), jnp.tile(b2[1], NSA_GROUPS)])[None, :],
        gk=jnp.tile(g[3], NSA_GROUPS)[None, :], bd=_block_diag_ones(kvw, HEAD_DIM))
    return dict(
        g_attn=attn_norm_g[l][None, :], w1t=w1t, wngt=wngt, wmg=wmg, gcol=gcol, cw=cw,
        wa=w_branch_moba[l].astype(BF16), wb=w_branch_nsa[l].astype(BF16), wo=w_out[l].astype(BF16),
        gf=ffn_norm_g[l][None, :], win=w_ffn_in[l].astype(BF16), cwt=ffn_conv_w[l], cb=ffn_conv_b[l][None, :],
        wd=w_ffn_down[l].astype(BF16))


def _prompt_layer(x, wts):
    b, s, d = x.shape
    assert s % (CMP_STRIDE * LANES) == 0 and s % TQ == 0
    t = b * s
    x2d = x.reshape(t, d)
    assert TM == MOBA_BLOCK
    pr = _inproj(x, wts["g_attn"], wts["w1t"], wts["wngt"], wts["wmg"], wts["gcol"])
    mg = pr["mg"]
    nb = s // MOBA_BLOCK
    nbp = -(-nb // 16) * 16
    means = jnp.pad(pr["ksum"].reshape(b, nb, MW) * (1.0 / MOBA_BLOCK), ((0, 0), (0, nbp - nb), (0, 0)))
    m_bias, m_diag, _ = _alibi_tables(MOBA_SLOPES)
    o_m = _moba_prompt(pr["mqt"], pr["mk"], pr["mvt"], means, m_bias, m_diag)

    ncp = s // CMP_STRIDE
    ck, cv = _compress_prompt(pr["ck"], pr["cv"], wts["cw"])
    ns = s // SEL_BLOCK
    nsr = -(-ns // 8) * 8
    ovl_t = jnp.asarray(_overlap_matrix(ncp, ncp - CMP_LEN // CMP_STRIDE + 1, ns, nsr).T, BF16)
    cb = np.asarray(NSA_SLOPES, np.float32)[:, None, None] * (
        (np.arange(ncp) * CMP_STRIDE + CMP_LEN - 1)[None, :, None] - np.arange(TQ)[None, None, :]).astype(np.float32)
    n_bias, n_diag, n_edge = _alibi_tables(NSA_SLOPES)
    o_n = _nsa_prompt(pr["nqt"], ck, jnp.swapaxes(cv, 1, 2), pr["sk"], pr["svt"], pr["wk"], pr["wvt"], pr["ngt"],
                      ovl_t, jnp.asarray(cb), n_bias, n_diag, n_edge)

    h2d = _merge(x2d, o_m.reshape(t, -1), o_n.reshape(t, -1), mg, wts["wa"], wts["wb"], wts["wo"])
    f = wts["cwt"].shape[1]
    y2d, conv = _ffn_prompt(h2d, b, jnp.zeros((b, CONV_W - 1, f), F32), wts["gf"], wts["win"], wts["cwt"],
                            wts["cb"], wts["wd"])
    wrows = min(WINDOW, s)
    rows_view = lambda a, heads: jnp.transpose(a.reshape(b, 2, heads, HEAD_DIM, a.shape[2]), (0, 4, 1, 2, 3))
    outs = (rows_view(pr["mkvt"], MOBA_HEADS), rows_view(pr["ckvt"], NSA_GROUPS), rows_view(pr["skvt"], NSA_GROUPS),
            rows_view(pr["wkvt"][:, :, s - wrows:], NSA_GROUPS), conv)
    return y2d.reshape(b, s, d), outs


def _pages_t(cache):
    dp, npool = cache.shape[0], cache.shape[1]
    return jnp.transpose(cache, (0, 1, 3, 4, 5, 2)).reshape(dp * npool, -1, PAGE)


def _sample_layer(x, wts, pool_moba_t, pool_cmp_t, pool_sel_t, win_state, conv_state, page_table):
    db, ds, d = x.shape
    t = db * ds
    assert t % TM == 0 and ds == 8
    n_pages = page_table.shape[1]
    past = n_pages * PAGE
    x2d = x.reshape(t, d)
    pr = _inproj(x2d[None], wts["g_attn"], wts["w1t"], wts["wngt"], wts["wmg"], wts["gcol"])
    rows = lambda a: jnp.transpose(a[0])
    mq, mkv, nq, ckv, skv, wkv = (rows(pr[k]) for k in ("mqt", "mkvt", "nqt", "ckvt", "skvt", "wkvt"))
    ng = jnp.pad(rows(pr["ngt"]), ((0, 0), (0, LANES - NG_ROWS)))
    mg = pr["mg"]
    mw = MW
    kw = KW
    o_m = _moba_sample(page_table, mq.reshape(db, ds, mw), mkv[:, 0:mw].reshape(db, ds, mw),
                       mkv[:, mw:2 * mw].reshape(db, ds, mw), pool_moba_t.reshape(-1, 2, mw, PAGE))

    assert (past + ds) // CMP_STRIDE == past // CMP_STRIDE
    ncp = past // CMP_STRIDE
    ck, cv = _compress_sample(page_table, pool_cmp_t, wts["cw"])

    nqf = nq.astype(F32).reshape(db, ds, NSA_HEADS, HEAD_DIM).transpose(0, 2, 1, 3)
    zero = jnp.zeros_like(nqf)
    hh = jnp.arange(NSA_HEADS)[None, :, None, None]
    q_grp = jnp.where(hh // NSA_HPG == 0, jnp.concatenate([nqf, zero], -1), jnp.concatenate([zero, nqf], -1))
    q_grp = q_grp.reshape(db, NSA_HEADS * ds, LANES).astype(BF16)
    ns = -(-(past + ds) // SEL_BLOCK)
    nsp = -(-ns // LANES) * LANES
    ovl = jnp.asarray(_overlap_matrix(ncp, ncp - CMP_LEN // CMP_STRIDE + 1, ns, nsp), BF16)
    expand = jnp.asarray((np.arange(nsp)[:, None] == np.arange(past)[None, :] // SEL_BLOCK).astype(np.float32), BF16)
    wb = win_state.shape[1]
    wst_t = jnp.transpose(win_state, (0, 2, 3, 4, 1)).reshape(db, 2 * kw, wb)
    o_n = _nsa_sample(page_table, q_grp, ck, cv, skv.reshape(db, ds, -1), wkv.reshape(db, ds, -1), wst_t,
                      ng.reshape(db, ds, LANES), ovl, expand, pool_sel_t)

    h2d = _merge(x2d, o_m.reshape(t, -1), o_n.reshape(t, -1), mg, wts["wa"], wts["wb"], wts["wo"])
    st0 = jnp.repeat(conv_state[:, 0], ds, axis=0)
    st1 = jnp.repeat(conv_state[:, 1], ds, axis=0)
    y2d, g2d = _ffn_sample(h2d, ds, st0, st1, wts["gf"], wts["win"], wts["cwt"], wts["cb"], wts["wd"])
    gp = jnp.concatenate([conv_state, g2d.reshape(db, ds, -1)], axis=1)
    wcat = jnp.concatenate([win_state, wkv.reshape(db, ds, 2, NSA_GROUPS, HEAD_DIM)], axis=1)
    wrows = min(WINDOW, wcat.shape[1])
    outs = (mkv.reshape(db, ds, 2, MOBA_HEADS, HEAD_DIM), ckv.reshape(db, ds, 2, NSA_GROUPS, HEAD_DIM),
            skv.reshape(db, ds, 2, NSA_GROUPS, HEAD_DIM), wcat[:, wcat.shape[1] - wrows:], gp[:, ds:])
    return y2d.reshape(db, ds, d), outs


def kernel(x_prompt, x_sample, cache_moba_kv, cache_nsa_cmp_kv, cache_nsa_sel_kv, state_nsa_win_kv, state_ffn_conv,
           page_table, attn_norm_g, w_in, qk_norm_g, nsa_cmp_pe, nsa_cmp_w1, nsa_cmp_b1, nsa_cmp_w2, nsa_cmp_b2,
           w_branch_moba, w_branch_nsa, w_out, ffn_norm_g, w_ffn_in, ffn_conv_w, ffn_conv_b, w_ffn_down):
    depth = w_in.shape[0]
    n_pool = cache_moba_kv.shape[1]
    pools = [_pages_t(c) for c in (cache_moba_kv, cache_nsa_cmp_kv, cache_nsa_sel_kv)]
    hp, hs = x_prompt, x_sample
    new = [[] for _ in range(10)]
    for l in range(depth):
        wts = _layer_weights(l, attn_norm_g, w_in, qk_norm_g, nsa_cmp_pe, nsa_cmp_w1, nsa_cmp_b1, nsa_cmp_w2,
                             nsa_cmp_b2, w_branch_moba, w_branch_nsa, w_out, ffn_norm_g, w_ffn_in, ffn_conv_w,
                             ffn_conv_b, w_ffn_down)
        hp, outs_p = _prompt_layer(hp, wts)
        hs, outs_s = _sample_layer(hs, wts, *pools, state_nsa_win_kv[l], state_ffn_conv[l],
                                   page_table + l * n_pool)
        for lst, arr in zip(new[:5], outs_p):
            lst.append(arr)
        for lst, arr in zip(new[5:], outs_s):
            lst.append(arr)
    st = [jnp.stack(v) for v in new]
    return (hp, hs, st[0], st[5], st[1], st[6], st[2], st[7], st[3], st[8], st[4], st[9])
```

```python
import functools

import numpy as np
import jax
import jax.numpy as jnp
from jax import lax
from jax.experimental import pallas as pl
from jax.experimental.pallas import tpu as pltpu

F32 = jnp.float32
BF16 = jnp.bfloat16

HEAD_DIM = 64
MOBA_HEADS = 8
MOBA_BLOCK = 256
MOBA_TOPK = 3
NSA_HEADS = 8
NSA_GROUPS = 2
NSA_HPG = NSA_HEADS // NSA_GROUPS
CMP_LEN = 32
CMP_STRIDE = 16
CMP_HIDDEN = 128
SEL_BLOCK = 64
SEL_TOPN = 8
WINDOW = 512
CONV_W = 3
PAGE = 128
RMS_EPS = 1e-6
NEG = -1e30
BIG = 1e30
LOG2E = 1.4426950408889634
QK_SCALE = HEAD_DIM ** -0.5 * LOG2E

LANES = 128
TQ = 256
FFN_TM = 512
KEY_CHUNK = 64
TM = 256
PAGES_PER_STEP = 16
VMEM_LIMIT = 56 * 1024 * 1024

MOBA_SLOPES = tuple(float(2.0 ** (-8.0 * (h + 1) / MOBA_HEADS)) * LOG2E for h in range(MOBA_HEADS))
NSA_SLOPES = tuple(float(2.0 ** (-8.0 * (h + 1) / NSA_HEADS)) * LOG2E for h in range(NSA_HEADS))


def _dot(a, b):
    return jnp.dot(a, b, preferred_element_type=F32)


def _dot_nt(a, b):
    return lax.dot_general(a, b, (((1,), (1,)), ((), ())), preferred_element_type=F32)


def _div_pow2(x, n):
    assert n & (n - 1) == 0
    return lax.shift_right_logical(x, jnp.int32(n.bit_length() - 1))


def _mod_pow2(x, n):
    assert n & (n - 1) == 0
    return x & (n - 1)


def _split_bf16(x):
    hi = x.astype(BF16)
    lo = (x - hi.astype(F32)).astype(BF16)
    return hi, lo


def _cparams(sem):
    return pltpu.CompilerParams(dimension_semantics=sem, vmem_limit_bytes=VMEM_LIMIT)


def _const_spec(shape):
    nd = len(shape)
    return pl.BlockSpec(shape, lambda *_: (0,) * nd)


def _seg_rms(y, bd, gain):
    wb = bd.shape[0]
    outs = []
    for c in range(y.shape[1] // wb):
        s = y[:, c * wb:(c + 1) * wb]
        ss = _dot((s * s).astype(BF16), bd)
        outs.append(s * lax.rsqrt(ss * (1.0 / HEAD_DIM) + RMS_EPS))
    r = outs[0] if len(outs) == 1 else jnp.concatenate(outs, axis=1)
    return r * gain


def _top_rank(val, n, lane):
    rank = jnp.zeros(val.shape, F32)
    for j2 in range(n):
        c = val[:, j2:j2 + 1]
        before = (c > val) | ((c == val) & (lane > j2))
        rank = rank + jnp.where(before, 1.0, 0.0)
    return rank


def _top_rank_rows(val, n, rowid):
    rank = jnp.zeros(val.shape, F32)
    for j2 in range(n):
        c = val[j2:j2 + 1, :]
        before = (c > val) | ((c == val) & (rowid > j2))
        rank = rank + jnp.where(before, 1.0, 0.0)
    return rank


MW = MOBA_HEADS * HEAD_DIM
KW = NSA_GROUPS * HEAD_DIM
R_MQ, R_MK, R_MV, R_NQ = 0, MW, 2 * MW, 3 * MW
R_CK = 4 * MW
R_CV, R_SK, R_SV, R_WK, R_WV, R_END = (R_CK + j * KW for j in range(1, 7))
NG_ROWS = -(-3 * NSA_HEADS // 16) * 16


def _inproj_kernel(x_ref, g_ref, w1t_ref, wngt_ref, wmg_ref, gcol_ref,
                   mqt_ref, mkvt_ref, mk_ref, mvt_ref, nqt_ref, ckvt_ref, ck_ref, cv_ref, skvt_ref, sk_ref, svt_ref,
                   wkvt_ref, wk_ref, wvt_ref, ngt_ref, mg_ref, ksum_ref):
    x = x_ref[0]
    xn = x * lax.rsqrt(jnp.mean(x * x, axis=-1, keepdims=True) + RMS_EPS) * g_ref[...]
    mg_ref[...] = jax.nn.sigmoid(_dot(xn.astype(BF16), wmg_ref[...])).astype(BF16)
    xnt = jnp.transpose(xn).astype(BF16)
    ngt_ref[0] = jax.nn.sigmoid(_dot(wngt_ref[...], xnt))
    yt = _dot(w1t_ref[...], xnt)

    def normed(r0, n_heads):
        outs = []
        for j in range(n_heads):
            seg = yt[r0 + j * HEAD_DIM:r0 + (j + 1) * HEAD_DIM, :]
            ss = jnp.sum(seg * seg, axis=0, keepdims=True)
            outs.append(seg * lax.rsqrt(ss * (1.0 / HEAD_DIM) + RMS_EPS)
                        * gcol_ref[r0 + j * HEAD_DIM:r0 + (j + 1) * HEAD_DIM, :])
        return jnp.concatenate(outs, axis=0)

    mqt_ref[0] = (normed(R_MQ, MOBA_HEADS) * QK_SCALE).astype(BF16)
    mkt = normed(R_MK, MOBA_HEADS)
    mkvt_ref[0, 0:MW, :] = mkt
    mkvt_ref[0, MW:2 * MW, :] = yt[R_MV:R_NQ, :]
    mk = jnp.transpose(mkt)
    mk_ref[0] = mk.astype(BF16)
    ksum_ref[0] = jnp.sum(mk, axis=0, keepdims=True)
    mvt_ref[0] = yt[R_MV:R_NQ, :].astype(BF16)
    nqt_ref[0] = (normed(R_NQ, NSA_HEADS) * QK_SCALE).astype(BF16)
    ckvt_ref[0] = yt[R_CK:R_SK, :]
    ck_ref[0] = jnp.transpose(yt[R_CK:R_CV, :])
    cv_ref[0] = jnp.transpose(yt[R_CV:R_SK, :])
    skt = normed(R_SK, NSA_GROUPS)
    skvt_ref[0, 0:KW, :] = skt
    skvt_ref[0, KW:2 * KW, :] = yt[R_SV:R_WK, :]
    sk_ref[0] = jnp.transpose(skt).astype(BF16)
    svt_ref[0] = yt[R_SV:R_WK, :].astype(BF16)
    wkt = normed(R_WK, NSA_GROUPS)
    wkvt_ref[0, 0:KW, :] = wkt
    wkvt_ref[0, KW:2 * KW, :] = yt[R_WV:R_END, :]
    wk_ref[0] = jnp.transpose(wkt).astype(BF16)
    wvt_ref[0] = yt[R_WV:R_END, :].astype(BF16)


def _inproj(x, g_attn, w1t, wngt, wmg, gcol):
    n, length, d = x.shape
    nt = length // TM
    ft = lambda rows: pl.BlockSpec((1, rows, TM), lambda b, i: (b, 0, i))
    rw = lambda w: pl.BlockSpec((1, TM, w), lambda b, i: (b, i, 0))
    fts = lambda rows, dt: jax.ShapeDtypeStruct((n, rows, length), dt)
    rws = lambda w, dt: jax.ShapeDtypeStruct((n, length, w), dt)
    cst = lambda a: pl.BlockSpec(a.shape, lambda b, i, nd=a.ndim: (0,) * nd)
    out_specs = (ft(MW), ft(2 * MW), rw(MW), ft(MW), ft(MW), ft(2 * KW), rw(KW), rw(KW), ft(2 * KW), rw(KW), ft(KW),
                 ft(2 * KW), rw(KW), ft(KW), ft(NG_ROWS),
                 pl.BlockSpec((TM, 2 * d), lambda b, i: (b * nt + i, 0)),
                 pl.BlockSpec((1, 1, MW), lambda b, i: (b * nt + i, 0, 0)))
    out_shape = (fts(MW, BF16), fts(2 * MW, F32), rws(MW, BF16), fts(MW, BF16), fts(MW, BF16), fts(2 * KW, F32),
                 rws(KW, F32), rws(KW, F32), fts(2 * KW, F32), rws(KW, BF16), fts(KW, BF16),
                 fts(2 * KW, F32), rws(KW, BF16), fts(KW, BF16), fts(NG_ROWS, F32),
                 jax.ShapeDtypeStruct((n * length, 2 * d), BF16), jax.ShapeDtypeStruct((n * nt, 1, MW), F32))
    names = ("mqt", "mkvt", "mk", "mvt", "nqt", "ckvt", "ck", "cv", "skvt", "sk", "svt", "wkvt", "wk", "wvt", "ngt",
             "mg", "ksum")
    outs = pl.pallas_call(
        _inproj_kernel, grid=(n, nt),
        in_specs=[pl.BlockSpec((1, TM, d), lambda b, i: (b, i, 0)), cst(g_attn), cst(w1t), cst(wngt), cst(wmg),
                  cst(gcol)],
        out_specs=out_specs, out_shape=out_shape,
        compiler_params=_cparams(("arbitrary", "arbitrary")), name="inproj",
    )(x, g_attn, w1t, wngt, wmg, gcol)
    return dict(zip(names, outs))


def _alibi_tables(slopes):
    rel = (np.arange(TQ)[:, None] - np.arange(TQ)[None, :]).astype(np.float32)
    bias = np.asarray(slopes, np.float32)[:, None, None] * rel[None]
    diag = np.where(rel[None] <= 0, bias, np.float32(NEG))
    edge = np.where(rel[None] >= 0, bias, np.float32(NEG))
    return jnp.asarray(bias), jnp.asarray(diag), jnp.asarray(edge)


def _moba_prompt_kernel(qt_ref, k_ref, vt_ref, mean_ref, bias_ref, diag_ref, o_ref,
                        qz_sc, sel_sc, m_sc, l_sc, acc_sc, s_sc, p_sc, *, nb):
    i = pl.program_id(1)
    npair = MOBA_HEADS // 2
    zeros = jnp.zeros((HEAD_DIM, TQ), BF16)
    for pair in range(npair):
        r0 = 2 * pair * HEAD_DIM
        qz_sc[pair, :, 0:TQ] = jnp.concatenate([qt_ref[0, r0:r0 + HEAD_DIM, :], zeros], axis=0)
        qz_sc[pair, :, TQ:2 * TQ] = jnp.concatenate([zeros, qt_ref[0, r0 + HEAD_DIM:r0 + 2 * HEAD_DIM, :]], axis=0)
    m_sc[...] = jnp.full(m_sc.shape, NEG, F32)
    l_sc[...] = jnp.zeros(l_sc.shape, F32)
    acc_sc[...] = jnp.zeros(acc_sc.shape, F32)

    nbp = mean_ref.shape[1]
    rowb = lax.broadcasted_iota(jnp.int32, (nbp, TQ), 0)
    cand = rowb < i
    for h in range(MOBA_HEADS):
        pair, half = divmod(h, 2)
        qz = qz_sc[pair, :, half * TQ:(half + 1) * TQ]
        mh, ml = _split_bf16(mean_ref[0, :, pair * LANES:(pair + 1) * LANES])
        ss = _dot(mh, qz) + _dot(ml, qz)
        rank = _top_rank_rows(jnp.where(cand, ss, -jnp.inf), nb, rowb)
        sel_sc[h] = jnp.where(cand & (rank < MOBA_TOPK), 1.0, 0.0)

    def kv_block(kb, own):
        start = pl.multiple_of(kb * TQ, TQ)
        k_blk = k_ref[0, pl.ds(start, TQ), :]
        vt_blk = vt_ref[0, :, pl.ds(start, TQ)]
        off = ((kb - i) * TQ).astype(F32)
        tab_ref = diag_ref if own else bias_ref
        bms = []
        for pair in range(npair):
            s2 = _dot(k_blk[:, pair * LANES:(pair + 1) * LANES], qz_sc[pair])
            for half in range(2):
                h = 2 * pair + half
                s = s2[:, half * TQ:(half + 1) * TQ] + tab_ref[h]
                s_sc[h] = s
                bms.append(jnp.max(s, axis=0, keepdims=True))
        alphas = []
        for h in range(MOBA_HEADS):
            m_old = m_sc[h:h + 1, :]
            if own:
                m_new = jnp.maximum(m_old, bms[h])
                shift = m_new
            else:
                c = MOBA_SLOPES[h] * off
                sel = sel_sc[h, pl.ds(kb, 1), :] > 0.5
                m_new = jnp.maximum(m_old, jnp.where(sel, bms[h] + c, NEG))
                shift = jnp.where(sel, m_new - c, BIG)
            alpha = jnp.exp2(m_old - m_new)
            psum = jnp.zeros((8, TQ), F32)
            for r0 in range(0, TQ, KEY_CHUNK):
                rs = slice(r0, r0 + KEY_CHUNK)
                p = jnp.exp2(s_sc[h, rs, :] - shift)
                psum = psum + jnp.sum(p.reshape(KEY_CHUNK // 8, 8, TQ), axis=0)
                p_sc[h, rs, :] = p.astype(BF16)
            l_sc[h:h + 1, :] = alpha * l_sc[h:h + 1, :] + jnp.sum(psum, axis=0, keepdims=True)
            m_sc[h:h + 1, :] = m_new
            alphas.append(alpha)
        for h in range(MOBA_HEADS):
            rows = slice(h * HEAD_DIM, (h + 1) * HEAD_DIM)
            acc_sc[rows, :] = alphas[h] * acc_sc[rows, :] + _dot(vt_blk[rows, :], p_sc[h])

    def body(kb, carry):
        kv_block(kb, False)
        return carry

    lax.fori_loop(0, i, body, 0)
    kv_block(i, True)

    for h in range(MOBA_HEADS):
        rows = slice(h * HEAD_DIM, (h + 1) * HEAD_DIM)
        acc_sc[rows, :] = acc_sc[rows, :] / l_sc[h:h + 1, :]
    o_ref[0] = jnp.transpose(acc_sc[...]).astype(BF16)


def _moba_prompt(qt, k, vt, means, bias, diag):
    b, w, s = qt.shape
    nb = s // TQ
    return pl.pallas_call(
        functools.partial(_moba_prompt_kernel, nb=nb), grid=(b, nb),
        in_specs=[pl.BlockSpec((1, w, TQ), lambda bi, i: (bi, 0, i)),
                  pl.BlockSpec((1, s, w), lambda bi, i: (bi, 0, 0)),
                  pl.BlockSpec((1, w, s), lambda bi, i: (bi, 0, 0)),
                  pl.BlockSpec((1,) + means.shape[1:], lambda bi, i: (bi, 0, 0)),
                  _const_spec(bias.shape), _const_spec(diag.shape)],
        out_specs=pl.BlockSpec((1, TQ, w), lambda bi, i: (bi, i, 0)),
        out_shape=jax.ShapeDtypeStruct((b, s, w), BF16),
        scratch_shapes=[pltpu.VMEM((MOBA_HEADS // 2, LANES, 2 * TQ), BF16),
                        pltpu.VMEM((MOBA_HEADS, means.shape[1], TQ), F32),
                        pltpu.VMEM((MOBA_HEADS, TQ), F32), pltpu.VMEM((MOBA_HEADS, TQ), F32),
                        pltpu.VMEM((w, TQ), F32), pltpu.VMEM((MOBA_HEADS, TQ, TQ), F32),
                        pltpu.VMEM((MOBA_HEADS, TQ, TQ), BF16)],
        compiler_params=_cparams(("arbitrary", "arbitrary")), name="moba_prompt",
    )(qt, k, vt, means, bias, diag)


def _compress_core(load_rows, ncp, wfull_ref, pe_ref, w1_ref, b1_ref, w2_ref, b2_ref, gk_ref, bd_ref, ck_ref, cv_ref):
    kw = NSA_GROUPS * HEAD_DIM
    hw = wfull_ref.shape[2]
    projs = []
    for kv in range(2):
        acc = jnp.zeros((ncp, hw), F32)
        for t in range(0, CMP_STRIDE, 2):
            x2 = jnp.concatenate([load_rows(kv, t), load_rows(kv, t + 1)], axis=1).astype(BF16)
            acc = acc + _dot(x2, wfull_ref[kv, t * kw:(t + 2) * kw, :])
        projs.append(acc)
    proj = jnp.concatenate(projs, axis=1)
    hids = []
    for kvg in range(2 * NSA_GROUPS):
        kv = kvg // NSA_GROUPS
        pe_term = _dot(pe_ref[kv], w1_ref[kv])[0:1, :]
        c0 = kvg * 2 * CMP_HIDDEN
        p0 = proj[:, c0:c0 + CMP_HIDDEN]
        p1 = pltpu.roll(proj[:, c0 + CMP_HIDDEN:c0 + 2 * CMP_HIDDEN], ncp - 1, 0)
        hids.append(jax.nn.gelu(b1_ref[kv:kv + 1, :] + p0 + p1 + pe_term))
    hid = jnp.concatenate(hids, axis=1).astype(BF16)
    out = _dot(hid, w2_ref[...]) + b2_ref[...]
    ck_ref[0] = _seg_rms(out[:, 0:kw], bd_ref[...], gk_ref[...]).astype(BF16)
    cv_ref[0] = out[:, kw:2 * kw].astype(BF16)


def _compress_prompt_kernel(xk_ref, xv_ref, *refs):
    ncp = xk_ref.shape[1] // CMP_STRIDE
    x_refs = (xk_ref, xv_ref)
    _compress_core(lambda kv, t: x_refs[kv][0, pl.ds(t, ncp, stride=CMP_STRIDE), :], ncp, *refs)


def _compress_sample_kernel(pt_ref, *refs, n_pages):
    pg_refs = refs[:n_pages]
    x_scs = refs[-2:]
    kw = NSA_GROUPS * HEAD_DIM
    for j in range(n_pages):
        for kv in range(2):
            x_scs[kv][j * PAGE:(j + 1) * PAGE, :] = jnp.transpose(pg_refs[j][0, kv * kw:(kv + 1) * kw, :])
    ncp = n_pages * PAGE // CMP_STRIDE
    _compress_core(lambda kv, t: x_scs[kv][pl.ds(t, ncp, stride=CMP_STRIDE), :], ncp, *refs[n_pages:-2])


def _compress_consts(cw):
    return (cw["wfull"], cw["pe"], cw["w1"], cw["b1"], cw["w2bd"], cw["b2row"], cw["gk"], cw["bd"])


def _compress_prompt(xk, xv, cw):
    b, s, kw = xk.shape
    ncp = s // CMP_STRIDE
    consts = _compress_consts(cw)
    xspec = pl.BlockSpec((1, s, kw), lambda bi: (bi, 0, 0))
    ospec = pl.BlockSpec((1, ncp, kw), lambda bi: (bi, 0, 0))
    return pl.pallas_call(
        _compress_prompt_kernel, grid=(b,),
        in_specs=[xspec, xspec] + [_const_spec(a.shape) for a in consts],
        out_specs=(ospec, ospec),
        out_shape=(jax.ShapeDtypeStruct((b, ncp, kw), BF16), jax.ShapeDtypeStruct((b, ncp, kw), BF16)),
        compiler_params=_cparams(("arbitrary",)), name="nsa_compress_prompt",
    )(xk, xv, *consts)


def _compress_sample(page_table, pool_t, cw):
    db, n_pages = page_table.shape
    fw = pool_t.shape[1]
    ncp = n_pages * PAGE // CMP_STRIDE
    consts = _compress_consts(cw)
    kw = NSA_GROUPS * HEAD_DIM
    cst = lambda a: pl.BlockSpec(a.shape, lambda bi, pt, nd=a.ndim: (0,) * nd)
    ospec = pl.BlockSpec((1, ncp, kw), lambda bi, pt: (bi, 0, 0))
    pg_specs = [pl.BlockSpec((1, fw, PAGE), lambda bi, pt, j=j: (pt[bi, j], 0, 0)) for j in range(n_pages)]
    gs = pltpu.PrefetchScalarGridSpec(
        num_scalar_prefetch=1, grid=(db,), in_specs=pg_specs + [cst(a) for a in consts], out_specs=(ospec, ospec),
        scratch_shapes=[pltpu.VMEM((n_pages * PAGE, kw), F32), pltpu.VMEM((n_pages * PAGE, kw), F32)])
    return pl.pallas_call(
        functools.partial(_compress_sample_kernel, n_pages=n_pages), grid_spec=gs,
        out_shape=(jax.ShapeDtypeStruct((db, ncp, kw), BF16), jax.ShapeDtypeStruct((db, ncp, kw), BF16)),
        compiler_params=_cparams(("arbitrary",)), name="nsa_compress_sample",
    )(page_table, *([pool_t] * n_pages), *consts)


def _nsa_prompt_kernel(qt_ref, ck_ref, cvt_ref, sk_ref, svt_ref, wk_ref, wvt_ref, ng_ref, ovl_ref, cb_ref,
                       bias_ref, diag_ref, edge_ref, o_ref,
                       qz_sc, p_sc, s_sc, sel_sc, oc_sc, ms_sc, ls_sc, as_sc, mw_sc, lw_sc, aw_sc, *, ns):
    i = pl.program_id(1)
    ncp = ck_ref.shape[1]
    zeros = jnp.zeros((HEAD_DIM, TQ), BF16)
    for h in range(NSA_HEADS):
        g, p_ = divmod(h, NSA_HPG)
        qh = qt_ref[0, h * HEAD_DIM:(h + 1) * HEAD_DIM, :]
        qz_sc[g, :, p_ * TQ:(p_ + 1) * TQ] = jnp.concatenate([qh, zeros] if g == 0 else [zeros, qh], axis=0)
    for m_ref, l_ref, a_ref in ((ms_sc, ls_sc, as_sc), (mw_sc, lw_sc, aw_sc)):
        m_ref[...] = jnp.full(m_ref.shape, NEG, F32)
        l_ref[...] = jnp.zeros(l_ref.shape, F32)
        a_ref[...] = jnp.zeros(a_ref.shape, F32)

    qpos = i * TQ + lax.broadcasted_iota(jnp.int32, (1, TQ), 1)
    cend = lax.broadcasted_iota(jnp.int32, (ncp, 1), 0) * CMP_STRIDE + (CMP_LEN - 1)
    cmask = cend <= qpos
    nsr = sel_sc.shape[1]
    rowj = lax.broadcasted_iota(jnp.int32, (nsr, TQ), 0)
    cur = _div_pow2(qpos, SEL_BLOCK)
    forced = (rowj == 0) | (rowj == cur) | (rowj == cur - 1)
    cand = rowj <= cur
    ioff = (i * TQ).astype(F32)
    for g in range(NSA_GROUPS):
        grow = slice(g * HEAD_DIM, (g + 1) * HEAD_DIM)
        s4 = _dot(ck_ref[0], qz_sc[g])
        psum = jnp.zeros((ncp, TQ), F32)
        for p_ in range(NSA_HPG):
            h = g * NSA_HPG + p_
            s = jnp.where(cmask, s4[:, p_ * TQ:(p_ + 1) * TQ] + cb_ref[h] - NSA_SLOPES[h] * ioff, NEG)
            e = jnp.where(cmask, jnp.exp2(s - jnp.max(s, axis=0, keepdims=True)), 0.0)
            den = jnp.sum(e, axis=0, keepdims=True)
            pn = e / jnp.where(den > 0.0, den, 1.0)
            psum = psum + pn
            p_sc[g, 0:ncp, p_ * TQ:(p_ + 1) * TQ] = pn.astype(BF16)
        oc4 = _dot(cvt_ref[0], p_sc[g, 0:ncp, :])
        for p_ in range(NSA_HPG):
            h = g * NSA_HPG + p_
            oc_sc[h * HEAD_DIM:(h + 1) * HEAD_DIM, :] = oc4[grow, p_ * TQ:(p_ + 1) * TQ]
        ph, plo = _split_bf16(psum)
        imp = _dot(ovl_ref[...], ph) + _dot(ovl_ref[...], plo)
        val = jnp.where(cand, jnp.where(forced, jnp.inf, imp), -jnp.inf)
        rank = _top_rank_rows(val, ns, rowj)
        sel_sc[g] = jnp.where(cand & (rank < SEL_TOPN), 1.0, 0.0)

    spb = TQ // SEL_BLOCK

    def attend(k_blk, vt_blk, tab_ref, off, kb_sel, m_ref, l_ref, a_ref):
        blocks = [slice(j * SEL_BLOCK, (j + 1) * SEL_BLOCK) for j in range(spb)]
        parts = []
        for g in range(NSA_GROUPS):
            s4 = _dot(k_blk, qz_sc[g])
            for p_ in range(NSA_HPG):
                h = g * NSA_HPG + p_
                s = s4[:, p_ * TQ:(p_ + 1) * TQ] + tab_ref[h]
                s_sc[h] = s
                parts.append([jnp.max(s[bl, :].reshape(SEL_BLOCK // 8, 8, TQ), axis=0) for bl in blocks])
        alphas = []
        for h in range(NSA_HEADS):
            g, p_ = divmod(h, NSA_HPG)
            c = NSA_SLOPES[h] * off
            m_old = m_ref[h:h + 1, :]
            if kb_sel is None:
                sels = [None] * spb
                part = functools.reduce(jnp.maximum, parts[h])
            else:
                sels = [sel_sc[g, pl.ds(kb_sel * spb + j, 1), :] > 0.5 for j in range(spb)]
                part = functools.reduce(jnp.maximum, [jnp.where(sl, pt, NEG) for sl, pt in zip(sels, parts[h])])
            m_new = jnp.maximum(m_old, jnp.max(part, axis=0, keepdims=True) + c)
            alpha = jnp.exp2(m_old - m_new)
            psum = jnp.zeros((8, TQ), F32)
            for bl, sl in zip(blocks, sels):
                shift = m_new - c if sl is None else jnp.where(sl, m_new - c, BIG)
                p = jnp.exp2(s_sc[h, bl, :] - shift)
                psum = psum + jnp.sum(p.reshape(SEL_BLOCK // 8, 8, TQ), axis=0)
                p_sc[g, bl, p_ * TQ:(p_ + 1) * TQ] = p.astype(BF16)
            l_ref[h:h + 1, :] = alpha * l_ref[h:h + 1, :] + jnp.sum(psum, axis=0, keepdims=True)
            m_ref[h:h + 1, :] = m_new
            alphas.append(alpha)
        for g in range(NSA_GROUPS):
            grow = slice(g * HEAD_DIM, (g + 1) * HEAD_DIM)
            pv4 = _dot(vt_blk, p_sc[g])
            for p_ in range(NSA_HPG):
                h = g * NSA_HPG + p_
                rows = slice(h * HEAD_DIM, (h + 1) * HEAD_DIM)
                a_ref[rows, :] = alphas[h] * a_ref[rows, :] + pv4[grow, p_ * TQ:(p_ + 1) * TQ]

    def sel_tile(kb, tab_ref):
        start = pl.multiple_of(kb * TQ, TQ)
        attend(sk_ref[0, pl.ds(start, TQ), :], svt_ref[0, :, pl.ds(start, TQ)], tab_ref,
               ((kb - i) * TQ).astype(F32), kb, ms_sc, ls_sc, as_sc)

    def win_tile(back, tab_ref):
        kb = jnp.maximum(i - back, 0)
        start = pl.multiple_of(kb * TQ, TQ)
        attend(wk_ref[0, pl.ds(start, TQ), :], wvt_ref[0, :, pl.ds(start, TQ)], tab_ref,
               -float(back * TQ), None, mw_sc, lw_sc, aw_sc)

    def sel_body(kb, carry):
        sel_tile(kb, bias_ref)
        return carry

    lax.fori_loop(0, i, sel_body, 0)
    sel_tile(i, diag_ref)

    @pl.when(i >= 2)
    def _():
        win_tile(2, edge_ref)

    @pl.when(i >= 1)
    def _():
        win_tile(1, bias_ref)

    win_tile(0, diag_ref)

    ng = ng_ref[0]
    for h in range(NSA_HEADS):
        rows = slice(h * HEAD_DIM, (h + 1) * HEAD_DIM)
        o_s = as_sc[rows, :] / ls_sc[h:h + 1, :]
        o_w = aw_sc[rows, :] / lw_sc[h:h + 1, :]
        oc_sc[rows, :] = (ng[h:h + 1, :] * oc_sc[rows, :] + ng[NSA_HEADS + h:NSA_HEADS + h + 1, :] * o_s
                          + ng[2 * NSA_HEADS + h:2 * NSA_HEADS + h + 1, :] * o_w)
    o_ref[0] = jnp.transpose(oc_sc[...]).astype(BF16)


def _nsa_prompt(qt, ck, cvt, sk, svt, wk, wvt, ngt, ovl, cb, bias, diag, edge):
    b, w, s = qt.shape
    assert WINDOW == 2 * TQ
    ns = s // SEL_BLOCK
    kw = NSA_GROUPS * HEAD_DIM
    full = lambda a: pl.BlockSpec((1,) + a.shape[1:], lambda bi, i: (bi, 0, 0))
    hq = NSA_HPG * TQ
    return pl.pallas_call(
        functools.partial(_nsa_prompt_kernel, ns=ns), grid=(b, s // TQ),
        in_specs=[pl.BlockSpec((1, w, TQ), lambda bi, i: (bi, 0, i)), full(ck), full(cvt), full(sk), full(svt),
                  full(wk), full(wvt), pl.BlockSpec((1, ngt.shape[1], TQ), lambda bi, i: (bi, 0, i)),
                  _const_spec(ovl.shape), _const_spec(cb.shape), _const_spec(bias.shape), _const_spec(diag.shape),
                  _const_spec(edge.shape)],
        out_specs=pl.BlockSpec((1, TQ, w), lambda bi, i: (bi, i, 0)),
        out_shape=jax.ShapeDtypeStruct((b, s, w), BF16),
        scratch_shapes=[pltpu.VMEM((NSA_GROUPS, kw, hq), BF16), pltpu.VMEM((NSA_GROUPS, TQ, hq), BF16),
                        pltpu.VMEM((NSA_HEADS, TQ, TQ), F32),
                        pltpu.VMEM((NSA_GROUPS, ovl.shape[0], TQ), F32), pltpu.VMEM((w, TQ), F32),
                        pltpu.VMEM((NSA_HEADS, TQ), F32), pltpu.VMEM((NSA_HEADS, TQ), F32), pltpu.VMEM((w, TQ), F32),
                        pltpu.VMEM((NSA_HEADS, TQ), F32), pltpu.VMEM((NSA_HEADS, TQ), F32), pltpu.VMEM((w, TQ), F32)],
        compiler_params=_cparams(("arbitrary", "arbitrary")), name="nsa_prompt",
    )(qt, ck, cvt, sk, svt, wk, wvt, ngt, ovl, cb, bias, diag, edge)


def _moba_sample_kernel(pt_ref, q_ref, kn_ref, vn_ref, *rest, past, ds, nbs):
    pg_refs = rest[:PAGES_PER_STEP]
    o_ref, m_sc, l_sc, o_sc, ssum_sc = rest[PAGES_PER_STEP:]
    c = pl.program_id(1)
    nch = pl.num_programs(1)
    rows = MOBA_HEADS * ds
    w = MOBA_HEADS * HEAD_DIM
    ppb = MOBA_BLOCK // PAGE
    row_w = lax.broadcasted_iota(jnp.int32, (rows, w), 0)
    diag = _div_pow2(lax.broadcasted_iota(jnp.int32, (rows, w), 1), HEAD_DIM) == _div_pow2(row_w, ds)
    lane = lax.broadcasted_iota(jnp.int32, (rows, LANES), 1)
    row1 = lax.broadcasted_iota(jnp.int32, (rows, 1), 0)
    qi = _mod_pow2(row1, ds)
    slope = jnp.zeros((rows, 1), F32)
    for h in range(MOBA_HEADS):
        slope = jnp.where(_div_pow2(row1, ds) == h, MOBA_SLOPES[h], slope)
    q8 = q_ref[0].astype(F32)
    qbd = jnp.where(diag, jnp.concatenate([q8] * MOBA_HEADS, axis=0), 0.0).astype(BF16)

    @pl.when(c == 0)
    def _():
        m_sc[...] = jnp.full(m_sc.shape, NEG, F32)
        l_sc[...] = jnp.zeros(l_sc.shape, F32)
        ssum_sc[...] = jnp.zeros(ssum_sc.shape, F32)

    nk = PAGES_PER_STEP * PAGE
    col = lax.broadcasted_iota(jnp.int32, (rows, nk), 1)
    m_all, l_all, ssum = m_sc[...], l_sc[...], ssum_sc[...]
    kt = jnp.concatenate([pg_refs[j][0, 0] for j in range(PAGES_PER_STEP)], axis=1).astype(BF16)
    s_raw = _dot(qbd, kt)
    s_all = s_raw + slope * (col - qi + (c * nk - past)).astype(F32)
    for bb in range(PAGES_PER_STEP // ppb):
        blk = c * (PAGES_PER_STEP // ppb) + bb
        cols = slice(bb * MOBA_BLOCK, (bb + 1) * MOBA_BLOCK)
        vt = jnp.concatenate([pg_refs[bb * ppb + j][0, 1] for j in range(ppb)], axis=1).astype(BF16)
        ssum = jnp.where(lane == blk, jnp.sum(s_raw[:, cols], axis=1, keepdims=True), ssum)
        s = s_all[:, cols]
        m_b = jnp.max(s, axis=1, keepdims=True)
        p = jnp.exp2(s - m_b)
        o_sc[blk] = jnp.where(diag, _dot_nt(p.astype(BF16), vt), 0.0)
        m_all = jnp.where(lane == blk, m_b, m_all)
        l_all = jnp.where(lane == blk, jnp.sum(p, axis=1, keepdims=True), l_all)
    m_sc[...] = m_all
    l_sc[...] = l_all
    ssum_sc[...] = ssum

    @pl.when(c == nch - 1)
    def _():
        ss = ssum_sc[...] * (1.0 / MOBA_BLOCK)
        own = past // MOBA_BLOCK
        cand = lane < own
        rank = _top_rank(jnp.where(cand, ss, -jnp.inf), nbs, lane)
        selb = cand & (rank < MOBA_TOPK)
        m_all = m_sc[...]
        m_past = jnp.max(jnp.where(selb, m_all, NEG), axis=1, keepdims=True)
        zpad = jnp.zeros((LANES - ds, w), F32)
        k_new = jnp.concatenate([kn_ref[0], zpad], axis=0).astype(BF16)
        v_new = jnp.concatenate([vn_ref[0], zpad], axis=0).astype(BF16)
        s_own = _dot_nt(qbd, k_new) + slope * (lane - qi).astype(F32)
        s_own = jnp.where(lane <= qi, s_own, NEG)
        m_tot = jnp.maximum(m_past, jnp.max(s_own, axis=1, keepdims=True))
        wgt = jnp.where(selb, jnp.exp2(m_all - m_tot), 0.0)
        p_own = jnp.exp2(s_own - m_tot)
        den = jnp.sum(wgt * l_sc[...], axis=1, keepdims=True) + jnp.sum(p_own, axis=1, keepdims=True)
        out = jnp.where(diag, _dot(p_own.astype(BF16), v_new), 0.0)
        for b2 in range(nbs):
            out = out + wgt[:, b2:b2 + 1] * o_sc[b2]
        out = out / den
        o8 = out[0:ds, :]
        for h in range(1, MOBA_HEADS):
            o8 = o8 + out[h * ds:(h + 1) * ds, :]
        o_ref[0] = o8.astype(BF16)


def _moba_sample(page_table, q, k_new, v_new, pool_t):
    db, ds, w = q.shape
    n_pages = page_table.shape[1]
    past = n_pages * PAGE
    nbs = past // MOBA_BLOCK
    assert n_pages % PAGES_PER_STEP == 0 and past % MOBA_BLOCK == 0 and nbs <= LANES and ds == 8
    tok = pl.BlockSpec((1, ds, w), lambda bi, c, pt: (bi, 0, 0))
    pg_specs = [pl.BlockSpec((1, 2, w, PAGE), lambda bi, c, pt, j=j: (pt[bi, c * PAGES_PER_STEP + j], 0, 0, 0))
                for j in range(PAGES_PER_STEP)]
    gs = pltpu.PrefetchScalarGridSpec(
        num_scalar_prefetch=1, grid=(db, n_pages // PAGES_PER_STEP),
        in_specs=[tok, tok, tok] + pg_specs, out_specs=tok,
        scratch_shapes=[pltpu.VMEM((MOBA_HEADS * ds, LANES), F32), pltpu.VMEM((MOBA_HEADS * ds, LANES), F32),
                        pltpu.VMEM((nbs, MOBA_HEADS * ds, w), F32), pltpu.VMEM((MOBA_HEADS * ds, LANES), F32)])
    return pl.pallas_call(
        functools.partial(_moba_sample_kernel, past=past, ds=ds, nbs=nbs), grid_spec=gs,
        out_shape=jax.ShapeDtypeStruct((db, ds, w), BF16),
        compiler_params=_cparams(("arbitrary", "arbitrary")), name="moba_sample",
    )(page_table, q, k_new, v_new, *([pool_t] * PAGES_PER_STEP))


def _nsa_sample_kernel(pt_ref, qg_ref, ck_ref, cv_ref, skn_ref, wkn_ref, wst_ref, ng_ref, ovl_ref, exp_ref,
                       *rest, past, ds, n_pages, ns):
    pg_refs = rest[:n_pages]
    o_ref = rest[n_pages]
    rows = NSA_HEADS * ds
    grows = NSA_HPG * ds
    kw = NSA_GROUPS * HEAD_DIM
    ncp = ck_ref.shape[1]
    nsp = ovl_ref.shape[1]
    wb = wst_ref.shape[2]
    row1 = lax.broadcasted_iota(jnp.int32, (rows, 1), 0)
    qi = _mod_pow2(row1, ds)
    qpos = past + qi
    slope = jnp.zeros((rows, 1), F32)
    for h in range(NSA_HEADS):
        slope = jnp.where(_div_pow2(row1, ds) == h, NSA_SLOPES[h], slope)
    lane = lax.broadcasted_iota(jnp.int32, (rows, LANES), 1)
    grp_half = _div_pow2(lane, HEAD_DIM) == _div_pow2(row1, grows)
    q_grp = qg_ref[0]

    def softmax_parts(s_list, m_list):
        m = None
        for s, mk in zip(s_list, m_list):
            mm = jnp.max(jnp.where(mk, s, NEG), axis=1, keepdims=True)
            m = mm if m is None else jnp.maximum(m, mm)
        ps = [jnp.where(mk, jnp.exp2(jnp.where(mk, s, NEG) - m), 0.0) for s, mk in zip(s_list, m_list)]
        den = sum(jnp.sum(p, axis=1, keepdims=True) for p in ps)
        return ps, jnp.where(den > 0.0, den, 1.0)

    def own_group(o):
        return jnp.where(grp_half, o, pltpu.roll(o, HEAD_DIM, 1))

    cend = lax.broadcasted_iota(jnp.int32, (rows, ncp), 1) * CMP_STRIDE + (CMP_LEN - 1)
    cmask = cend <= qpos
    s_c = _dot_nt(q_grp, ck_ref[0]) + slope * (cend - qpos).astype(F32)
    (e,), den = softmax_parts([s_c], [cmask])
    pn = e / den
    o_c = own_group(_dot(pn.astype(BF16), cv_ref[0]))
    lane_s = lax.broadcasted_iota(jnp.int32, (ds, nsp), 1)
    cur = _div_pow2(past + lax.broadcasted_iota(jnp.int32, (ds, 1), 0), SEL_BLOCK)
    forced = (lane_s == 0) | (lane_s == cur) | (lane_s == cur - 1)
    cand = lane_s <= cur
    sel_parts = []
    for g in range(NSA_GROUPS):
        psum = pn[g * grows:g * grows + ds]
        for p_ in range(1, NSA_HPG):
            psum = psum + pn[g * grows + p_ * ds:g * grows + (p_ + 1) * ds]
        ph, plo = _split_bf16(psum)
        imp = _dot(ph, ovl_ref[...]) + _dot(plo, ovl_ref[...])
        val = jnp.where(cand, jnp.where(forced, jnp.inf, imp), -jnp.inf)
        rank = _top_rank(val, ns, lane_s)
        sel_g = jnp.where(cand & (rank < SEL_TOPN), 1.0, 0.0)
        sel_parts.append(jnp.concatenate([sel_g] * NSA_HPG, axis=0))
    sel_f32 = jnp.concatenate(sel_parts, axis=0)
    sel_rows = sel_f32.astype(BF16)

    kt_all = jnp.concatenate([r[0, 0:kw, :] for r in pg_refs], axis=1).astype(BF16)
    vt_all = jnp.concatenate([r[0, kw:2 * kw, :] for r in pg_refs], axis=1).astype(BF16)
    kpos = lax.broadcasted_iota(jnp.int32, (rows, past), 1)
    s_past = _dot(q_grp, kt_all) + slope * (kpos - qpos).astype(F32)
    m_past = _dot(sel_rows, exp_ref[...]) > 0.5
    zpad = jnp.zeros((LANES - ds, 2 * kw), F32)
    skn = jnp.concatenate([skn_ref[0], zpad], axis=0)
    s_new = _dot_nt(q_grp, skn[:, 0:kw].astype(BF16)) + slope * (lane - qi).astype(F32)
    new_blk = past // SEL_BLOCK
    m_new = (lane <= qi) & (sel_f32[:, new_blk:new_blk + 1] > 0.5)
    (p_past, p_new), den = softmax_parts([s_past, s_new], [m_past, m_new])
    o_s = _dot_nt(p_past.astype(BF16), vt_all) + _dot(p_new.astype(BF16), skn[:, kw:2 * kw].astype(BF16))
    o_s = own_group(o_s / den)

    jst = lax.broadcasted_iota(jnp.int32, (rows, wb), 1)
    dist_st = qpos - (past - wb + jst)
    s_st = _dot(q_grp, wst_ref[0, 0:kw, :].astype(BF16)) - slope * dist_st.astype(F32)
    m_st = (dist_st >= 0) & (dist_st <= WINDOW)
    wkn = jnp.concatenate([wkn_ref[0], zpad], axis=0)
    s_wn = _dot_nt(q_grp, wkn[:, 0:kw].astype(BF16)) + slope * (lane - qi).astype(F32)
    m_wn = lane <= qi
    (p_st, p_wn), den = softmax_parts([s_st, s_wn], [m_st, m_wn])
    o_w = _dot_nt(p_st.astype(BF16), wst_ref[0, kw:2 * kw, :].astype(BF16)) \
        + _dot(p_wn.astype(BF16), wkn[:, kw:2 * kw].astype(BF16))
    o_w = own_group(o_w / den)

    ng = ng_ref[0]
    gate = lambda br: jnp.concatenate([ng[:, br * NSA_HEADS + h:br * NSA_HEADS + h + 1] for h in range(NSA_HEADS)],
                                      axis=0)
    o = gate(0) * o_c + gate(1) * o_s + gate(2) * o_w
    lane8 = lax.broadcasted_iota(jnp.int32, (ds, LANES), 1)
    for pair in range(NSA_HEADS // 2):
        o_even = o[(2 * pair) * ds:(2 * pair + 1) * ds]
        o_odd = o[(2 * pair + 1) * ds:(2 * pair + 2) * ds]
        o_ref[0, :, pair * LANES:(pair + 1) * LANES] = jnp.where(lane8 < HEAD_DIM, o_even, o_odd).astype(BF16)


def _nsa_sample(page_table, q_grp, ck, cv, sk_new, wk_new, wstate_t, ng, ovl, expand, pool_t):
    db, rows, _ = q_grp.shape
    ds = rows // NSA_HEADS
    n_pages = page_table.shape[1]
    past = n_pages * PAGE
    ns = -(-(past + ds) // SEL_BLOCK)
    fw = pool_t.shape[1]
    per_b = lambda a: pl.BlockSpec((1,) + a.shape[1:], lambda bi, pt: (bi, 0, 0))
    cst = lambda a: pl.BlockSpec(a.shape, lambda bi, pt, nd=a.ndim: (0,) * nd)
    pg_specs = [pl.BlockSpec((1, fw, PAGE), lambda bi, pt, j=j: (pt[bi, j], 0, 0)) for j in range(n_pages)]
    gs = pltpu.PrefetchScalarGridSpec(
        num_scalar_prefetch=1, grid=(db,),
        in_specs=[per_b(q_grp), per_b(ck), per_b(cv), per_b(sk_new), per_b(wk_new), per_b(wstate_t),
                  per_b(ng), cst(ovl), cst(expand)] + pg_specs,
        out_specs=pl.BlockSpec((1, ds, NSA_HEADS * HEAD_DIM), lambda bi, pt: (bi, 0, 0)))
    return pl.pallas_call(
        functools.partial(_nsa_sample_kernel, past=past, ds=ds, n_pages=n_pages, ns=ns), grid_spec=gs,
        out_shape=jax.ShapeDtypeStruct((db, ds, NSA_HEADS * HEAD_DIM), BF16),
        compiler_params=_cparams(("arbitrary",)), name="nsa_sample",
    )(page_table, q_grp, ck, cv, sk_new, wk_new, wstate_t, ng, ovl, expand, *([pool_t] * n_pages))


def _merged_h(x_ref, om_ref, on_ref, mg_ref, wa_ref, wb_ref, wo_ref):
    d = x_ref.shape[1]
    a = _dot(om_ref[...], wa_ref[...])
    b = _dot(on_ref[...], wb_ref[...])
    merged = mg_ref[:, 0:d] * a + mg_ref[:, d:2 * d] * b
    return x_ref[...] + _dot(merged.astype(BF16), wo_ref[...])


def _merge_kernel(*refs):
    refs[-1][...] = _merged_h(*refs[:-1])


def _merge(x2d, o_m, o_n, mg, wa, wb, wo):
    t, d = x2d.shape
    row = lambda w: pl.BlockSpec((TM, w), lambda i: (i, 0))
    return pl.pallas_call(
        _merge_kernel, grid=(t // TM,),
        in_specs=[row(d), row(o_m.shape[1]), row(o_n.shape[1]), row(2 * d),
                  _const_spec(wa.shape), _const_spec(wb.shape), _const_spec(wo.shape)],
        out_specs=row(d), out_shape=jax.ShapeDtypeStruct((t, d), F32),
        compiler_params=_cparams(("arbitrary",)), name="merge",
    )(x2d, o_m, o_n, mg, wa, wb, wo)


def _ffn_kernel(*refs, seq_len, per_seq_state):
    if per_seq_state:
        h = _merged_h(*refs[:7])
        st_ref, gf_ref, win_ref, cw_ref, cb_ref, wd_ref, y_ref, cs_ref, carry_sc = refs[7:]
    else:
        h = refs[0][...]
        st0_ref, st1_ref, gf_ref, win_ref, cw_ref, cb_ref, wd_ref, y_ref, g_ref = refs[1:]
    f = cw_ref.shape[1]
    hn = (h * lax.rsqrt(jnp.mean(h * h, axis=-1, keepdims=True) + RMS_EPS) * gf_ref[...]).astype(BF16)
    gu = _dot(hn, win_ref[...])
    g = gu[:, 0:f]
    u = gu[:, f:2 * f]
    tm = h.shape[0]
    idx = _mod_pow2(lax.broadcasted_iota(jnp.int32, (tm, 1), 0), seq_len)
    if per_seq_state:
        @pl.when(pl.program_id(1) == 0)
        def _():
            carry_sc[0:CONV_W - 1, :] = st_ref[0]
        st0 = carry_sc[0:1, :]
        st1 = carry_sc[1:2, :]
    else:
        st0 = st0_ref[...]
        st1 = st1_ref[...]
    g1 = pltpu.roll(g, 1, 0)
    g2 = pltpu.roll(g, 2, 0)
    prev1 = jnp.where(idx == 0, st1, g1)
    prev2 = jnp.where(idx == 0, st0, jnp.where(idx == 1, st1, g2))
    gc = cb_ref[...] + prev2 * cw_ref[0:1, :] + prev1 * cw_ref[1:2, :] + g * cw_ref[2:3, :]
    act = (jax.nn.silu(gc) * u).astype(BF16)
    y_ref[...] = h + _dot(act, wd_ref[...])
    if per_seq_state:
        carry_sc[0:CONV_W - 1, :] = g[tm - (CONV_W - 1):tm, :]
        cs_ref[0] = g[tm - (CONV_W - 1):tm, :]
    else:
        g_ref[...] = g


def _merge_ffn_prompt(x2d, o_m, o_n, mg, wa, wb, wo, n, conv_state, gf, win, cw, cb, wd):
    t, d = x2d.shape
    f = cw.shape[1]
    nt = (t // n) // FFN_TM
    roww = lambda w: pl.BlockSpec((FFN_TM, w), lambda bi, i: (bi * nt + i, 0))
    row = roww(d)
    st = pl.BlockSpec((1, CONV_W - 1, f), lambda bi, i: (bi, 0, 0))
    cst = lambda a: pl.BlockSpec(a.shape, lambda bi, i, nd=a.ndim: (0,) * nd, pipeline_mode=pl.Buffered(1))
    return pl.pallas_call(
        functools.partial(_ffn_kernel, seq_len=FFN_TM, per_seq_state=True), grid=(n, nt),
        in_specs=[row, roww(o_m.shape[1]), roww(o_n.shape[1]), roww(2 * d), cst(wa), cst(wb), cst(wo),
                  st, cst(gf), cst(win), cst(cw), cst(cb), cst(wd)],
        out_specs=(row, st),
        out_shape=(jax.ShapeDtypeStruct((t, d), F32), jax.ShapeDtypeStruct((n, CONV_W - 1, f), F32)),
        scratch_shapes=[pltpu.VMEM((8, f), F32)],
        compiler_params=_cparams(("arbitrary", "arbitrary")), name="ffn_prompt",
    )(x2d, o_m, o_n, mg, wa, wb, wo, conv_state, gf, win, cw, cb, wd)


def _ffn_sample(h2d, seq_len, st0, st1, gf, win, cw, cb, wd):
    t, d = h2d.shape
    f = cw.shape[1]
    row = lambda w: pl.BlockSpec((TM, w), lambda i: (i, 0))
    return pl.pallas_call(
        functools.partial(_ffn_kernel, seq_len=seq_len, per_seq_state=False), grid=(t // TM,),
        in_specs=[row(d), row(f), row(f), _const_spec(gf.shape), _const_spec(win.shape), _const_spec(cw.shape),
                  _const_spec(cb.shape), _const_spec(wd.shape)],
        out_specs=(row(d), row(f)),
        out_shape=(jax.ShapeDtypeStruct((t, d), F32), jax.ShapeDtypeStruct((t, f), F32)),
        compiler_params=_cparams(("arbitrary",)), name="ffn_sample",
    )(h2d, st0, st1, gf, win, cw, cb, wd)


def _block_diag_ones(n, blk):
    i = np.arange(n)
    return jnp.asarray((i[:, None] // blk == i[None, :] // blk).astype(np.float32), BF16)


def _overlap_matrix(ncp, nc, ns, nsp):
    c0 = np.arange(ncp) * CMP_STRIDE
    j0 = np.arange(nsp) * SEL_BLOCK
    m = (c0[:, None] < j0[None, :] + SEL_BLOCK) & (c0[:, None] + CMP_LEN > j0[None, :])
    m &= (np.arange(ncp)[:, None] < nc) & (np.arange(nsp)[None, :] < ns)
    return m.astype(np.float32)


def _layer_weights(l, attn_norm_g, w_in, qk_norm_g, nsa_cmp_pe, nsa_cmp_w1, nsa_cmp_b1, nsa_cmp_w2, nsa_cmp_b2,
                   w_branch_moba, w_branch_nsa, w_out, ffn_norm_g, w_ffn_in, ffn_conv_w, ffn_conv_b, w_ffn_down):
    d = w_in.shape[1]
    wi = w_in[l]
    kvw = KW
    o_mg = R_END + 3 * NSA_HEADS
    w1t = jnp.transpose(wi[:, :R_END]).astype(BF16)
    wngt = jnp.pad(jnp.transpose(wi[:, R_END:o_mg]), ((0, NG_ROWS - 3 * NSA_HEADS), (0, 0))).astype(BF16)
    wmg = wi[:, o_mg:o_mg + 2 * d].astype(BF16)
    g = qk_norm_g[l]
    ones = lambda n: jnp.ones((n,), F32)
    gcol = jnp.concatenate([jnp.tile(g[0], MOBA_HEADS), jnp.tile(g[1], MOBA_HEADS), ones(MW), jnp.tile(g[2], NSA_HEADS),
                            ones(2 * KW), jnp.tile(g[4], NSA_GROUPS), ones(KW), jnp.tile(g[5], NSA_GROUPS),
                            ones(KW)])[:, None]
    r = CMP_LEN // CMP_STRIDE
    w1c = nsa_cmp_w1[l]
    w1r = w1c.reshape(2, r, CMP_STRIDE, HEAD_DIM, CMP_HIDDEN)
    nkvg = 2 * NSA_GROUPS
    wfull = jnp.zeros((CMP_STRIDE, nkvg, HEAD_DIM, nkvg, r, CMP_HIDDEN), F32)
    for kvg in range(nkvg):
        wfull = wfull.at[:, kvg, :, kvg].set(jnp.transpose(w1r[kvg // NSA_GROUPS], (1, 2, 0, 3)))
    wfull = wfull.reshape(CMP_STRIDE, 2, kvw, 2, NSA_GROUPS * r * CMP_HIDDEN)
    wfull = jnp.stack([wfull[:, kv, :, kv] for kv in range(2)]).reshape(2, CMP_STRIDE * kvw, -1).astype(BF16)
    w2bd = jnp.zeros((nkvg, CMP_HIDDEN, nkvg, HEAD_DIM), F32)
    for kvg in range(nkvg):
        w2bd = w2bd.at[kvg, :, kvg].set(nsa_cmp_w2[l][kvg // NSA_GROUPS])
    w2bd = w2bd.reshape(nkvg * CMP_HIDDEN, nkvg * HEAD_DIM).astype(BF16)
    b2 = nsa_cmp_b2[l]
    cw = dict(
        wfull=wfull,
        pe=jnp.broadcast_to(nsa_cmp_pe[l].reshape(2, 1, CMP_LEN * HEAD_DIM), (2, 8, CMP_LEN * HEAD_DIM)).astype(BF16),
        w1=w1c.astype(BF16), b1=nsa_cmp_b1[l], w2bd=w2bd,
        b2row=jnp.concatenate([jnp.tile(b2[0], NSA_GROUPS), jnp.tile(b2[1], NSA_GROUPS)])[None, :],
        gk=jnp.tile(g[3], NSA_GROUPS)[None, :], bd=_block_diag_ones(kvw, HEAD_DIM))
    return dict(
        g_attn=attn_norm_g[l][None, :], w1t=w1t, wngt=wngt, wmg=wmg, gcol=gcol, cw=cw,
        wa=w_branch_moba[l].astype(BF16), wb=w_branch_nsa[l].astype(BF16), wo=w_out[l].astype(BF16),
        gf=ffn_norm_g[l][None, :], win=w_ffn_in[l].astype(BF16), cwt=ffn_conv_w[l], cb=ffn_conv_b[l][None, :],
        wd=w_ffn_down[l].astype(BF16))


def _prompt_layer(x, wts):
    b, s, d = x.shape
    assert s % (CMP_STRIDE * LANES) == 0 and s % TQ == 0
    t = b * s
    x2d = x.reshape(t, d)
    assert TM == MOBA_BLOCK
    pr = _inproj(x, wts["g_attn"], wts["w1t"], wts["wngt"], wts["wmg"], wts["gcol"])
    mg = pr["mg"]
    nb = s // MOBA_BLOCK
    nbp = -(-nb // 16) * 16
    means = jnp.pad(pr["ksum"].reshape(b, nb, MW) * (1.0 / MOBA_BLOCK), ((0, 0), (0, nbp - nb), (0, 0)))
    m_bias, m_diag, _ = _alibi_tables(MOBA_SLOPES)
    o_m = _moba_prompt(pr["mqt"], pr["mk"], pr["mvt"], means, m_bias, m_diag)

    ncp = s // CMP_STRIDE
    ck, cv = _compress_prompt(pr["ck"], pr["cv"], wts["cw"])
    ns = s // SEL_BLOCK
    nsr = -(-ns // 8) * 8
    ovl_t = jnp.asarray(_overlap_matrix(ncp, ncp - CMP_LEN // CMP_STRIDE + 1, ns, nsr).T, BF16)
    cb = np.asarray(NSA_SLOPES, np.float32)[:, None, None] * (
        (np.arange(ncp) * CMP_STRIDE + CMP_LEN - 1)[None, :, None] - np.arange(TQ)[None, None, :]).astype(np.float32)
    n_bias, n_diag, n_edge = _alibi_tables(NSA_SLOPES)
    o_n = _nsa_prompt(pr["nqt"], ck, jnp.swapaxes(cv, 1, 2), pr["sk"], pr["svt"], pr["wk"], pr["wvt"], pr["ngt"],
                      ovl_t, jnp.asarray(cb), n_bias, n_diag, n_edge)

    f = wts["cwt"].shape[1]
    y2d, conv = _merge_ffn_prompt(x2d, o_m.reshape(t, -1), o_n.reshape(t, -1), mg, wts["wa"], wts["wb"], wts["wo"],
                                  b, jnp.zeros((b, CONV_W - 1, f), F32), wts["gf"], wts["win"], wts["cwt"],
                                  wts["cb"], wts["wd"])
    wrows = min(WINDOW, s)
    rows_view = lambda a, heads: jnp.transpose(a.reshape(b, 2, heads, HEAD_DIM, a.shape[2]), (0, 4, 1, 2, 3))
    outs = (rows_view(pr["mkvt"], MOBA_HEADS), rows_view(pr["ckvt"], NSA_GROUPS), rows_view(pr["skvt"], NSA_GROUPS),
            rows_view(pr["wkvt"][:, :, s - wrows:], NSA_GROUPS), conv)
    return y2d.reshape(b, s, d), outs


def _pages_t(cache):
    dp, npool = cache.shape[0], cache.shape[1]
    return jnp.transpose(cache, (0, 1, 3, 4, 5, 2)).reshape(dp * npool, -1, PAGE)


def _sample_layer(x, wts, pool_moba_t, pool_cmp_t, pool_sel_t, win_state, conv_state, page_table):
    db, ds, d = x.shape
    t = db * ds
    assert t % TM == 0 and ds == 8
    n_pages = page_table.shape[1]
    past = n_pages * PAGE
    x2d = x.reshape(t, d)
    pr = _inproj(x2d[None], wts["g_attn"], wts["w1t"], wts["wngt"], wts["wmg"], wts["gcol"])
    rows = lambda a: jnp.transpose(a[0])
    mq, mkv, nq, ckv, skv, wkv = (rows(pr[k]) for k in ("mqt", "mkvt", "nqt", "ckvt", "skvt", "wkvt"))
    ng = jnp.pad(rows(pr["ngt"]), ((0, 0), (0, LANES - NG_ROWS)))
    mg = pr["mg"]
    mw = MW
    kw = KW
    o_m = _moba_sample(page_table, mq.reshape(db, ds, mw), mkv[:, 0:mw].reshape(db, ds, mw),
                       mkv[:, mw:2 * mw].reshape(db, ds, mw), pool_moba_t.reshape(-1, 2, mw, PAGE))

    assert (past + ds) // CMP_STRIDE == past // CMP_STRIDE
    ncp = past // CMP_STRIDE
    ck, cv = _compress_sample(page_table, pool_cmp_t, wts["cw"])

    nqf = nq.astype(F32).reshape(db, ds, NSA_HEADS, HEAD_DIM).transpose(0, 2, 1, 3)
    zero = jnp.zeros_like(nqf)
    hh = jnp.arange(NSA_HEADS)[None, :, None, None]
    q_grp = jnp.where(hh // NSA_HPG == 0, jnp.concatenate([nqf, zero], -1), jnp.concatenate([zero, nqf], -1))
    q_grp = q_grp.reshape(db, NSA_HEADS * ds, LANES).astype(BF16)
    ns = -(-(past + ds) // SEL_BLOCK)
    nsp = -(-ns // LANES) * LANES
    ovl = jnp.asarray(_overlap_matrix(ncp, ncp - CMP_LEN // CMP_STRIDE + 1, ns, nsp), BF16)
    expand = jnp.asarray((np.arange(nsp)[:, None] == np.arange(past)[None, :] // SEL_BLOCK).astype(np.float32), BF16)
    wb = win_state.shape[1]
    wst_t = jnp.transpose(win_state, (0, 2, 3, 4, 1)).reshape(db, 2 * kw, wb)
    o_n = _nsa_sample(page_table, q_grp, ck, cv, skv.reshape(db, ds, -1), wkv.reshape(db, ds, -1), wst_t,
                      ng.reshape(db, ds, LANES), ovl, expand, pool_sel_t)

    h2d = _merge(x2d, o_m.reshape(t, -1), o_n.reshape(t, -1), mg, wts["wa"], wts["wb"], wts["wo"])
    st0 = jnp.repeat(conv_state[:, 0], ds, axis=0)
    st1 = jnp.repeat(conv_state[:, 1], ds, axis=0)
    y2d, g2d = _ffn_sample(h2d, ds, st0, st1, wts["gf"], wts["win"], wts["cwt"], wts["cb"], wts["wd"])
    gp = jnp.concatenate([conv_state, g2d.reshape(db, ds, -1)], axis=1)
    wcat = jnp.concatenate([win_state, wkv.reshape(db, ds, 2, NSA_GROUPS, HEAD_DIM)], axis=1)
    wrows = min(WINDOW, wcat.shape[1])
    outs = (mkv.reshape(db, ds, 2, MOBA_HEADS, HEAD_DIM), ckv.reshape(db, ds, 2, NSA_GROUPS, HEAD_DIM),
            skv.reshape(db, ds, 2, NSA_GROUPS, HEAD_DIM), wcat[:, wcat.shape[1] - wrows:], gp[:, ds:])
    return y2d.reshape(db, ds, d), outs


def kernel(x_prompt, x_sample, cache_moba_kv, cache_nsa_cmp_kv, cache_nsa_sel_kv, state_nsa_win_kv, state_ffn_conv,
           page_table, attn_norm_g, w_in, qk_norm_g, nsa_cmp_pe, nsa_cmp_w1, nsa_cmp_b1, nsa_cmp_w2, nsa_cmp_b2,
           w_branch_moba, w_branch_nsa, w_out, ffn_norm_g, w_ffn_in, ffn_conv_w, ffn_conv_b, w_ffn_down):
    depth = w_in.shape[0]
    n_pool = cache_moba_kv.shape[1]
    pools = [_pages_t(c) for c in (cache_moba_kv, cache_nsa_cmp_kv, cache_nsa_sel_kv)]
    hp, hs = x_prompt, x_sample
    new = [[] for _ in range(10)]
    for l in range(depth):
        wts = _layer_weights(l, attn_norm_g, w_in, qk_norm_g, nsa_cmp_pe, nsa_cmp_w1, nsa_cmp_b1, nsa_cmp_w2,
                             nsa_cmp_b2, w_branch_moba, w_branch_nsa, w_out, ffn_norm_g, w_ffn_in, ffn_conv_w,
                             ffn_conv_b, w_ffn_down)
        hp, outs_p = _prompt_layer(hp, wts)
        hs, outs_s = _sample_layer(hs, wts, *pools, state_nsa_win_kv[l], state_ffn_conv[l],
                                   page_table + l * n_pool)
        for lst, arr in zip(new[:5], outs_p):
            lst.append(arr)
        for lst, arr in zip(new[5:], outs_s):
            lst.append(arr)
    st = [jnp.stack(v) for v in new]
    return (hp, hs, st[0], st[5], st[1], st[6], st[2], st[7], st[3], st[8], st[4], st[9])
```

```python
import functools

import numpy as np
import jax
import jax.numpy as jnp
from jax import lax
from jax.experimental import pallas as pl
from jax.experimental.pallas import tpu as pltpu

F32 = jnp.float32
BF16 = jnp.bfloat16

HEAD_DIM = 64
MOBA_HEADS = 8
MOBA_BLOCK = 256
MOBA_TOPK = 3
NSA_HEADS = 8
NSA_GROUPS = 2
NSA_HPG = NSA_HEADS // NSA_GROUPS
CMP_LEN = 32
CMP_STRIDE = 16
CMP_HIDDEN = 128
SEL_BLOCK = 64
SEL_TOPN = 8
WINDOW = 512
CONV_W = 3
PAGE = 128
RMS_EPS = 1e-6
NEG = -1e30
BIG = 1e30
LOG2E = 1.4426950408889634
QK_SCALE = HEAD_DIM ** -0.5 * LOG2E

LANES = 128
TQ = 256
FFN_TM = 512
KEY_CHUNK = 64
TM = 256
PAGES_PER_STEP = 32
VMEM_LIMIT = 56 * 1024 * 1024

MOBA_SLOPES = tuple(float(2.0 ** (-8.0 * (h + 1) / MOBA_HEADS)) * LOG2E for h in range(MOBA_HEADS))
NSA_SLOPES = tuple(float(2.0 ** (-8.0 * (h + 1) / NSA_HEADS)) * LOG2E for h in range(NSA_HEADS))


def _dot(a, b):
    return jnp.dot(a, b, preferred_element_type=F32)


def _dot_nt(a, b):
    return lax.dot_general(a, b, (((1,), (1,)), ((), ())), preferred_element_type=F32)


def _div_pow2(x, n):
    assert n & (n - 1) == 0
    return lax.shift_right_logical(x, jnp.int32(n.bit_length() - 1))


def _mod_pow2(x, n):
    assert n & (n - 1) == 0
    return x & (n - 1)


def _split_bf16(x):
    hi = x.astype(BF16)
    lo = (x - hi.astype(F32)).astype(BF16)
    return hi, lo


def _cparams(sem):
    return pltpu.CompilerParams(dimension_semantics=sem, vmem_limit_bytes=VMEM_LIMIT)


def _const_spec(shape):
    nd = len(shape)
    return pl.BlockSpec(shape, lambda *_: (0,) * nd)


def _seg_rms(y, bd, gain):
    wb = bd.shape[0]
    outs = []
    for c in range(y.shape[1] // wb):
        s = y[:, c * wb:(c + 1) * wb]
        ss = _dot((s * s).astype(BF16), bd)
        outs.append(s * lax.rsqrt(ss * (1.0 / HEAD_DIM) + RMS_EPS))
    r = outs[0] if len(outs) == 1 else jnp.concatenate(outs, axis=1)
    return r * gain


def _top_rank(val, n, lane):
    rank = jnp.zeros(val.shape, F32)
    for j2 in range(n):
        c = val[:, j2:j2 + 1]
        before = (c > val) | ((c == val) & (lane > j2))
        rank = rank + jnp.where(before, 1.0, 0.0)
    return rank


def _top_rank_rows(val, n, rowid):
    rank = jnp.zeros(val.shape, F32)
    for j2 in range(n):
        c = val[j2:j2 + 1, :]
        before = (c > val) | ((c == val) & (rowid > j2))
        rank = rank + jnp.where(before, 1.0, 0.0)
    return rank


MW = MOBA_HEADS * HEAD_DIM
KW = NSA_GROUPS * HEAD_DIM
R_MQ, R_MK, R_MV, R_NQ = 0, MW, 2 * MW, 3 * MW
R_CK = 4 * MW
R_CV, R_SK, R_SV, R_WK, R_WV, R_END = (R_CK + j * KW for j in range(1, 7))
NG_ROWS = -(-3 * NSA_HEADS // 16) * 16


def _inproj_kernel(x_ref, g_ref, w1t_ref, wngt_ref, wmg_ref, gcol_ref,
                   mqt_ref, mkvt_ref, mk_ref, mvt_ref, nqt_ref, ckvt_ref, ck_ref, cv_ref, skvt_ref, sk_ref, svt_ref,
                   wkvt_ref, wk_ref, wvt_ref, ngt_ref, mg_ref, ksum_ref):
    x = x_ref[0]
    xn = x * lax.rsqrt(jnp.mean(x * x, axis=-1, keepdims=True) + RMS_EPS) * g_ref[...]
    mg_ref[...] = jax.nn.sigmoid(_dot(xn.astype(BF16), wmg_ref[...])).astype(BF16)
    xnt = jnp.transpose(xn).astype(BF16)
    ngt_ref[0] = jax.nn.sigmoid(_dot(wngt_ref[...], xnt))
    yt = _dot(w1t_ref[...], xnt)

    def normed(r0, n_heads):
        outs = []
        for j in range(n_heads):
            seg = yt[r0 + j * HEAD_DIM:r0 + (j + 1) * HEAD_DIM, :]
            ss = jnp.sum(seg * seg, axis=0, keepdims=True)
            outs.append(seg * lax.rsqrt(ss * (1.0 / HEAD_DIM) + RMS_EPS)
                        * gcol_ref[r0 + j * HEAD_DIM:r0 + (j + 1) * HEAD_DIM, :])
        return jnp.concatenate(outs, axis=0)

    mqt_ref[0] = (normed(R_MQ, MOBA_HEADS) * QK_SCALE).astype(BF16)
    mkt = normed(R_MK, MOBA_HEADS)
    mkvt_ref[0, 0:MW, :] = mkt
    mkvt_ref[0, MW:2 * MW, :] = yt[R_MV:R_NQ, :]
    mk = jnp.transpose(mkt)
    mk_ref[0] = mk.astype(BF16)
    ksum_ref[0] = jnp.sum(mk, axis=0, keepdims=True)
    mvt_ref[0] = yt[R_MV:R_NQ, :].astype(BF16)
    nqt_ref[0] = (normed(R_NQ, NSA_HEADS) * QK_SCALE).astype(BF16)
    ckvt_ref[0] = yt[R_CK:R_SK, :]
    ck_ref[0] = jnp.transpose(yt[R_CK:R_CV, :])
    cv_ref[0] = jnp.transpose(yt[R_CV:R_SK, :])
    skt = normed(R_SK, NSA_GROUPS)
    skvt_ref[0, 0:KW, :] = skt
    skvt_ref[0, KW:2 * KW, :] = yt[R_SV:R_WK, :]
    sk_ref[0] = jnp.transpose(skt).astype(BF16)
    svt_ref[0] = yt[R_SV:R_WK, :].astype(BF16)
    wkt = normed(R_WK, NSA_GROUPS)
    wkvt_ref[0, 0:KW, :] = wkt
    wkvt_ref[0, KW:2 * KW, :] = yt[R_WV:R_END, :]
    wk_ref[0] = jnp.transpose(wkt).astype(BF16)
    wvt_ref[0] = yt[R_WV:R_END, :].astype(BF16)


def _inproj(x, g_attn, w1t, wngt, wmg, gcol):
    n, length, d = x.shape
    nt = length // TM
    ft = lambda rows: pl.BlockSpec((1, rows, TM), lambda b, i: (b, 0, i))
    rw = lambda w: pl.BlockSpec((1, TM, w), lambda b, i: (b, i, 0))
    fts = lambda rows, dt: jax.ShapeDtypeStruct((n, rows, length), dt)
    rws = lambda w, dt: jax.ShapeDtypeStruct((n, length, w), dt)
    cst = lambda a: pl.BlockSpec(a.shape, lambda b, i, nd=a.ndim: (0,) * nd)
    out_specs = (ft(MW), ft(2 * MW), rw(MW), ft(MW), ft(MW), ft(2 * KW), rw(KW), rw(KW), ft(2 * KW), rw(KW), ft(KW),
                 ft(2 * KW), rw(KW), ft(KW), ft(NG_ROWS),
                 pl.BlockSpec((TM, 2 * d), lambda b, i: (b * nt + i, 0)),
                 pl.BlockSpec((1, 1, MW), lambda b, i: (b * nt + i, 0, 0)))
    out_shape = (fts(MW, BF16), fts(2 * MW, F32), rws(MW, BF16), fts(MW, BF16), fts(MW, BF16), fts(2 * KW, F32),
                 rws(KW, F32), rws(KW, F32), fts(2 * KW, F32), rws(KW, BF16), fts(KW, BF16),
                 fts(2 * KW, F32), rws(KW, BF16), fts(KW, BF16), fts(NG_ROWS, F32),
                 jax.ShapeDtypeStruct((n * length, 2 * d), BF16), jax.ShapeDtypeStruct((n * nt, 1, MW), F32))
    names = ("mqt", "mkvt", "mk", "mvt", "nqt", "ckvt", "ck", "cv", "skvt", "sk", "svt", "wkvt", "wk", "wvt", "ngt",
             "mg", "ksum")
    outs = pl.pallas_call(
        _inproj_kernel, grid=(n, nt),
        in_specs=[pl.BlockSpec((1, TM, d), lambda b, i: (b, i, 0)), cst(g_attn), cst(w1t), cst(wngt), cst(wmg),
                  cst(gcol)],
        out_specs=out_specs, out_shape=out_shape,
        compiler_params=_cparams(("arbitrary", "arbitrary")), name="inproj",
    )(x, g_attn, w1t, wngt, wmg, gcol)
    return dict(zip(names, outs))


def _alibi_tables(slopes):
    rel = (np.arange(TQ)[:, None] - np.arange(TQ)[None, :]).astype(np.float32)
    bias = np.asarray(slopes, np.float32)[:, None, None] * rel[None]
    diag = np.where(rel[None] <= 0, bias, np.float32(NEG))
    edge = np.where(rel[None] >= 0, bias, np.float32(NEG))
    return jnp.asarray(bias), jnp.asarray(diag), jnp.asarray(edge)


def _moba_prompt_kernel(qt_ref, k_ref, vt_ref, mean_ref, bias_ref, diag_ref, o_ref,
                        qz_sc, sel_sc, m_sc, l_sc, acc_sc, s_sc, p_sc, *, nb):
    i = pl.program_id(1)
    npair = MOBA_HEADS // 2
    zeros = jnp.zeros((HEAD_DIM, TQ), BF16)
    for pair in range(npair):
        r0 = 2 * pair * HEAD_DIM
        qz_sc[pair, :, 0:TQ] = jnp.concatenate([qt_ref[0, r0:r0 + HEAD_DIM, :], zeros], axis=0)
        qz_sc[pair, :, TQ:2 * TQ] = jnp.concatenate([zeros, qt_ref[0, r0 + HEAD_DIM:r0 + 2 * HEAD_DIM, :]], axis=0)
    m_sc[...] = jnp.full(m_sc.shape, NEG, F32)
    l_sc[...] = jnp.zeros(l_sc.shape, F32)
    acc_sc[...] = jnp.zeros(acc_sc.shape, F32)

    nbp = mean_ref.shape[1]
    rowb = lax.broadcasted_iota(jnp.int32, (nbp, TQ), 0)
    cand = rowb < i
    for h in range(MOBA_HEADS):
        pair, half = divmod(h, 2)
        qz = qz_sc[pair, :, half * TQ:(half + 1) * TQ]
        mh, ml = _split_bf16(mean_ref[0, :, pair * LANES:(pair + 1) * LANES])
        ss = _dot(mh, qz) + _dot(ml, qz)
        rank = _top_rank_rows(jnp.where(cand, ss, -jnp.inf), nb, rowb)
        sel_sc[h] = jnp.where(cand & (rank < MOBA_TOPK), 1.0, 0.0)

    def kv_block(kb, own):
        start = pl.multiple_of(kb * TQ, TQ)
        k_blk = k_ref[0, pl.ds(start, TQ), :]
        vt_blk = vt_ref[0, :, pl.ds(start, TQ)]
        off = ((kb - i) * TQ).astype(F32)
        tab_ref = diag_ref if own else bias_ref
        bms = []
        for pair in range(npair):
            s2 = _dot(k_blk[:, pair * LANES:(pair + 1) * LANES], qz_sc[pair])
            for half in range(2):
                h = 2 * pair + half
                s = s2[:, half * TQ:(half + 1) * TQ] + tab_ref[h]
                s_sc[h] = s
                bms.append(jnp.max(s, axis=0, keepdims=True))
        alphas = []
        for h in range(MOBA_HEADS):
            m_old = m_sc[h:h + 1, :]
            if own:
                m_new = jnp.maximum(m_old, bms[h])
                shift = m_new
            else:
                c = MOBA_SLOPES[h] * off
                sel = sel_sc[h, pl.ds(kb, 1), :] > 0.5
                m_new = jnp.maximum(m_old, jnp.where(sel, bms[h] + c, NEG))
                shift = jnp.where(sel, m_new - c, BIG)
            alpha = jnp.exp2(m_old - m_new)
            psum = jnp.zeros((8, TQ), F32)
            for r0 in range(0, TQ, KEY_CHUNK):
                rs = slice(r0, r0 + KEY_CHUNK)
                p = jnp.exp2(s_sc[h, rs, :] - shift)
                psum = psum + jnp.sum(p.reshape(KEY_CHUNK // 8, 8, TQ), axis=0)
                p_sc[h, rs, :] = p.astype(BF16)
            l_sc[h:h + 1, :] = alpha * l_sc[h:h + 1, :] + jnp.sum(psum, axis=0, keepdims=True)
            m_sc[h:h + 1, :] = m_new
            alphas.append(alpha)
        for h in range(MOBA_HEADS):
            rows = slice(h * HEAD_DIM, (h + 1) * HEAD_DIM)
            acc_sc[rows, :] = alphas[h] * acc_sc[rows, :] + _dot(vt_blk[rows, :], p_sc[h])

    def body(kb, carry):
        kv_block(kb, False)
        return carry

    lax.fori_loop(0, i, body, 0)
    kv_block(i, True)

    for h in range(MOBA_HEADS):
        rows = slice(h * HEAD_DIM, (h + 1) * HEAD_DIM)
        acc_sc[rows, :] = acc_sc[rows, :] / l_sc[h:h + 1, :]
    o_ref[0] = jnp.transpose(acc_sc[...]).astype(BF16)


def _moba_prompt(qt, k, vt, means, bias, diag):
    b, w, s = qt.shape
    nb = s // TQ
    return pl.pallas_call(
        functools.partial(_moba_prompt_kernel, nb=nb), grid=(b, nb),
        in_specs=[pl.BlockSpec((1, w, TQ), lambda bi, i: (bi, 0, i)),
                  pl.BlockSpec((1, s, w), lambda bi, i: (bi, 0, 0)),
                  pl.BlockSpec((1, w, s), lambda bi, i: (bi, 0, 0)),
                  pl.BlockSpec((1,) + means.shape[1:], lambda bi, i: (bi, 0, 0)),
                  _const_spec(bias.shape), _const_spec(diag.shape)],
        out_specs=pl.BlockSpec((1, TQ, w), lambda bi, i: (bi, i, 0)),
        out_shape=jax.ShapeDtypeStruct((b, s, w), BF16),
        scratch_shapes=[pltpu.VMEM((MOBA_HEADS // 2, LANES, 2 * TQ), BF16),
                        pltpu.VMEM((MOBA_HEADS, means.shape[1], TQ), F32),
                        pltpu.VMEM((MOBA_HEADS, TQ), F32), pltpu.VMEM((MOBA_HEADS, TQ), F32),
                        pltpu.VMEM((w, TQ), F32), pltpu.VMEM((MOBA_HEADS, TQ, TQ), F32),
                        pltpu.VMEM((MOBA_HEADS, TQ, TQ), BF16)],
        compiler_params=_cparams(("arbitrary", "arbitrary")), name="moba_prompt",
    )(qt, k, vt, means, bias, diag)


def _compress_core(load_rows, ncp, wfull_ref, pe_ref, w1_ref, b1_ref, w2_ref, b2_ref, gk_ref, bd_ref, ck_ref, cv_ref):
    kw = NSA_GROUPS * HEAD_DIM
    hw = wfull_ref.shape[2]
    projs = []
    for kv in range(2):
        acc = jnp.zeros((ncp, hw), F32)
        for t in range(0, CMP_STRIDE, 2):
            x2 = jnp.concatenate([load_rows(kv, t), load_rows(kv, t + 1)], axis=1).astype(BF16)
            acc = acc + _dot(x2, wfull_ref[kv, t * kw:(t + 2) * kw, :])
        projs.append(acc)
    proj = jnp.concatenate(projs, axis=1)
    hids = []
    for kvg in range(2 * NSA_GROUPS):
        kv = kvg // NSA_GROUPS
        pe_term = _dot(pe_ref[kv], w1_ref[kv])[0:1, :]
        c0 = kvg * 2 * CMP_HIDDEN
        p0 = proj[:, c0:c0 + CMP_HIDDEN]
        p1 = pltpu.roll(proj[:, c0 + CMP_HIDDEN:c0 + 2 * CMP_HIDDEN], ncp - 1, 0)
        hids.append(jax.nn.gelu(b1_ref[kv:kv + 1, :] + p0 + p1 + pe_term))
    hid = jnp.concatenate(hids, axis=1).astype(BF16)
    out = _dot(hid, w2_ref[...]) + b2_ref[...]
    ck_ref[0] = _seg_rms(out[:, 0:kw], bd_ref[...], gk_ref[...]).astype(BF16)
    cv_ref[0] = out[:, kw:2 * kw].astype(BF16)


def _compress_prompt_kernel(xk_ref, xv_ref, *refs):
    ncp = xk_ref.shape[1] // CMP_STRIDE
    x_refs = (xk_ref, xv_ref)
    _compress_core(lambda kv, t: x_refs[kv][0, pl.ds(t, ncp, stride=CMP_STRIDE), :], ncp, *refs)


def _compress_sample_kernel(pt_ref, *refs, n_pages):
    pg_refs = refs[:n_pages]
    x_scs = refs[-2:]
    kw = NSA_GROUPS * HEAD_DIM
    for j in range(n_pages):
        for kv in range(2):
            x_scs[kv][j * PAGE:(j + 1) * PAGE, :] = jnp.transpose(pg_refs[j][0, kv * kw:(kv + 1) * kw, :])
    ncp = n_pages * PAGE // CMP_STRIDE
    _compress_core(lambda kv, t: x_scs[kv][pl.ds(t, ncp, stride=CMP_STRIDE), :], ncp, *refs[n_pages:-2])


def _compress_consts(cw):
    return (cw["wfull"], cw["pe"], cw["w1"], cw["b1"], cw["w2bd"], cw["b2row"], cw["gk"], cw["bd"])


def _compress_prompt(xk, xv, cw):
    b, s, kw = xk.shape
    ncp = s // CMP_STRIDE
    consts = _compress_consts(cw)
    xspec = pl.BlockSpec((1, s, kw), lambda bi: (bi, 0, 0))
    ospec = pl.BlockSpec((1, ncp, kw), lambda bi: (bi, 0, 0))
    return pl.pallas_call(
        _compress_prompt_kernel, grid=(b,),
        in_specs=[xspec, xspec] + [_const_spec(a.shape) for a in consts],
        out_specs=(ospec, ospec),
        out_shape=(jax.ShapeDtypeStruct((b, ncp, kw), BF16), jax.ShapeDtypeStruct((b, ncp, kw), BF16)),
        compiler_params=_cparams(("arbitrary",)), name="nsa_compress_prompt",
    )(xk, xv, *consts)


def _compress_sample(page_table, pool_t, cw):
    db, n_pages = page_table.shape
    fw = pool_t.shape[1]
    ncp = n_pages * PAGE // CMP_STRIDE
    consts = _compress_consts(cw)
    kw = NSA_GROUPS * HEAD_DIM
    cst = lambda a: pl.BlockSpec(a.shape, lambda bi, pt, nd=a.ndim: (0,) * nd)
    ospec = pl.BlockSpec((1, ncp, kw), lambda bi, pt: (bi, 0, 0))
    pg_specs = [pl.BlockSpec((1, fw, PAGE), lambda bi, pt, j=j: (pt[bi, j], 0, 0)) for j in range(n_pages)]
    gs = pltpu.PrefetchScalarGridSpec(
        num_scalar_prefetch=1, grid=(db,), in_specs=pg_specs + [cst(a) for a in consts], out_specs=(ospec, ospec),
        scratch_shapes=[pltpu.VMEM((n_pages * PAGE, kw), F32), pltpu.VMEM((n_pages * PAGE, kw), F32)])
    return pl.pallas_call(
        functools.partial(_compress_sample_kernel, n_pages=n_pages), grid_spec=gs,
        out_shape=(jax.ShapeDtypeStruct((db, ncp, kw), BF16), jax.ShapeDtypeStruct((db, ncp, kw), BF16)),
        compiler_params=_cparams(("arbitrary",)), name="nsa_compress_sample",
    )(page_table, *([pool_t] * n_pages), *consts)


def _nsa_prompt_kernel(qt_ref, ck_ref, cvt_ref, sk_ref, svt_ref, wk_ref, wvt_ref, ng_ref, ovl_ref, cb_ref,
                       bias_ref, diag_ref, edge_ref, o_ref,
                       qz_sc, p_sc, s_sc, sel_sc, oc_sc, ms_sc, ls_sc, as_sc, mw_sc, lw_sc, aw_sc, *, ns):
    i = pl.program_id(1)
    ncp = ck_ref.shape[1]
    zeros = jnp.zeros((HEAD_DIM, TQ), BF16)
    for h in range(NSA_HEADS):
        g, p_ = divmod(h, NSA_HPG)
        qh = qt_ref[0, h * HEAD_DIM:(h + 1) * HEAD_DIM, :]
        qz_sc[g, :, p_ * TQ:(p_ + 1) * TQ] = jnp.concatenate([qh, zeros] if g == 0 else [zeros, qh], axis=0)
    for m_ref, l_ref, a_ref in ((ms_sc, ls_sc, as_sc), (mw_sc, lw_sc, aw_sc)):
        m_ref[...] = jnp.full(m_ref.shape, NEG, F32)
        l_ref[...] = jnp.zeros(l_ref.shape, F32)
        a_ref[...] = jnp.zeros(a_ref.shape, F32)

    qpos = i * TQ + lax.broadcasted_iota(jnp.int32, (1, TQ), 1)
    cend = lax.broadcasted_iota(jnp.int32, (ncp, 1), 0) * CMP_STRIDE + (CMP_LEN - 1)
    cmask = cend <= qpos
    nsr = sel_sc.shape[1]
    rowj = lax.broadcasted_iota(jnp.int32, (nsr, TQ), 0)
    cur = _div_pow2(qpos, SEL_BLOCK)
    forced = (rowj == 0) | (rowj == cur) | (rowj == cur - 1)
    cand = rowj <= cur
    ioff = (i * TQ).astype(F32)
    for g in range(NSA_GROUPS):
        grow = slice(g * HEAD_DIM, (g + 1) * HEAD_DIM)
        s4 = _dot(ck_ref[0], qz_sc[g])
        psum = jnp.zeros((ncp, TQ), F32)
        for p_ in range(NSA_HPG):
            h = g * NSA_HPG + p_
            s = jnp.where(cmask, s4[:, p_ * TQ:(p_ + 1) * TQ] + cb_ref[h] - NSA_SLOPES[h] * ioff, NEG)
            e = jnp.where(cmask, jnp.exp2(s - jnp.max(s, axis=0, keepdims=True)), 0.0)
            den = jnp.sum(e, axis=0, keepdims=True)
            pn = e / jnp.where(den > 0.0, den, 1.0)
            psum = psum + pn
            p_sc[g, 0:ncp, p_ * TQ:(p_ + 1) * TQ] = pn.astype(BF16)
        oc4 = _dot(cvt_ref[0], p_sc[g, 0:ncp, :])
        for p_ in range(NSA_HPG):
            h = g * NSA_HPG + p_
            oc_sc[h * HEAD_DIM:(h + 1) * HEAD_DIM, :] = oc4[grow, p_ * TQ:(p_ + 1) * TQ]
        ph, plo = _split_bf16(psum)
        imp = _dot(ovl_ref[...], ph) + _dot(ovl_ref[...], plo)
        val = jnp.where(cand, jnp.where(forced, jnp.inf, imp), -jnp.inf)
        rank = _top_rank_rows(val, ns, rowj)
        sel_sc[g] = jnp.where(cand & (rank < SEL_TOPN), 1.0, 0.0)

    spb = TQ // SEL_BLOCK

    def attend(k_blk, vt_blk, tab_ref, off, kb_sel, m_ref, l_ref, a_ref):
        blocks = [slice(j * SEL_BLOCK, (j + 1) * SEL_BLOCK) for j in range(spb)]
        parts = []
        for g in range(NSA_GROUPS):
            s4 = _dot(k_blk, qz_sc[g])
            for p_ in range(NSA_HPG):
                h = g * NSA_HPG + p_
                s = s4[:, p_ * TQ:(p_ + 1) * TQ] + tab_ref[h]
                s_sc[h] = s
                parts.append([jnp.max(s[bl, :].reshape(SEL_BLOCK // 8, 8, TQ), axis=0) for bl in blocks])
        alphas = []
        for h in range(NSA_HEADS):
            g, p_ = divmod(h, NSA_HPG)
            c = NSA_SLOPES[h] * off
            m_old = m_ref[h:h + 1, :]
            if kb_sel is None:
                sels = [None] * spb
                part = functools.reduce(jnp.maximum, parts[h])
            else:
                sels = [sel_sc[g, pl.ds(kb_sel * spb + j, 1), :] > 0.5 for j in range(spb)]
                part = functools.reduce(jnp.maximum, [jnp.where(sl, pt, NEG) for sl, pt in zip(sels, parts[h])])
            m_new = jnp.maximum(m_old, jnp.max(part, axis=0, keepdims=True) + c)
            alpha = jnp.exp2(m_old - m_new)
            psum = jnp.zeros((8, TQ), F32)
            for bl, sl in zip(blocks, sels):
                shift = m_new - c if sl is None else jnp.where(sl, m_new - c, BIG)
                p = jnp.exp2(s_sc[h, bl, :] - shift)
                psum = psum + jnp.sum(p.reshape(SEL_BLOCK // 8, 8, TQ), axis=0)
                p_sc[g, bl, p_ * TQ:(p_ + 1) * TQ] = p.astype(BF16)
            l_ref[h:h + 1, :] = alpha * l_ref[h:h + 1, :] + jnp.sum(psum, axis=0, keepdims=True)
            m_ref[h:h + 1, :] = m_new
            alphas.append(alpha)
        for g in range(NSA_GROUPS):
            grow = slice(g * HEAD_DIM, (g + 1) * HEAD_DIM)
            pv4 = _dot(vt_blk, p_sc[g])
            for p_ in range(NSA_HPG):
                h = g * NSA_HPG + p_
                rows = slice(h * HEAD_DIM, (h + 1) * HEAD_DIM)
                a_ref[rows, :] = alphas[h] * a_ref[rows, :] + pv4[grow, p_ * TQ:(p_ + 1) * TQ]

    def sel_tile(kb, tab_ref):
        start = pl.multiple_of(kb * TQ, TQ)
        attend(sk_ref[0, pl.ds(start, TQ), :], svt_ref[0, :, pl.ds(start, TQ)], tab_ref,
               ((kb - i) * TQ).astype(F32), kb, ms_sc, ls_sc, as_sc)

    def win_tile(back, tab_ref):
        kb = jnp.maximum(i - back, 0)
        start = pl.multiple_of(kb * TQ, TQ)
        attend(wk_ref[0, pl.ds(start, TQ), :], wvt_ref[0, :, pl.ds(start, TQ)], tab_ref,
               -float(back * TQ), None, mw_sc, lw_sc, aw_sc)

    def sel_body(kb, carry):
        sel_tile(kb, bias_ref)
        return carry

    lax.fori_loop(0, i, sel_body, 0)
    sel_tile(i, diag_ref)

    @pl.when(i >= 2)
    def _():
        win_tile(2, edge_ref)

    @pl.when(i >= 1)
    def _():
        win_tile(1, bias_ref)

    win_tile(0, diag_ref)

    ng = ng_ref[0]
    for h in range(NSA_HEADS):
        rows = slice(h * HEAD_DIM, (h + 1) * HEAD_DIM)
        o_s = as_sc[rows, :] / ls_sc[h:h + 1, :]
        o_w = aw_sc[rows, :] / lw_sc[h:h + 1, :]
        oc_sc[rows, :] = (ng[h:h + 1, :] * oc_sc[rows, :] + ng[NSA_HEADS + h:NSA_HEADS + h + 1, :] * o_s
                          + ng[2 * NSA_HEADS + h:2 * NSA_HEADS + h + 1, :] * o_w)
    o_ref[0] = jnp.transpose(oc_sc[...]).astype(BF16)


def _nsa_prompt(qt, ck, cvt, sk, svt, wk, wvt, ngt, ovl, cb, bias, diag, edge):
    b, w, s = qt.shape
    assert WINDOW == 2 * TQ
    ns = s // SEL_BLOCK
    kw = NSA_GROUPS * HEAD_DIM
    full = lambda a: pl.BlockSpec((1,) + a.shape[1:], lambda bi, i: (bi, 0, 0))
    hq = NSA_HPG * TQ
    return pl.pallas_call(
        functools.partial(_nsa_prompt_kernel, ns=ns), grid=(b, s // TQ),
        in_specs=[pl.BlockSpec((1, w, TQ), lambda bi, i: (bi, 0, i)), full(ck), full(cvt), full(sk), full(svt),
                  full(wk), full(wvt), pl.BlockSpec((1, ngt.shape[1], TQ), lambda bi, i: (bi, 0, i)),
                  _const_spec(ovl.shape), _const_spec(cb.shape), _const_spec(bias.shape), _const_spec(diag.shape),
                  _const_spec(edge.shape)],
        out_specs=pl.BlockSpec((1, TQ, w), lambda bi, i: (bi, i, 0)),
        out_shape=jax.ShapeDtypeStruct((b, s, w), BF16),
        scratch_shapes=[pltpu.VMEM((NSA_GROUPS, kw, hq), BF16), pltpu.VMEM((NSA_GROUPS, TQ, hq), BF16),
                        pltpu.VMEM((NSA_HEADS, TQ, TQ), F32),
                        pltpu.VMEM((NSA_GROUPS, ovl.shape[0], TQ), F32), pltpu.VMEM((w, TQ), F32),
                        pltpu.VMEM((NSA_HEADS, TQ), F32), pltpu.VMEM((NSA_HEADS, TQ), F32), pltpu.VMEM((w, TQ), F32),
                        pltpu.VMEM((NSA_HEADS, TQ), F32), pltpu.VMEM((NSA_HEADS, TQ), F32), pltpu.VMEM((w, TQ), F32)],
        compiler_params=_cparams(("arbitrary", "arbitrary")), name="nsa_prompt",
    )(qt, ck, cvt, sk, svt, wk, wvt, ngt, ovl, cb, bias, diag, edge)


def _moba_sample_kernel(pt_ref, q_ref, kn_ref, vn_ref, *rest, past, ds, nbs):
    pg_refs = rest[:PAGES_PER_STEP]
    o_ref, m_sc, l_sc, o_sc, ssum_sc = rest[PAGES_PER_STEP:]
    c = pl.program_id(1)
    nch = pl.num_programs(1)
    rows = MOBA_HEADS * ds
    w = MOBA_HEADS * HEAD_DIM
    ppb = MOBA_BLOCK // PAGE
    row_w = lax.broadcasted_iota(jnp.int32, (rows, w), 0)
    diag = _div_pow2(lax.broadcasted_iota(jnp.int32, (rows, w), 1), HEAD_DIM) == _div_pow2(row_w, ds)
    lane = lax.broadcasted_iota(jnp.int32, (rows, LANES), 1)
    row1 = lax.broadcasted_iota(jnp.int32, (rows, 1), 0)
    qi = _mod_pow2(row1, ds)
    slope = jnp.zeros((rows, 1), F32)
    for h in range(MOBA_HEADS):
        slope = jnp.where(_div_pow2(row1, ds) == h, MOBA_SLOPES[h], slope)
    q8 = q_ref[0].astype(F32)
    qbd = jnp.where(diag, jnp.concatenate([q8] * MOBA_HEADS, axis=0), 0.0).astype(BF16)

    @pl.when(c == 0)
    def _():
        m_sc[...] = jnp.full(m_sc.shape, NEG, F32)
        l_sc[...] = jnp.zeros(l_sc.shape, F32)
        ssum_sc[...] = jnp.zeros(ssum_sc.shape, F32)

    nk = PAGES_PER_STEP * PAGE
    col = lax.broadcasted_iota(jnp.int32, (rows, nk), 1)
    m_all, l_all, ssum = m_sc[...], l_sc[...], ssum_sc[...]
    kt = jnp.concatenate([pg_refs[j][0, 0] for j in range(PAGES_PER_STEP)], axis=1).astype(BF16)
    s_raw = _dot(qbd, kt)
    s_all = s_raw + slope * (col - qi + (c * nk - past)).astype(F32)
    for bb in range(PAGES_PER_STEP // ppb):
        blk = c * (PAGES_PER_STEP // ppb) + bb
        cols = slice(bb * MOBA_BLOCK, (bb + 1) * MOBA_BLOCK)
        vt = jnp.concatenate([pg_refs[bb * ppb + j][0, 1] for j in range(ppb)], axis=1).astype(BF16)
        ssum = jnp.where(lane == blk, jnp.sum(s_raw[:, cols], axis=1, keepdims=True), ssum)
        s = s_all[:, cols]
        m_b = jnp.max(s, axis=1, keepdims=True)
        p = jnp.exp2(s - m_b)
        o_sc[blk] = jnp.where(diag, _dot_nt(p.astype(BF16), vt), 0.0)
        m_all = jnp.where(lane == blk, m_b, m_all)
        l_all = jnp.where(lane == blk, jnp.sum(p, axis=1, keepdims=True), l_all)
    m_sc[...] = m_all
    l_sc[...] = l_all
    ssum_sc[...] = ssum

    @pl.when(c == nch - 1)
    def _():
        ss = ssum_sc[...] * (1.0 / MOBA_BLOCK)
        own = past // MOBA_BLOCK
        cand = lane < own
        rank = _top_rank(jnp.where(cand, ss, -jnp.inf), nbs, lane)
        selb = cand & (rank < MOBA_TOPK)
        m_all = m_sc[...]
        m_past = jnp.max(jnp.where(selb, m_all, NEG), axis=1, keepdims=True)
        zpad = jnp.zeros((LANES - ds, w), F32)
        k_new = jnp.concatenate([kn_ref[0], zpad], axis=0).astype(BF16)
        v_new = jnp.concatenate([vn_ref[0], zpad], axis=0).astype(BF16)
        s_own = _dot_nt(qbd, k_new) + slope * (lane - qi).astype(F32)
        s_own = jnp.where(lane <= qi, s_own, NEG)
        m_tot = jnp.maximum(m_past, jnp.max(s_own, axis=1, keepdims=True))
        wgt = jnp.where(selb, jnp.exp2(m_all - m_tot), 0.0)
        p_own = jnp.exp2(s_own - m_tot)
        den = jnp.sum(wgt * l_sc[...], axis=1, keepdims=True) + jnp.sum(p_own, axis=1, keepdims=True)
        out = jnp.where(diag, _dot(p_own.astype(BF16), v_new), 0.0)
        for b2 in range(nbs):
            out = out + wgt[:, b2:b2 + 1] * o_sc[b2]
        out = out / den
        o8 = out[0:ds, :]
        for h in range(1, MOBA_HEADS):
            o8 = o8 + out[h * ds:(h + 1) * ds, :]
        o_ref[0] = o8.astype(BF16)


def _moba_sample(page_table, q, k_new, v_new, pool_t):
    db, ds, w = q.shape
    n_pages = page_table.shape[1]
    past = n_pages * PAGE
    nbs = past // MOBA_BLOCK
    assert n_pages % PAGES_PER_STEP == 0 and past % MOBA_BLOCK == 0 and nbs <= LANES and ds == 8
    tok = pl.BlockSpec((1, ds, w), lambda bi, c, pt: (bi, 0, 0))
    pg_specs = [pl.BlockSpec((1, 2, w, PAGE), lambda bi, c, pt, j=j: (pt[bi, c * PAGES_PER_STEP + j], 0, 0, 0))
                for j in range(PAGES_PER_STEP)]
    gs = pltpu.PrefetchScalarGridSpec(
        num_scalar_prefetch=1, grid=(db, n_pages // PAGES_PER_STEP),
        in_specs=[tok, tok, tok] + pg_specs, out_specs=tok,
        scratch_shapes=[pltpu.VMEM((MOBA_HEADS * ds, LANES), F32), pltpu.VMEM((MOBA_HEADS * ds, LANES), F32),
                        pltpu.VMEM((nbs, MOBA_HEADS * ds, w), F32), pltpu.VMEM((MOBA_HEADS * ds, LANES), F32)])
    return pl.pallas_call(
        functools.partial(_moba_sample_kernel, past=past, ds=ds, nbs=nbs), grid_spec=gs,
        out_shape=jax.ShapeDtypeStruct((db, ds, w), BF16),
        compiler_params=_cparams(("arbitrary", "arbitrary")), name="moba_sample",
    )(page_table, q, k_new, v_new, *([pool_t] * PAGES_PER_STEP))


def _nsa_sample_kernel(pt_ref, qg_ref, ck_ref, cv_ref, skn_ref, wkn_ref, wst_ref, ng_ref, ovl_ref, exp_ref,
                       *rest, past, ds, n_pages, ns):
    pg_refs = rest[:n_pages]
    o_ref = rest[n_pages]
    rows = NSA_HEADS * ds
    grows = NSA_HPG * ds
    kw = NSA_GROUPS * HEAD_DIM
    ncp = ck_ref.shape[1]
    nsp = ovl_ref.shape[1]
    wb = wst_ref.shape[2]
    row1 = lax.broadcasted_iota(jnp.int32, (rows, 1), 0)
    qi = _mod_pow2(row1, ds)
    qpos = past + qi
    slope = jnp.zeros((rows, 1), F32)
    for h in range(NSA_HEADS):
        slope = jnp.where(_div_pow2(row1, ds) == h, NSA_SLOPES[h], slope)
    lane = lax.broadcasted_iota(jnp.int32, (rows, LANES), 1)
    grp_half = _div_pow2(lane, HEAD_DIM) == _div_pow2(row1, grows)
    q_grp = qg_ref[0]

    def softmax_parts(s_list, m_list):
        m = None
        for s, mk in zip(s_list, m_list):
            mm = jnp.max(jnp.where(mk, s, NEG), axis=1, keepdims=True)
            m = mm if m is None else jnp.maximum(m, mm)
        ps = [jnp.where(mk, jnp.exp2(jnp.where(mk, s, NEG) - m), 0.0) for s, mk in zip(s_list, m_list)]
        den = sum(jnp.sum(p, axis=1, keepdims=True) for p in ps)
        return ps, jnp.where(den > 0.0, den, 1.0)

    def own_group(o):
        return jnp.where(grp_half, o, pltpu.roll(o, HEAD_DIM, 1))

    cend = lax.broadcasted_iota(jnp.int32, (rows, ncp), 1) * CMP_STRIDE + (CMP_LEN - 1)
    cmask = cend <= qpos
    s_c = _dot_nt(q_grp, ck_ref[0]) + slope * (cend - qpos).astype(F32)
    (e,), den = softmax_parts([s_c], [cmask])
    pn = e / den
    o_c = own_group(_dot(pn.astype(BF16), cv_ref[0]))
    lane_s = lax.broadcasted_iota(jnp.int32, (ds, nsp), 1)
    cur = _div_pow2(past + lax.broadcasted_iota(jnp.int32, (ds, 1), 0), SEL_BLOCK)
    forced = (lane_s == 0) | (lane_s == cur) | (lane_s == cur - 1)
    cand = lane_s <= cur
    sel_parts = []
    for g in range(NSA_GROUPS):
        psum = pn[g * grows:g * grows + ds]
        for p_ in range(1, NSA_HPG):
            psum = psum + pn[g * grows + p_ * ds:g * grows + (p_ + 1) * ds]
        ph, plo = _split_bf16(psum)
        imp = _dot(ph, ovl_ref[...]) + _dot(plo, ovl_ref[...])
        val = jnp.where(cand, jnp.where(forced, jnp.inf, imp), -jnp.inf)
        rank = _top_rank(val, ns, lane_s)
        sel_g = jnp.where(cand & (rank < SEL_TOPN), 1.0, 0.0)
        sel_parts.append(jnp.concatenate([sel_g] * NSA_HPG, axis=0))
    sel_f32 = jnp.concatenate(sel_parts, axis=0)
    sel_rows = sel_f32.astype(BF16)

    kt_all = jnp.concatenate([r[0, 0:kw, :] for r in pg_refs], axis=1).astype(BF16)
    vt_all = jnp.concatenate([r[0, kw:2 * kw, :] for r in pg_refs], axis=1).astype(BF16)
    kpos = lax.broadcasted_iota(jnp.int32, (rows, past), 1)
    s_past = _dot(q_grp, kt_all) + slope * (kpos - qpos).astype(F32)
    m_past = _dot(sel_rows, exp_ref[...]) > 0.5
    zpad = jnp.zeros((LANES - ds, 2 * kw), F32)
    skn = jnp.concatenate([skn_ref[0], zpad], axis=0)
    s_new = _dot_nt(q_grp, skn[:, 0:kw].astype(BF16)) + slope * (lane - qi).astype(F32)
    new_blk = past // SEL_BLOCK
    m_new = (lane <= qi) & (sel_f32[:, new_blk:new_blk + 1] > 0.5)
    (p_past, p_new), den = softmax_parts([s_past, s_new], [m_past, m_new])
    o_s = _dot_nt(p_past.astype(BF16), vt_all) + _dot(p_new.astype(BF16), skn[:, kw:2 * kw].astype(BF16))
    o_s = own_group(o_s / den)

    jst = lax.broadcasted_iota(jnp.int32, (rows, wb), 1)
    dist_st = qpos - (past - wb + jst)
    s_st = _dot(q_grp, wst_ref[0, 0:kw, :].astype(BF16)) - slope * dist_st.astype(F32)
    m_st = (dist_st >= 0) & (dist_st <= WINDOW)
    wkn = jnp.concatenate([wkn_ref[0], zpad], axis=0)
    s_wn = _dot_nt(q_grp, wkn[:, 0:kw].astype(BF16)) + slope * (lane - qi).astype(F32)
    m_wn = lane <= qi
    (p_st, p_wn), den = softmax_parts([s_st, s_wn], [m_st, m_wn])
    o_w = _dot_nt(p_st.astype(BF16), wst_ref[0, kw:2 * kw, :].astype(BF16)) \
        + _dot(p_wn.astype(BF16), wkn[:, kw:2 * kw].astype(BF16))
    o_w = own_group(o_w / den)

    ng = ng_ref[0]
    gate = lambda br: jnp.concatenate([ng[:, br * NSA_HEADS + h:br * NSA_HEADS + h + 1] for h in range(NSA_HEADS)],
                                      axis=0)
    o = gate(0) * o_c + gate(1) * o_s + gate(2) * o_w
    lane8 = lax.broadcasted_iota(jnp.int32, (ds, LANES), 1)
    for pair in range(NSA_HEADS // 2):
        o_even = o[(2 * pair) * ds:(2 * pair + 1) * ds]
        o_odd = o[(2 * pair + 1) * ds:(2 * pair + 2) * ds]
        o_ref[0, :, pair * LANES:(pair + 1) * LANES] = jnp.where(lane8 < HEAD_DIM, o_even, o_odd).astype(BF16)


def _nsa_sample(page_table, q_grp, ck, cv, sk_new, wk_new, wstate_t, ng, ovl, expand, pool_t):
    db, rows, _ = q_grp.shape
    ds = rows // NSA_HEADS
    n_pages = page_table.shape[1]
    past = n_pages * PAGE
    ns = -(-(past + ds) // SEL_BLOCK)
    fw = pool_t.shape[1]
    per_b = lambda a: pl.BlockSpec((1,) + a.shape[1:], lambda bi, pt: (bi, 0, 0))
    cst = lambda a: pl.BlockSpec(a.shape, lambda bi, pt, nd=a.ndim: (0,) * nd)
    pg_specs = [pl.BlockSpec((1, fw, PAGE), lambda bi, pt, j=j: (pt[bi, j], 0, 0)) for j in range(n_pages)]
    gs = pltpu.PrefetchScalarGridSpec(
        num_scalar_prefetch=1, grid=(db,),
        in_specs=[per_b(q_grp), per_b(ck), per_b(cv), per_b(sk_new), per_b(wk_new), per_b(wstate_t),
                  per_b(ng), cst(ovl), cst(expand)] + pg_specs,
        out_specs=pl.BlockSpec((1, ds, NSA_HEADS * HEAD_DIM), lambda bi, pt: (bi, 0, 0)))
    return pl.pallas_call(
        functools.partial(_nsa_sample_kernel, past=past, ds=ds, n_pages=n_pages, ns=ns), grid_spec=gs,
        out_shape=jax.ShapeDtypeStruct((db, ds, NSA_HEADS * HEAD_DIM), BF16),
        compiler_params=_cparams(("arbitrary",)), name="nsa_sample",
    )(page_table, q_grp, ck, cv, sk_new, wk_new, wstate_t, ng, ovl, expand, *([pool_t] * n_pages))


def _merged_h(x_ref, om_ref, on_ref, mg_ref, wa_ref, wb_ref, wo_ref):
    d = x_ref.shape[1]
    a = _dot(om_ref[...], wa_ref[...])
    b = _dot(on_ref[...], wb_ref[...])
    merged = mg_ref[:, 0:d] * a + mg_ref[:, d:2 * d] * b
    return x_ref[...] + _dot(merged.astype(BF16), wo_ref[...])


def _merge_kernel(*refs):
    refs[-1][...] = _merged_h(*refs[:-1])


def _merge(x2d, o_m, o_n, mg, wa, wb, wo):
    t, d = x2d.shape
    row = lambda w: pl.BlockSpec((TM, w), lambda i: (i, 0))
    return pl.pallas_call(
        _merge_kernel, grid=(t // TM,),
        in_specs=[row(d), row(o_m.shape[1]), row(o_n.shape[1]), row(2 * d),
                  _const_spec(wa.shape), _const_spec(wb.shape), _const_spec(wo.shape)],
        out_specs=row(d), out_shape=jax.ShapeDtypeStruct((t, d), F32),
        compiler_params=_cparams(("arbitrary",)), name="merge",
    )(x2d, o_m, o_n, mg, wa, wb, wo)


def _ffn_kernel(*refs, seq_len, per_seq_state):
    if per_seq_state:
        h = _merged_h(*refs[:7])
        st_ref, gf_ref, win_ref, cw_ref, cb_ref, wd_ref, y_ref, cs_ref, carry_sc = refs[7:]
    else:
        h = refs[0][...]
        st0_ref, st1_ref, gf_ref, win_ref, cw_ref, cb_ref, wd_ref, y_ref, g_ref = refs[1:]
    f = cw_ref.shape[1]
    hn = (h * lax.rsqrt(jnp.mean(h * h, axis=-1, keepdims=True) + RMS_EPS) * gf_ref[...]).astype(BF16)
    gu = _dot(hn, win_ref[...])
    g = gu[:, 0:f]
    u = gu[:, f:2 * f]
    tm = h.shape[0]
    idx = _mod_pow2(lax.broadcasted_iota(jnp.int32, (tm, 1), 0), seq_len)
    if per_seq_state:
        @pl.when(pl.program_id(1) == 0)
        def _():
            carry_sc[0:CONV_W - 1, :] = st_ref[0]
        st0 = carry_sc[0:1, :]
        st1 = carry_sc[1:2, :]
    else:
        st0 = st0_ref[...]
        st1 = st1_ref[...]
    g1 = pltpu.roll(g, 1, 0)
    g2 = pltpu.roll(g, 2, 0)
    prev1 = jnp.where(idx == 0, st1, g1)
    prev2 = jnp.where(idx == 0, st0, jnp.where(idx == 1, st1, g2))
    gc = cb_ref[...] + prev2 * cw_ref[0:1, :] + prev1 * cw_ref[1:2, :] + g * cw_ref[2:3, :]
    act = (jax.nn.silu(gc) * u).astype(BF16)
    y_ref[...] = h + _dot(act, wd_ref[...])
    if per_seq_state:
        carry_sc[0:CONV_W - 1, :] = g[tm - (CONV_W - 1):tm, :]
        cs_ref[0] = g[tm - (CONV_W - 1):tm, :]
    else:
        g_ref[...] = g


def _merge_ffn_prompt(x2d, o_m, o_n, mg, wa, wb, wo, n, conv_state, gf, win, cw, cb, wd):
    t, d = x2d.shape
    f = cw.shape[1]
    nt = (t // n) // FFN_TM
    roww = lambda w: pl.BlockSpec((FFN_TM, w), lambda bi, i: (bi * nt + i, 0))
    row = roww(d)
    st = pl.BlockSpec((1, CONV_W - 1, f), lambda bi, i: (bi, 0, 0))
    cst = lambda a: pl.BlockSpec(a.shape, lambda bi, i, nd=a.ndim: (0,) * nd, pipeline_mode=pl.Buffered(1))
    return pl.pallas_call(
        functools.partial(_ffn_kernel, seq_len=FFN_TM, per_seq_state=True), grid=(n, nt),
        in_specs=[row, roww(o_m.shape[1]), roww(o_n.shape[1]), roww(2 * d), cst(wa), cst(wb), cst(wo),
                  st, cst(gf), cst(win), cst(cw), cst(cb), cst(wd)],
        out_specs=(row, st),
        out_shape=(jax.ShapeDtypeStruct((t, d), F32), jax.ShapeDtypeStruct((n, CONV_W - 1, f), F32)),
        scratch_shapes=[pltpu.VMEM((8, f), F32)],
        compiler_params=_cparams(("arbitrary", "arbitrary")), name="ffn_prompt",
    )(x2d, o_m, o_n, mg, wa, wb, wo, conv_state, gf, win, cw, cb, wd)


def _ffn_sample(h2d, seq_len, st0, st1, gf, win, cw, cb, wd):
    t, d = h2d.shape
    f = cw.shape[1]
    row = lambda w: pl.BlockSpec((TM, w), lambda i: (i, 0))
    return pl.pallas_call(
        functools.partial(_ffn_kernel, seq_len=seq_len, per_seq_state=False), grid=(t // TM,),
        in_specs=[row(d), row(f), row(f), _const_spec(gf.shape), _const_spec(win.shape), _const_spec(cw.shape),
                  _const_spec(cb.shape), _const_spec(wd.shape)],
        out_specs=(row(d), row(f)),
        out_shape=(jax.ShapeDtypeStruct((t, d), F32), jax.ShapeDtypeStruct((t, f), F32)),
        compiler_params=_cparams(("arbitrary",)), name="ffn_sample",
    )(h2d, st0, st1, gf, win, cw, cb, wd)


def _block_diag_ones(n, blk):
    i = np.arange(n)
    return jnp.asarray((i[:, None] // blk == i[None, :] // blk).astype(np.float32), BF16)


def _overlap_matrix(ncp, nc, ns, nsp):
    c0 = np.arange(ncp) * CMP_STRIDE
    j0 = np.arange(nsp) * SEL_BLOCK
    m = (c0[:, None] < j0[None, :] + SEL_BLOCK) & (c0[:, None] + CMP_LEN > j0[None, :])
    m &= (np.arange(ncp)[:, None] < nc) & (np.arange(nsp)[None, :] < ns)
    return m.astype(np.float32)


def _layer_weights(l, attn_norm_g, w_in, qk_norm_g, nsa_cmp_pe, nsa_cmp_w1, nsa_cmp_b1, nsa_cmp_w2, nsa_cmp_b2,
                   w_branch_moba, w_branch_nsa, w_out, ffn_norm_g, w_ffn_in, ffn_conv_w, ffn_conv_b, w_ffn_down):
    d = w_in.shape[1]
    wi = w_in[l]
    kvw = KW
    o_mg = R_END + 3 * NSA_HEADS
    w1t = jnp.transpose(wi[:, :R_END]).astype(BF16)
    wngt = jnp.pad(jnp.transpose(wi[:, R_END:o_mg]), ((0, NG_ROWS - 3 * NSA_HEADS), (0, 0))).astype(BF16)
    wmg = wi[:, o_mg:o_mg + 2 * d].astype(BF16)
    g = qk_norm_g[l]
    ones = lambda n: jnp.ones((n,), F32)
    gcol = jnp.concatenate([jnp.tile(g[0], MOBA_HEADS), jnp.tile(g[1], MOBA_HEADS), ones(MW), jnp.tile(g[2], NSA_HEADS),
                            ones(2 * KW), jnp.tile(g[4], NSA_GROUPS), ones(KW), jnp.tile(g[5], NSA_GROUPS),
                            ones(KW)])[:, None]
    r = CMP_LEN // CMP_STRIDE
    w1c = nsa_cmp_w1[l]
    w1r = w1c.reshape(2, r, CMP_STRIDE, HEAD_DIM, CMP_HIDDEN)
    nkvg = 2 * NSA_GROUPS
    wfull = jnp.zeros((CMP_STRIDE, nkvg, HEAD_DIM, nkvg, r, CMP_HIDDEN), F32)
    for kvg in range(nkvg):
        wfull = wfull.at[:, kvg, :, kvg].set(jnp.transpose(w1r[kvg // NSA_GROUPS], (1, 2, 0, 3)))
    wfull = wfull.reshape(CMP_STRIDE, 2, kvw, 2, NSA_GROUPS * r * CMP_HIDDEN)
    wfull = jnp.stack([wfull[:, kv, :, kv] for kv in range(2)]).reshape(2, CMP_STRIDE * kvw, -1).astype(BF16)
    w2bd = jnp.zeros((nkvg, CMP_HIDDEN, nkvg, HEAD_DIM), F32)
    for kvg in range(nkvg):
        w2bd = w2bd.at[kvg, :, kvg].set(nsa_cmp_w2[l][kvg // NSA_GROUPS])
    w2bd = w2bd.reshape(nkvg * CMP_HIDDEN, nkvg * HEAD_DIM).astype(BF16)
    b2 = nsa_cmp_b2[l]
    cw = dict(
        wfull=wfull,
        pe=jnp.broadcast_to(nsa_cmp_pe[l].reshape(2, 1, CMP_LEN * HEAD_DIM), (2, 8, CMP_LEN * HEAD_DIM)).astype(BF16),
        w1=w1c.astype(BF16), b1=nsa_cmp_b1[l], w2bd=w2bd,
        b2row=jnp.concatenate([jnp.tile(b2[0], NSA_GROUPS), jnp.tile(b2[1], NSA_GROUPS)])[None, :],
        gk=jnp.tile(g[3], NSA_GROUPS)[None, :], bd=_block_diag_ones(kvw, HEAD_DIM))
    return dict(
        g_attn=attn_norm_g[l][None, :], w1t=w1t, wngt=wngt, wmg=wmg, gcol=gcol, cw=cw,
        wa=w_branch_moba[l].astype(BF16), wb=w_branch_nsa[l].astype(BF16), wo=w_out[l].astype(BF16),
        gf=ffn_norm_g[l][None, :], win=w_ffn_in[l].astype(BF16), cwt=ffn_conv_w[l], cb=ffn_conv_b[l][None, :],
        wd=w_ffn_down[l].astype(BF16))


def _prompt_layer(x, wts):
    b, s, d = x.shape
    assert s % (CMP_STRIDE * LANES) == 0 and s % TQ == 0
    t = b * s
    x2d = x.reshape(t, d)
    assert TM == MOBA_BLOCK
    pr = _inproj(x, wts["g_attn"], wts["w1t"], wts["wngt"], wts["wmg"], wts["gcol"])
    mg = pr["mg"]
    nb = s // MOBA_BLOCK
    nbp = -(-nb // 16) * 16
    means = jnp.pad(pr["ksum"].reshape(b, nb, MW) * (1.0 / MOBA_BLOCK), ((0, 0), (0, nbp - nb), (0, 0)))
    m_bias, m_diag, _ = _alibi_tables(MOBA_SLOPES)
    o_m = _moba_prompt(pr["mqt"], pr["mk"], pr["mvt"], means, m_bias, m_diag)

    ncp = s // CMP_STRIDE
    ck, cv = _compress_prompt(pr["ck"], pr["cv"], wts["cw"])
    ns = s // SEL_BLOCK
    nsr = -(-ns // 8) * 8
    ovl_t = jnp.asarray(_overlap_matrix(ncp, ncp - CMP_LEN // CMP_STRIDE + 1, ns, nsr).T, BF16)
    cb = np.asarray(NSA_SLOPES, np.float32)[:, None, None] * (
        (np.arange(ncp) * CMP_STRIDE + CMP_LEN - 1)[None, :, None] - np.arange(TQ)[None, None, :]).astype(np.float32)
    n_bias, n_diag, n_edge = _alibi_tables(NSA_SLOPES)
    o_n = _nsa_prompt(pr["nqt"], ck, jnp.swapaxes(cv, 1, 2), pr["sk"], pr["svt"], pr["wk"], pr["wvt"], pr["ngt"],
                      ovl_t, jnp.asarray(cb), n_bias, n_diag, n_edge)

    f = wts["cwt"].shape[1]
    y2d, conv = _merge_ffn_prompt(x2d, o_m.reshape(t, -1), o_n.reshape(t, -1), mg, wts["wa"], wts["wb"], wts["wo"],
                                  b, jnp.zeros((b, CONV_W - 1, f), F32), wts["gf"], wts["win"], wts["cwt"],
                                  wts["cb"], wts["wd"])
    wrows = min(WINDOW, s)
    rows_view = lambda a, heads: jnp.transpose(a.reshape(b, 2, heads, HEAD_DIM, a.shape[2]), (0, 4, 1, 2, 3))
    outs = (rows_view(pr["mkvt"], MOBA_HEADS), rows_view(pr["ckvt"], NSA_GROUPS), rows_view(pr["skvt"], NSA_GROUPS),
            rows_view(pr["wkvt"][:, :, s - wrows:], NSA_GROUPS), conv)
    return y2d.reshape(b, s, d), outs


def _pages_t(cache):
    dp, npool = cache.shape[0], cache.shape[1]
    return jnp.transpose(cache, (0, 1, 3, 4, 5, 2)).reshape(dp * npool, -1, PAGE)


def _sample_layer(x, wts, pool_moba_t, pool_cmp_t, pool_sel_t, win_state, conv_state, page_table):
    db, ds, d = x.shape
    t = db * ds
    assert t % TM == 0 and ds == 8
    n_pages = page_table.shape[1]
    past = n_pages * PAGE
    x2d = x.reshape(t, d)
    pr = _inproj(x2d[None], wts["g_attn"], wts["w1t"], wts["wngt"], wts["wmg"], wts["gcol"])
    rows = lambda a: jnp.transpose(a[0])
    mq, mkv, nq, ckv, skv, wkv = (rows(pr[k]) for k in ("mqt", "mkvt", "nqt", "ckvt", "skvt", "wkvt"))
    ng = jnp.pad(rows(pr["ngt"]), ((0, 0), (0, LANES - NG_ROWS)))
    mg = pr["mg"]
    mw = MW
    kw = KW
    o_m = _moba_sample(page_table, mq.reshape(db, ds, mw), mkv[:, 0:mw].reshape(db, ds, mw),
                       mkv[:, mw:2 * mw].reshape(db, ds, mw), pool_moba_t.reshape(-1, 2, mw, PAGE))

    assert (past + ds) // CMP_STRIDE == past // CMP_STRIDE
    ncp = past // CMP_STRIDE
    ck, cv = _compress_sample(page_table, pool_cmp_t, wts["cw"])

    nqf = nq.astype(F32).reshape(db, ds, NSA_HEADS, HEAD_DIM).transpose(0, 2, 1, 3)
    zero = jnp.zeros_like(nqf)
    hh = jnp.arange(NSA_HEADS)[None, :, None, None]
    q_grp = jnp.where(hh // NSA_HPG == 0, jnp.concatenate([nqf, zero], -1), jnp.concatenate([zero, nqf], -1))
    q_grp = q_grp.reshape(db, NSA_HEADS * ds, LANES).astype(BF16)
    ns = -(-(past + ds) // SEL_BLOCK)
    nsp = -(-ns // LANES) * LANES
    ovl = jnp.asarray(_overlap_matrix(ncp, ncp - CMP_LEN // CMP_STRIDE + 1, ns, nsp), BF16)
    expand = jnp.asarray((np.arange(nsp)[:, None] == np.arange(past)[None, :] // SEL_BLOCK).astype(np.float32), BF16)
    wb = win_state.shape[1]
    wst_t = jnp.transpose(win_state, (0, 2, 3, 4, 1)).reshape(db, 2 * kw, wb)
    o_n = _nsa_sample(page_table, q_grp, ck, cv, skv.reshape(db, ds, -1), wkv.reshape(db, ds, -1), wst_t,
                      ng.reshape(db, ds, LANES), ovl, expand, pool_sel_t)

    h2d = _merge(x2d, o_m.reshape(t, -1), o_n.reshape(t, -1), mg, wts["wa"], wts["wb"], wts["wo"])
    st0 = jnp.repeat(conv_state[:, 0], ds, axis=0)
    st1 = jnp.repeat(conv_state[:, 1], ds, axis=0)
    y2d, g2d = _ffn_sample(h2d, ds, st0, st1, wts["gf"], wts["win"], wts["cwt"], wts["cb"], wts["wd"])
    gp = jnp.concatenate([conv_state, g2d.reshape(db, ds, -1)], axis=1)
    wcat = jnp.concatenate([win_state, wkv.reshape(db, ds, 2, NSA_GROUPS, HEAD_DIM)], axis=1)
    wrows = min(WINDOW, wcat.shape[1])
    outs = (mkv.reshape(db, ds, 2, MOBA_HEADS, HEAD_DIM), ckv.reshape(db, ds, 2, NSA_GROUPS, HEAD_DIM),
            skv.reshape(db, ds, 2, NSA_GROUPS, HEAD_DIM), wcat[:, wcat.shape[1] - wrows:], gp[:, ds:])
    return y2d.reshape(db, ds, d), outs


def kernel(x_prompt, x_sample, cache_moba_kv, cache_nsa_cmp_kv, cache_nsa_sel_kv, state_nsa_win_kv, state_ffn_conv,
           page_table, attn_norm_g, w_in, qk_norm_g, nsa_cmp_pe, nsa_cmp_w1, nsa_cmp_b1, nsa_cmp_w2, nsa_cmp_b2,
           w_branch_moba, w_branch_nsa, w_out, ffn_norm_g, w_ffn_in, ffn_conv_w, ffn_conv_b, w_ffn_down):
    depth = w_in.shape[0]
    n_pool = cache_moba_kv.shape[1]
    pools = [_pages_t(c) for c in (cache_moba_kv, cache_nsa_cmp_kv, cache_nsa_sel_kv)]
    hp, hs = x_prompt, x_sample
    new = [[] for _ in range(10)]
    for l in range(depth):
        wts = _layer_weights(l, attn_norm_g, w_in, qk_norm_g, nsa_cmp_pe, nsa_cmp_w1, nsa_cmp_b1, nsa_cmp_w2,
                             nsa_cmp_b2, w_branch_moba, w_branch_nsa, w_out, ffn_norm_g, w_ffn_in, ffn_conv_w,
                             ffn_conv_b, w_ffn_down)
        hp, outs_p = _prompt_layer(hp, wts)
        hs, outs_s = _sample_layer(hs, wts, *pools, state_nsa_win_kv[l], state_ffn_conv[l],
                                   page_table + l * n_pool)
        for lst, arr in zip(new[:5], outs_p):
            lst.append(arr)
        for lst, arr in zip(new[5:], outs_s):
            lst.append(arr)
    st = [jnp.stack(v) for v in new]
    return (hp, hs, st[0], st[5], st[1], st[6], st[2], st[7], st[3], st[8], st[4], st[9])
```
